```python
import math
import jax
import jax.numpy as jnp
from jax import lax
import numpy as np


D_MODEL = 1024
BATCH = 16
SEQ = 2048
DEPTH = 4

CTX_LEN = 256
GRID_W = 64
N_MIXERS = 3
EPS = 1e-6

MLA_HEADS = 16
MLA_NOPE = 64
MLA_ROPE = 32
MLA_V = 64
MLA_Q_LORA = 384
MLA_KV_LORA = 256
ROPE_BASE = 10000.0
Q_BLOCK = 128

S5_GROUP = 16
S5_GROUPS = D_MODEL // S5_GROUP
S5_STATE = 64
S5_DT_MIN = 0.001
S5_DT_MAX = 0.1

GLA_HEADS = 4
GLA_KD = D_MODEL // 2
GLA_VD = D_MODEL
GLA_DK = GLA_KD // GLA_HEADS
GLA_DV = GLA_VD // GLA_HEADS
GLA_GATE_RANK = 16
GLA_GATE_NORM = 16.0
GLA_CHUNK = 64

PEER_HEADS = 8
PEER_NKEYS = 128
PEER_EXPERTS = PEER_NKEYS * PEER_NKEYS
PEER_DKEY = 128
PEER_TOPK = 16
PEER_CHUNK = 512

kernel_name = 'hybrid_mla_s5_gla_peer_dit'


def rmsnorm(x, g):
    xf = x.astype(jnp.float32)
    y = xf * lax.rsqrt(jnp.mean(xf * xf, axis=-1, keepdims=True) + EPS)
    return (y * g.astype(jnp.float32)).astype(x.dtype)


def modulate(h, shift, scale):
    return h * (1.0 + scale) + shift


def ada_params(cond, ada_w, ada_b):
    return jnp.split(jax.nn.silu(cond) @ ada_w + ada_b, 6, axis=-1)


def axial_angles(n):
    rows_n = n // GRID_W
    rows = jnp.repeat(jnp.arange(rows_n, dtype=jnp.float32), GRID_W)
    cols = jnp.tile(jnp.arange(GRID_W, dtype=jnp.float32), rows_n)
    half = MLA_ROPE // 2
    inv = ROPE_BASE ** (-jnp.arange(0, half, 2, dtype=jnp.float32) / half)
    return rows[:, None] * inv, cols[:, None] * inv


def rope_half(x, ang):
    m = x.shape[-1] // 2
    x1, x2 = x[..., :m], x[..., m:]
    cos, sin = jnp.cos(ang).astype(x.dtype), jnp.sin(ang).astype(x.dtype)
    return jnp.concatenate([x1 * cos - x2 * sin, x2 * cos + x1 * sin], axis=-1)


def axial_rope(x, ang_r, ang_c):
    h = x.shape[-1] // 2
    return jnp.concatenate([rope_half(x[..., :h], ang_r), rope_half(x[..., h:], ang_c)], axis=-1)


def mla_project(h, w_in, q_norm, w_uq, kv_norm, w_ukv):
    b, n, _ = h.shape
    a = h @ w_in
    cq, ckv, k_pe = jnp.split(a, [MLA_Q_LORA, MLA_Q_LORA + MLA_KV_LORA], axis=-1)
    q = (rmsnorm(cq, q_norm) @ w_uq).reshape(b, n, MLA_HEADS, MLA_NOPE + MLA_ROPE)
    kv = (rmsnorm(ckv, kv_norm) @ w_ukv).reshape(b, n, MLA_HEADS, MLA_NOPE + MLA_V)
    return q[..., :MLA_NOPE], q[..., MLA_NOPE:], kv[..., :MLA_NOPE], k_pe, kv[..., MLA_NOPE:]


def mla_attend(q_nope, q_pe, k_nope, k_pe, v):
    b, nq, h, _ = q_nope.shape
    nb = nq // Q_BLOCK
    scale = (MLA_NOPE + MLA_ROPE) ** -0.5

    def blk(args):
        qn, qp = args
        s = (jnp.einsum('bqhd,bkhd->bhqk', qn, k_nope)
             + jnp.einsum('bqhr,bkr->bhqk', qp, k_pe)).astype(jnp.float32) * scale
        p = jax.nn.softmax(s, axis=-1).astype(v.dtype)
        return jnp.einsum('bhqk,bkhd->bqhd', p, v)

    to_blocks = lambda t: jnp.moveaxis(t.reshape(b, nb, Q_BLOCK, *t.shape[2:]), 1, 0)
    o = lax.map(blk, (to_blocks(q_nope), to_blocks(q_pe)))
    return jnp.moveaxis(o, 0, 1).reshape(b, nq, h * MLA_V)


def mixer_mla(hc, hl, ctx_out, w_in, q_norm, w_uq, kv_norm, w_ukv, w_o):
    n_c = hc.shape[1]
    q_n, q_p, k_n, k_p, v = mla_project(jnp.concatenate([hc, hl], axis=1), w_in, q_norm, w_uq, kv_norm, w_ukv)
    ang_r, ang_c = axial_angles(hl.shape[1])
    lq_p = axial_rope(q_p[:, n_c:], ang_r[:, None], ang_c[:, None])
    k_p = jnp.concatenate([k_p[:, :n_c], axial_rope(k_p[:, n_c:], ang_r, ang_c)], axis=1)
    yl = mla_attend(q_n[:, n_c:], lq_p, k_n, k_p, v) @ w_o
    if not ctx_out:
        return None, yl
    yc = mla_attend(q_n[:, :n_c], q_p[:, :n_c], k_n[:, :n_c], k_p[:, :n_c], v[:, :n_c]) @ w_o
    return yc, yl


def s5_discretise(lam_re, lam_im, b_re, b_im, log_step):
    f32 = jnp.float32
    lr, li = lam_re.astype(f32), lam_im.astype(f32)
    dt = jnp.exp(log_step.astype(f32))[:, None]
    mag = jnp.exp(lr * dt)
    ar, ai = mag * jnp.cos(li * dt), mag * jnp.sin(li * dt)
    den = lr * lr + li * li
    fr = ((ar - 1.0) * lr + ai * li) / den
    fi = (ai * lr - (ar - 1.0) * li) / den
    br_, bi_ = b_re.astype(f32), b_im.astype(f32)
    br = fr[..., None] * br_ - fi[..., None] * bi_
    bi = fr[..., None] * bi_ + fi[..., None] * br_
    return ar, ai, br, bi


def s5_combine(e1, e2):
    a1r, a1i, b1r, b1i = e1
    a2r, a2i, b2r, b2i = e2
    return (a2r * a1r - a2i * a1i, a2r * a1i + a2i * a1r,
            a2r * b1r - a2i * b1i + b2r, a2r * b1i + a2i * b1r + b2i)


def s5_scan(u, h0r, h0i, disc, reverse):
    ar, ai, br, bi = disc
    bur = jnp.einsum('gpc,ngc->ngp', br, u)
    bui = jnp.einsum('gpc,ngc->ngp', bi, u)
    j = -1 if reverse else 0
    bur = bur.at[j].add(ar * h0r - ai * h0i)
    bui = bui.at[j].add(ar * h0i + ai * h0r)
    n = u.shape[0]
    a_r = jnp.broadcast_to(ar, (n,) + ar.shape)
    a_i = jnp.broadcast_to(ai, (n,) + ai.shape)
    _, _, xr, xi = lax.associative_scan(s5_combine, (a_r, a_i, bur, bui), reverse=reverse, axis=0)
    return xr, xi


def s5_readout(c_re, c_im, xr, xi):
    return (jnp.einsum('gcp,ngp->ngc', c_re.astype(jnp.float32), xr)
            - jnp.einsum('gcp,ngp->ngc', c_im.astype(jnp.float32), xi))


def s5_glu(y, u, d_skip, w_glu):
    z = jax.nn.gelu(y + d_skip * u) @ w_glu
    a, g = jnp.split(z, 2, axis=-1)
    return a * jax.nn.sigmoid(g)


def mixer_s5(hc, hl, ctx_out, lam_re, lam_im, b_re, b_im, c_re, c_im, log_step, d_skip, w_glu):
    f32 = jnp.float32
    n_c, n_l = hc.shape[1], hl.shape[1]
    disc = [s5_discretise(lam_re[d], lam_im[d], b_re[d], b_im[d], log_step[d]) for d in range(2)]

    def per_sample(args):
        uc, ul = args
        uc = uc.astype(f32).reshape(n_c, S5_GROUPS, S5_GROUP)
        ul = ul.astype(f32).reshape(n_l, S5_GROUPS, S5_GROUP)
        zero = jnp.zeros((S5_GROUPS, S5_STATE), f32)
        yl = jnp.zeros_like(ul)
        yc = jnp.zeros_like(uc) if ctx_out else None
        for d in range(2):
            rev = d == 1
            fin = 0 if rev else -1
            xcr, xci = s5_scan(uc, zero, zero, disc[d], rev)
            xlr, xli = s5_scan(ul, xcr[fin], xci[fin], disc[d], rev)
            yl = yl + s5_readout(c_re[d], c_im[d], xlr, xli)
            if ctx_out:
                yc = yc + s5_readout(c_re[d], c_im[d], xcr, xci)
        if ctx_out:
            return yc.reshape(n_c, D_MODEL), yl.reshape(n_l, D_MODEL)
        return yl.reshape(n_l, D_MODEL)

    res = lax.map(per_sample, (hc, hl))
    if not ctx_out:
        return None, s5_glu(res.astype(hl.dtype), hl, d_skip, w_glu)
    yc, yl = res
    return (s5_glu(yc.astype(hc.dtype), hc, d_skip, w_glu),
            s5_glu(yl.astype(hl.dtype), hl, d_skip, w_glu))


def gla_chunk_scan(q, k, v, lg, s0):
    b, h, n, _ = q.shape
    nc = n // GLA_CHUNK
    to_chunks = lambda t: jnp.moveaxis(t.reshape(b, h, nc, GLA_CHUNK, t.shape[-1]), 2, 0)
    mask = jnp.tril(jnp.ones((GLA_CHUNK, GLA_CHUNK), dtype=bool))

    def step(s, inp):
        qi, ki, vi, gi = inp
        bcum = jnp.cumsum(gi, axis=2)
        blast = bcum[:, :, -1:, :]
        qg = qi * jnp.exp(bcum)
        kg = ki * jnp.exp(-bcum)
        att = jnp.where(mask, jnp.einsum('bhik,bhjk->bhij', qg, kg), 0.0)
        o = jnp.einsum('bhij,bhjv->bhiv', att, vi) + jnp.einsum('bhik,bhkv->bhiv', qg, s)
        kdec = ki * jnp.exp(blast - bcum)
        s_new = jnp.exp(blast)[:, :, 0, :, None] * s + jnp.einsum('bhjk,bhjv->bhkv', kdec, vi)
        return s_new, o

    s_fin, o = lax.scan(step, s0, (to_chunks(q), to_chunks(k), to_chunks(v), to_chunks(lg)))
    return jnp.moveaxis(o, 0, 2).reshape(b, h, n, v.shape[-1]), s_fin


def mixer_gla(hc, hl, ctx_out, w_in, gk_w1, gk_w2, gk_b, o_norm, w_o):
    f32 = jnp.float32
    n_c = hc.shape[1]
    h = jnp.concatenate([hc, hl], axis=1)
    b, n, _ = h.shape
    q, k, v, g = jnp.split(h @ w_in, [GLA_KD, 2 * GLA_KD, 2 * GLA_KD + GLA_VD], axis=-1)
    heads = lambda t, dh: t.reshape(b, n, GLA_HEADS, dh).transpose(0, 2, 1, 3).astype(f32)
    q = heads(q, GLA_DK) * (GLA_DK ** -0.5)
    k = heads(k, GLA_DK)
    v = heads(v, GLA_DV)
    o_c, o_l = 0.0, 0.0
    for d in range(2):
        lg = jax.nn.log_sigmoid(heads((h @ gk_w1[d]) @ gk_w2[d] + gk_b[d], GLA_DK)) / GLA_GATE_NORM
        parts = (q, k, v, lg)
        ctx_p = [t[:, :, :n_c] for t in parts]
        lat_p = [t[:, :, n_c:] for t in parts]
        if d == 1:
            ctx_p = [jnp.flip(t, 2) for t in ctx_p]
            lat_p = [jnp.flip(t, 2) for t in lat_p]
        s0 = jnp.zeros((b, GLA_HEADS, GLA_DK, GLA_DV), f32)
        oc, sc = gla_chunk_scan(*ctx_p, s0)
        ol, _ = gla_chunk_scan(*lat_p, sc)
        if d == 1:
            oc, ol = jnp.flip(oc, 2), jnp.flip(ol, 2)
        o_c = o_c + oc
        o_l = o_l + ol

    def out(o, gt):
        m = o.shape[2]
        o = rmsnorm(o.transpose(0, 2, 1, 3).astype(h.dtype), o_norm)
        o = o * jax.nn.silu(gt.reshape(b, m, GLA_HEADS, GLA_DV))
        return o.reshape(b, m, GLA_VD) @ w_o

    yl = out(o_l, g[:, n_c:])
    yc = out(o_c, g[:, :n_c]) if ctx_out else None
    return yc, yl


def peer(h, w_q, sub_keys, u_tab, v_tab):
    t, d = h.shape
    pad = (-t) % PEER_CHUNK
    hp = jnp.pad(h, ((0, pad), (0, 0))).reshape(-1, PEER_CHUNK, d)
    kk = PEER_TOPK * PEER_TOPK

    def chunk(xc):
        q = (xc @ w_q).reshape(PEER_CHUNK, PEER_HEADS, 2, PEER_DKEY // 2)
        s = jnp.einsum('thpd,hpkd->thpk', q, sub_keys).astype(jnp.float32)
        sv, si = lax.top_k(s, PEER_TOPK)
        cand = (sv[:, :, 0, :, None] + sv[:, :, 1, None, :]).reshape(PEER_CHUNK, PEER_HEADS, kk)
        cid = (si[:, :, 0, :, None] * PEER_NKEYS + si[:, :, 1, None, :]).reshape(PEER_CHUNK, PEER_HEADS, kk)
        fv, fi = lax.top_k(cand, PEER_TOPK)
        eid = jnp.take_along_axis(cid, fi, axis=-1)
        gate = jax.nn.softmax(fv, axis=-1)
        act = jax.nn.gelu(jnp.einsum('thed,td->the', u_tab[eid], xc).astype(jnp.float32))
        w = (gate * act).astype(xc.dtype)
        return jnp.einsum('the,thed->td', w, v_tab[eid])

    return lax.map(chunk, hp).reshape(-1, d)[:t]


def setup_inputs(seed: int = 0) -> dict:
    key = jax.random.key(seed)
    ks = iter(jax.random.split(key, 40))
    nrm = lambda shape, s: jax.random.normal(next(ks), shape, jnp.float32) * s
    D = D_MODEL
    n_mla, n_s5, n_gla = (DEPTH + 2) // 3, (DEPTH + 1) // 3, DEPTH // 3
    n_idx = jnp.arange(S5_STATE, dtype=jnp.float32)
    return {
        'x': nrm((BATCH, SEQ, D), 1.0),
        'c': nrm((BATCH, D), 1.0),
        'ctx': nrm((BATCH, CTX_LEN, D), 1.0),
        'c_ctx': nrm((D,), 1.0),
        'norm_g': 1.0 + nrm((DEPTH, 2, D), 0.02),
        'ada_w': nrm((DEPTH, D, 6 * D), 0.5 * D ** -0.5),
        'ada_b': nrm((DEPTH, 6 * D), 0.01),
        'mla_w_in': nrm((n_mla, D, MLA_Q_LORA + MLA_KV_LORA + MLA_ROPE), D ** -0.5),
        'mla_q_norm': 1.0 + nrm((n_mla, MLA_Q_LORA), 0.02),
        'mla_w_uq': nrm((n_mla, MLA_Q_LORA, MLA_HEADS * (MLA_NOPE + MLA_ROPE)), MLA_Q_LORA ** -0.5),
        'mla_kv_norm': 1.0 + nrm((n_mla, MLA_KV_LORA), 0.02),
        'mla_w_ukv': nrm((n_mla, MLA_KV_LORA, MLA_HEADS * (MLA_NOPE + MLA_V)), MLA_KV_LORA ** -0.5),
        'mla_w_o': nrm((n_mla, MLA_HEADS * MLA_V, D), (MLA_HEADS * MLA_V) ** -0.5),
        's5_lam_re': -0.5 + nrm((n_s5, 2, S5_GROUPS, S5_STATE), 0.01),
        's5_lam_im': math.pi * n_idx + nrm((n_s5, 2, S5_GROUPS, S5_STATE), 0.01),
        's5_b_re': nrm((n_s5, 2, S5_GROUPS, S5_STATE, S5_GROUP), (2 * S5_GROUP) ** -0.5),
        's5_b_im': nrm((n_s5, 2, S5_GROUPS, S5_STATE, S5_GROUP), (2 * S5_GROUP) ** -0.5),
        's5_c_re': nrm((n_s5, 2, S5_GROUPS, S5_GROUP, S5_STATE), (2 * S5_STATE) ** -0.5),
        's5_c_im': nrm((n_s5, 2, S5_GROUPS, S5_GROUP, S5_STATE), (2 * S5_STATE) ** -0.5),
        's5_log_step': jax.random.uniform(next(ks), (n_s5, 2, S5_GROUPS), jnp.float32,
                                          math.log(S5_DT_MIN), math.log(S5_DT_MAX)),
        's5_d': nrm((n_s5, D), 1.0),
        's5_w_glu': nrm((n_s5, D, 2 * D), D ** -0.5),
        'gla_w_in': nrm((n_gla, D, 2 * GLA_KD + 2 * GLA_VD), D ** -0.5),
        'gla_gk_w1': nrm((n_gla, 2, D, GLA_GATE_RANK), D ** -0.5),
        'gla_gk_w2': nrm((n_gla, 2, GLA_GATE_RANK, GLA_KD), GLA_GATE_RANK ** -0.5),
        'gla_gk_b': nrm((n_gla, 2, GLA_KD), 0.1),
        'gla_o_norm': 1.0 + nrm((n_gla, GLA_DV), 0.02),
        'gla_w_o': nrm((n_gla, GLA_VD, D), GLA_VD ** -0.5),
        'peer_w_q': nrm((DEPTH, D, PEER_HEADS * PEER_DKEY), D ** -0.5),
        'peer_keys': nrm((DEPTH, PEER_HEADS, 2, PEER_NKEYS, PEER_DKEY // 2), (PEER_DKEY // 2) ** -0.5),
        'peer_u': nrm((DEPTH, PEER_EXPERTS, D), D ** -0.5),
        'peer_v': nrm((DEPTH, PEER_EXPERTS, D), 0.5),
        'final_g': 1.0 + nrm((D,), 0.02),
    }


def reference(x, c, ctx, c_ctx, norm_g, ada_w, ada_b,
              mla_w_in, mla_q_norm, mla_w_uq, mla_kv_norm, mla_w_ukv, mla_w_o,
              s5_lam_re, s5_lam_im, s5_b_re, s5_b_im, s5_c_re, s5_c_im, s5_log_step, s5_d, s5_w_glu,
              gla_w_in, gla_gk_w1, gla_gk_w2, gla_gk_b, gla_o_norm, gla_w_o,
              peer_w_q, peer_keys, peer_u, peer_v, final_g):
    xl, xc = x, ctx
    for i in range(DEPTH):
        kind, j = i % N_MIXERS, i // N_MIXERS
        last = i == DEPTH - 1
        sh1, sc1, g1, sh2, sc2, g2 = [t[:, None, :] for t in ada_params(c, ada_w[i], ada_b[i])]
        csh1, csc1, cg1, csh2, csc2, cg2 = ada_params(c_ctx, ada_w[i], ada_b[i])
        hl = modulate(rmsnorm(xl, norm_g[i, 0]), sh1, sc1)
        hc = modulate(rmsnorm(xc, norm_g[i, 0]), csh1, csc1)
        if kind == 0:
            yc, yl = mixer_mla(hc, hl, not last, mla_w_in[j], mla_q_norm[j], mla_w_uq[j],
                               mla_kv_norm[j], mla_w_ukv[j], mla_w_o[j])
        elif kind == 1:
            yc, yl = mixer_s5(hc, hl, not last, s5_lam_re[j], s5_lam_im[j], s5_b_re[j], s5_b_im[j],
                              s5_c_re[j], s5_c_im[j], s5_log_step[j], s5_d[j], s5_w_glu[j])
        else:
            yc, yl = mixer_gla(hc, hl, not last, gla_w_in[j], gla_gk_w1[j], gla_gk_w2[j],
                               gla_gk_b[j], gla_o_norm[j], gla_w_o[j])
        xl = xl + g1 * yl
        hl = modulate(rmsnorm(xl, norm_g[i, 1]), sh2, sc2)
        if last:
            f = peer(hl.reshape(-1, D_MODEL), peer_w_q[i], peer_keys[i], peer_u[i], peer_v[i])
            xl = xl + g2 * f.reshape(xl.shape)
        else:
            xc = xc + cg1 * yc
            hc = modulate(rmsnorm(xc, norm_g[i, 1]), csh2, csc2)
            h_all = jnp.concatenate([hc, hl], axis=1)
            f = peer(h_all.reshape(-1, D_MODEL), peer_w_q[i], peer_keys[i], peer_u[i], peer_v[i]).reshape(h_all.shape)
            n_c = hc.shape[1]
            xc = xc + cg2 * f[:, :n_c]
            xl = xl + g2 * f[:, n_c:]
    return rmsnorm(xl, final_g)
```

```python
import functools
import math

import numpy as np
import jax
import jax.numpy as jnp
from jax import lax
from jax.experimental import pallas as pl
from jax.experimental.pallas import tpu as pltpu

F32 = jnp.float32
BF16 = jnp.bfloat16

EPS = 1e-6
GRID_W = 64
N_MIXERS = 3

MLA_HEADS = 16
MLA_NOPE = 64
MLA_ROPE = 32
MLA_V = 64
MLA_Q_LORA = 384
MLA_KV_LORA = 256
ROPE_BASE = 10000.0
MLA_HEAD_PAD = 128

S5_GROUP = 16
S5_STATE = 64
S5_GROUPS_PER_BLOCK = 8
S5_CHUNK = 128

GLA_HEADS = 4
GLA_GATE_RANK = 16
GLA_GATE_NORM = 16.0
GLA_CHUNK = 64

PEER_HEADS = 8
PEER_NKEYS = 128
PEER_TOPK = 16

TILE = 256
VMEM_LIMIT = 56 * 1024 * 1024


def _params(*sem):
    return pltpu.CompilerParams(dimension_semantics=sem, vmem_limit_bytes=VMEM_LIMIT)


def _gelu_tanh(x):
    return 0.5 * x * (1.0 + jnp.tanh(0.7978845608028654 * (x + 0.044715 * x * x * x)))


def _is_ctx(tile_idx, rows, n_ctx):
    pos = tile_idx * rows + lax.broadcasted_iota(jnp.int32, (rows, 1), 0)
    return pos < n_ctx


def _mod_row(m_ref, row, is_ctx):
    return jnp.where(is_ctx, m_ref[0, row:row + 1, :], m_ref[1, row:row + 1, :])


def _rms(x, g):
    return x * lax.rsqrt(jnp.mean(x * x, axis=-1, keepdims=True) + EPS) * g


def _norm_mod(x, g, m_ref, row0, is_ctx):
    return _rms(x, g) * (1.0 + _mod_row(m_ref, row0 + 1, is_ctx)) + _mod_row(m_ref, row0, is_ctx)


def _dot(a, b):
    return jnp.dot(a, b, preferred_element_type=F32)


def _dot_nt(a, b):
    return lax.dot_general(a, b, (((1,), (1,)), ((), ())), preferred_element_type=F32)


def _ada_kernel(c_ref, w_ref, b_ref, o_ref):
    c = c_ref[...]
    s = (c * jax.nn.sigmoid(c)).astype(BF16)
    o_ref[...] = _dot(s, w_ref[...].astype(BF16)) + b_ref[...]


def ada_all(cond, ada_w, ada_b):
    depth, d, n6 = ada_w.shape
    r = cond.shape[0]
    tn = 1024
    return pl.pallas_call(
        _ada_kernel,
        grid=(depth, n6 // tn),
        in_specs=[
            pl.BlockSpec((r, d), lambda l, j: (0, 0)),
            pl.BlockSpec((None, d, tn), lambda l, j: (l, 0, j)),
            pl.BlockSpec((None, 1, tn), lambda l, j: (l, 0, j)),
        ],
        out_specs=pl.BlockSpec((None, r, tn), lambda l, j: (l, 0, j)),
        out_shape=jax.ShapeDtypeStruct((depth, r, n6), F32),
        compiler_params=_params("arbitrary", "arbitrary"),
        name="ada",
    )(cond, ada_w, ada_b.reshape(depth, 1, n6))


def _proj_residual_kernel(n_ctx, gate_row, x_ref, y_ref, w_ref, m_ref, o_ref):
    is_ctx = _is_ctx(pl.program_id(1), x_ref.shape[0], n_ctx)
    f = _dot(y_ref[...], w_ref[...])
    o_ref[...] = x_ref[...] + _mod_row(m_ref, gate_row, is_ctx) * f


def proj_residual(xs, y, w, mods, n_ctx, gate_row):
    b, n, d = xs.shape
    k = y.shape[-1]
    return pl.pallas_call(
        functools.partial(_proj_residual_kernel, n_ctx, gate_row),
        grid=(b, n // TILE),
        in_specs=[
            pl.BlockSpec((None, TILE, d), lambda i, t: (i, t, 0)),
            pl.BlockSpec((None, TILE, k), lambda i, t: (i, t, 0)),
            pl.BlockSpec((k, d), lambda i, t: (0, 0)),
            pl.BlockSpec((2, 8, d), lambda i, t: (i, 0, 0)),
        ],
        out_specs=pl.BlockSpec((None, TILE, d), lambda i, t: (i, t, 0)),
        out_shape=jax.ShapeDtypeStruct(xs.shape, F32),
        input_output_aliases={0: 0},
        compiler_params=_params("arbitrary", "arbitrary"),
        name="proj_residual",
    )(xs, y, w, mods)


def _rope_tables(n_ctx, n_lat, scale):
    half = MLA_ROPE // 2
    rows_n = n_lat // GRID_W
    rows = np.repeat(np.arange(rows_n, dtype=np.float32), GRID_W)
    cols = np.tile(np.arange(GRID_W, dtype=np.float32), rows_n)
    inv = (ROPE_BASE ** (-np.arange(0, half, 2, dtype=np.float32) / half)).astype(np.float32)
    ang_r = rows[:, None] * inv
    ang_c = cols[:, None] * inv
    cos = np.concatenate([np.cos(ang_r), np.cos(ang_r), np.cos(ang_c), np.cos(ang_c)], axis=1)
    sin = np.concatenate([-np.sin(ang_r), np.sin(ang_r), -np.sin(ang_c), np.sin(ang_c)], axis=1)
    n = n_ctx + n_lat
    a = np.zeros((n, MLA_HEAD_PAD), np.float32)
    b = np.zeros((n, MLA_HEAD_PAD), np.float32)
    a[:, :MLA_NOPE] = 1.0
    a[:n_ctx, MLA_NOPE:MLA_NOPE + MLA_ROPE] = 1.0
    a[n_ctx:, MLA_NOPE:MLA_NOPE + MLA_ROPE] = cos
    b[n_ctx:, MLA_NOPE:MLA_NOPE + MLA_ROPE] = sin
    return jnp.asarray(a * scale), jnp.asarray(b * scale)


def _rope_swap_index():
    q = MLA_ROPE // 4
    base = np.arange(MLA_ROPE)
    return np.where((base % (2 * q)) < q, base + q, base - q)


def _mla_in_kernel(n_ctx, x_ref, g_ref, m_ref, wq_ref, wkv_ref, wkp_ref, qn_ref, kvn_ref,
                   wuq1_ref, wuq2_ref, wuk_ref, wuv_ref, aq_ref, bq_ref, ak_ref, bk_ref,
                   q_ref, k_ref, v_ref):
    is_ctx = _is_ctx(pl.program_id(1), x_ref.shape[0], n_ctx)
    hb = _norm_mod(x_ref[...], g_ref[...], m_ref, 0, is_ctx).astype(BF16)
    cq = _rms(_dot(hb, wq_ref[...]), qn_ref[...]).astype(BF16)
    ckv = _rms(_dot(hb, wkv_ref[...]), kvn_ref[...]).astype(BF16)
    kp = _dot(hb, wkp_ref[...])
    kpe = kp[:, :MLA_HEAD_PAD] * ak_ref[...] + kp[:, MLA_HEAD_PAD:] * bk_ref[...]
    y1 = _dot(cq, wuq1_ref[...])
    y2 = _dot(cq, wuq2_ref[...])
    kk = _dot(ckv, wuk_ref[...])
    aq = aq_ref[...]
    bq = bq_ref[...]
    for h in range(MLA_HEADS):
        sl = slice(h * MLA_HEAD_PAD, (h + 1) * MLA_HEAD_PAD)
        q_ref[:, sl] = (y1[:, sl] * aq + y2[:, sl] * bq).astype(BF16)
        k_ref[:, sl] = (kk[:, sl] + kpe).astype(BF16)
    v_ref[...] = _dot(ckv, wuv_ref[...]).astype(BF16)


def _mla_attn_kernel(n_ctx, q_ref, k_ref, v_ref, o_ref):
    n = k_ref.shape[0]

    def attend(nk):
        outs = []
        for hh in range(2):
            sl = slice(hh * MLA_HEAD_PAD, (hh + 1) * MLA_HEAD_PAD)
            s = _dot_nt(q_ref[:, sl], k_ref[0:nk, sl])
            p = jnp.exp(s - jnp.max(s, axis=-1, keepdims=True))
            l = jnp.sum(p, axis=-1, keepdims=True)
            outs.append(_dot(p.astype(BF16), v_ref[0:nk, :]) / l)
        lane = lax.broadcasted_iota(jnp.int32, outs[0].shape, 1)
        o_ref[...] = jnp.where(lane < MLA_V, outs[0], outs[1]).astype(BF16)

    @pl.when(pl.program_id(2) * q_ref.shape[0] < n_ctx)
    def _():
        attend(n_ctx)

    @pl.when(pl.program_id(2) * q_ref.shape[0] >= n_ctx)
    def _():
        attend(n)


def mixer_mla(xs, mods, g, n_ctx, w_in, q_norm, w_uq, kv_norm, w_ukv, w_o):
    b, n, d = xs.shape
    hp = MLA_HEAD_PAD
    dq = MLA_NOPE + MLA_ROPE
    wq = w_in[:, :MLA_Q_LORA].astype(BF16)
    wkv = w_in[:, MLA_Q_LORA:MLA_Q_LORA + MLA_KV_LORA].astype(BF16)
    w_pe = w_in[:, MLA_Q_LORA + MLA_KV_LORA:]
    swap = _rope_swap_index()
    wkp = jnp.zeros((d, 2 * hp), F32)
    wkp = wkp.at[:, MLA_NOPE:dq].set(w_pe).at[:, hp + MLA_NOPE:hp + dq].set(w_pe[:, swap]).astype(BF16)
    uq = w_uq.reshape(MLA_Q_LORA, MLA_HEADS, dq)
    z = jnp.zeros((MLA_Q_LORA, MLA_HEADS, hp - dq), F32)
    wuq1 = jnp.concatenate([uq, z], axis=-1).reshape(MLA_Q_LORA, MLA_HEADS * hp).astype(BF16)
    zn = jnp.zeros((MLA_Q_LORA, MLA_HEADS, MLA_NOPE), F32)
    wuq2 = jnp.concatenate([zn, uq[:, :, MLA_NOPE:][:, :, swap], z], axis=-1)
    wuq2 = wuq2.reshape(MLA_Q_LORA, MLA_HEADS * hp).astype(BF16)
    ukv = w_ukv.reshape(MLA_KV_LORA, MLA_HEADS, MLA_NOPE + MLA_V)
    zk = jnp.zeros((MLA_KV_LORA, MLA_HEADS, hp - MLA_NOPE), F32)
    wuk = jnp.concatenate([ukv[:, :, :MLA_NOPE], zk], axis=-1).reshape(MLA_KV_LORA, MLA_HEADS * hp).astype(BF16)
    wuv = ukv[:, :, MLA_NOPE:].reshape(MLA_KV_LORA, MLA_HEADS * MLA_V).astype(BF16)
    aq, bq = _rope_tables(n_ctx, n - n_ctx, float(dq) ** -0.5)
    ak, bk = _rope_tables(n_ctx, n - n_ctx, 1.0)

    full = lambda shape: pl.BlockSpec(shape, lambda i, t: tuple(0 for _ in shape))
    tab = pl.BlockSpec((TILE, hp), lambda i, t: (t, 0))
    q, k, v = pl.pallas_call(
        functools.partial(_mla_in_kernel, n_ctx),
        grid=(b, n // TILE),
        in_specs=[
            pl.BlockSpec((None, TILE, d), lambda i, t: (i, t, 0)),
            full((1, d)),
            pl.BlockSpec((2, 8, d), lambda i, t: (i, 0, 0)),
            full(wq.shape), full(wkv.shape), full(wkp.shape),
            full((1, MLA_Q_LORA)), full((1, MLA_KV_LORA)),
            full(wuq1.shape), full(wuq2.shape), full(wuk.shape), full(wuv.shape),
            tab, tab, tab, tab,
        ],
        out_specs=[
            pl.BlockSpec((None, TILE, MLA_HEADS * hp), lambda i, t: (i, t, 0)),
            pl.BlockSpec((None, TILE, MLA_HEADS * hp), lambda i, t: (i, t, 0)),
            pl.BlockSpec((None, TILE, MLA_HEADS * MLA_V), lambda i, t: (i, t, 0)),
        ],
        out_shape=[
            jax.ShapeDtypeStruct((b, n, MLA_HEADS * hp), BF16),
            jax.ShapeDtypeStruct((b, n, MLA_HEADS * hp), BF16),
            jax.ShapeDtypeStruct((b, n, MLA_HEADS * MLA_V), BF16),
        ],
        compiler_params=_params("arbitrary", "arbitrary"),
        name="mla_in",
    )(xs, g.reshape(1, d), mods, wq, wkv, wkp, q_norm.reshape(1, -1), kv_norm.reshape(1, -1),
      wuq1, wuq2, wuk, wuv, aq, bq, ak, bk)

    o = pl.pallas_call(
        functools.partial(_mla_attn_kernel, n_ctx),
        grid=(b, MLA_HEADS // 2, n // TILE),
        in_specs=[
            pl.BlockSpec((None, TILE, 2 * hp), lambda i, h, t: (i, t, h)),
            pl.BlockSpec((None, n, 2 * hp), lambda i, h, t: (i, 0, h)),
            pl.BlockSpec((None, n, 2 * MLA_V), lambda i, h, t: (i, 0, h)),
        ],
        out_specs=pl.BlockSpec((None, TILE, 2 * MLA_V), lambda i, h, t: (i, t, h)),
        out_shape=jax.ShapeDtypeStruct((b, n, MLA_HEADS * MLA_V), BF16),
        compiler_params=_params("arbitrary", "arbitrary", "arbitrary"),
        name="mla_attn",
    )(q, k, v)
    return proj_residual(xs, o, w_o.astype(BF16), mods, n_ctx, 2)


def _s5_disc_kernel(lr_ref, li_ref, ls_ref, bre_ref, bim_ref, ar_ref, ai_ref, br_ref, bi_ref):
    lr = lr_ref[...]
    li = li_ref[...]
    dt = jnp.exp(ls_ref[...])
    mag = jnp.exp(lr * dt)
    ar = mag * jnp.cos(li * dt)
    ai = mag * jnp.sin(li * dt)
    den = lr * lr + li * li
    fr = ((ar - 1.0) * lr + ai * li) / den
    fi = (ai * lr - (ar - 1.0) * li) / den
    ar_ref[...] = ar
    ai_ref[...] = ai
    for c in range(S5_GROUP):
        br_ref[c] = fr * bre_ref[c] - fi * bim_ref[c]
        bi_ref[c] = fr * bim_ref[c] + fi * bre_ref[c]


def _s5_in_kernel(n_ctx, x_ref, g_ref, m_ref, u_ref):
    is_ctx = _is_ctx(pl.program_id(1), x_ref.shape[0], n_ctx)
    u_ref[...] = _norm_mod(x_ref[...], g_ref[...], m_ref, 0, is_ctx).astype(BF16)


def _s5_scan_kernel(nb, u_ref, bcat_ref, ar_ref, ai_ref, ccat_ref, y_ref, bu_ref, xs_ref, st_ref):
    d = pl.program_id(0)
    half = ar_ref.shape[-1]

    @pl.when(pl.program_id(2) == 0)
    def _():
        st_ref[...] = jnp.zeros_like(st_ref)

    bu_ref[...] = _dot(u_ref[...], bcat_ref[...])
    ar = jnp.broadcast_to(ar_ref[...], (nb, half))
    ai = jnp.broadcast_to(ai_ref[...], (nb, half))
    steps = u_ref.shape[0] // nb

    def step(i, carry):
        xr, xi = carry
        tt = jnp.where(d == 0, i, steps - 1 - i)
        r0 = pl.multiple_of(tt * nb, nb)
        nxr = ar * xr - ai * xi + bu_ref[pl.ds(r0, nb), 0:half]
        nxi = ar * xi + ai * xr + bu_ref[pl.ds(r0, nb), half:2 * half]
        xs_ref[pl.ds(r0, nb), 0:half] = nxr.astype(BF16)
        xs_ref[pl.ds(r0, nb), half:2 * half] = nxi.astype(BF16)
        return nxr, nxi

    xr, xi = lax.fori_loop(0, steps, step, (st_ref[:, 0:half], st_ref[:, half:2 * half]))
    st_ref[:, 0:half] = xr
    st_ref[:, half:2 * half] = xi
    y_ref[...] = _dot(xs_ref[...], ccat_ref[...])


def _s5_glu_kernel(n_ctx, x_ref, g_ref, m_ref, y_ref, dsk_ref, w_ref, o_ref):
    is_ctx = _is_ctx(pl.program_id(1), x_ref.shape[0], n_ctx)
    x = x_ref[...]
    u = _norm_mod(x, g_ref[...], m_ref, 0, is_ctx)
    y = y_ref[0] + y_ref[1]
    z = _dot(_gelu_tanh(y + dsk_ref[...] * u).astype(BF16), w_ref[...])
    dm = z.shape[-1] // 2
    out = z[:, :dm] * jax.nn.sigmoid(z[:, dm:])
    o_ref[...] = x + _mod_row(m_ref, 2, is_ctx) * out


def mixer_s5(xs, mods, g, n_ctx, lam_re, lam_im, b_re, b_im, c_re, c_im, log_step, d_skip, w_glu):
    b, n, d = xs.shape
    groups = d // S5_GROUP
    p = S5_STATE
    gb = S5_GROUPS_PER_BLOCK
    nblk = groups // gb
    cin = gb * S5_GROUP
    half = gb * p

    full = lambda shape: pl.BlockSpec(shape, lambda dd: tuple(0 for _ in shape))
    ar, ai, br, bi = pl.pallas_call(
        _s5_disc_kernel,
        grid=(2,),
        in_specs=[
            pl.BlockSpec((None, groups, p), lambda dd: (dd, 0, 0)),
            pl.BlockSpec((None, groups, p), lambda dd: (dd, 0, 0)),
            pl.BlockSpec((None, groups, 1), lambda dd: (dd, 0, 0)),
            pl.BlockSpec((None, S5_GROUP, groups, p), lambda dd: (dd, 0, 0, 0)),
            pl.BlockSpec((None, S5_GROUP, groups, p), lambda dd: (dd, 0, 0, 0)),
        ],
        out_specs=[
            pl.BlockSpec((None, groups, p), lambda dd: (dd, 0, 0)),
            pl.BlockSpec((None, groups, p), lambda dd: (dd, 0, 0)),
            pl.BlockSpec((None, S5_GROUP, groups, p), lambda dd: (dd, 0, 0, 0)),
            pl.BlockSpec((None, S5_GROUP, groups, p), lambda dd: (dd, 0, 0, 0)),
        ],
        out_shape=[
            jax.ShapeDtypeStruct((2, groups, p), F32),
            jax.ShapeDtypeStruct((2, groups, p), F32),
            jax.ShapeDtypeStruct((2, S5_GROUP, groups, p), F32),
            jax.ShapeDtypeStruct((2, S5_GROUP, groups, p), F32),
        ],
        compiler_params=_params("arbitrary"),
        name="s5_disc",
    )(lam_re, lam_im, log_step.reshape(2, groups, 1),
      jnp.transpose(b_re, (0, 3, 1, 2)), jnp.transpose(b_im, (0, 3, 1, 2)))

    eye = jnp.eye(gb, dtype=F32)

    def in_blocks(t):
        t = t.reshape(2, S5_GROUP, nblk, gb, p)
        return jnp.einsum('ab,dcjap->djacbp', eye, t).reshape(2, nblk, cin, half)

    bcat = jnp.concatenate([in_blocks(br), in_blocks(bi)], axis=-1).astype(BF16)

    def out_blocks(t):
        t = t.reshape(2, nblk, gb, S5_GROUP, p)
        return jnp.einsum('ab,djacp->djapbc', eye, t).reshape(2, nblk, half, cin)

    ccat = jnp.concatenate([out_blocks(c_re), -out_blocks(c_im)], axis=2).astype(BF16)
    ar_b = ar.reshape(2, nblk, 1, half)
    ai_b = ai.reshape(2, nblk, 1, half)

    u_tm = pl.pallas_call(
        functools.partial(_s5_in_kernel, n_ctx),
        grid=(b, n // TILE),
        in_specs=[
            pl.BlockSpec((None, TILE, d), lambda i, t: (i, t, 0)),
            pl.BlockSpec((1, d), lambda i, t: (0, 0)),
            pl.BlockSpec((2, 8, d), lambda i, t: (i, 0, 0)),
        ],
        out_specs=pl.BlockSpec((TILE, d), lambda i, t: (t, i)),
        out_shape=jax.ShapeDtypeStruct((n, b * d), BF16),
        compiler_params=_params("arbitrary", "arbitrary"),
        name="s5_in",
    )(xs, g.reshape(1, d), mods)

    tc = S5_CHUNK
    nchunks = n // tc
    ncc = n_ctx // tc
    rows = tc * b

    def chunk_of(dd, s):
        rev = jnp.where(s < ncc, ncc - 1 - s, nchunks - 1 - (s - ncc))
        return jnp.where(dd == 0, s, rev)

    y2 = pl.pallas_call(
        functools.partial(_s5_scan_kernel, b),
        grid=(2, nblk, nchunks),
        in_specs=[
            pl.BlockSpec((rows, cin), lambda dd, j, s: (chunk_of(dd, s), j)),
            pl.BlockSpec((None, None, cin, 2 * half), lambda dd, j, s: (dd, j, 0, 0)),
            pl.BlockSpec((None, None, 1, half), lambda dd, j, s: (dd, j, 0, 0)),
            pl.BlockSpec((None, None, 1, half), lambda dd, j, s: (dd, j, 0, 0)),
            pl.BlockSpec((None, None, 2 * half, cin), lambda dd, j, s: (dd, j, 0, 0)),
        ],
        out_specs=pl.BlockSpec((None, rows, cin), lambda dd, j, s: (dd, chunk_of(dd, s), j)),
        out_shape=jax.ShapeDtypeStruct((2, n * b, d), F32),
        scratch_shapes=[
            pltpu.VMEM((rows, 2 * half), F32),
            pltpu.VMEM((rows, 2 * half), BF16),
            pltpu.VMEM((b, 2 * half), F32),
        ],
        compiler_params=_params("arbitrary", "arbitrary", "arbitrary"),
        name="s5_scan",
    )(u_tm.reshape(n * b, d), bcat, ar_b, ai_b, ccat)

    return pl.pallas_call(
        functools.partial(_s5_glu_kernel, n_ctx),
        grid=(b, n // TILE),
        in_specs=[
            pl.BlockSpec((None, TILE, d), lambda i, t: (i, t, 0)),
            pl.BlockSpec((1, d), lambda i, t: (0, 0)),
            pl.BlockSpec((2, 8, d), lambda i, t: (i, 0, 0)),
            pl.BlockSpec((2, TILE, d), lambda i, t: (0, t, i)),
            pl.BlockSpec((1, d), lambda i, t: (0, 0)),
            pl.BlockSpec((d, 2 * d), lambda i, t: (0, 0)),
        ],
        out_specs=pl.BlockSpec((None, TILE, d), lambda i, t: (i, t, 0)),
        out_shape=jax.ShapeDtypeStruct(xs.shape, F32),
        input_output_aliases={0: 0},
        compiler_params=_params("arbitrary", "arbitrary"),
        name="s5_glu",
    )(xs, g.reshape(1, d), mods, y2.reshape(2, n, b * d), d_skip.reshape(1, d), w_glu.astype(BF16))


def _gla_in_kernel(n_ctx, x_ref, g_ref, m_ref, w_ref, w1_ref, o_ref, r_ref):
    is_ctx = _is_ctx(pl.program_id(1), x_ref.shape[0], n_ctx)
    hb = _norm_mod(x_ref[...], g_ref[...], m_ref, 0, is_ctx).astype(BF16)
    o_ref[...] = _dot(hb, w_ref[...]).astype(BF16)
    r_ref[...] = _dot(hb, w1_ref[...])


def _gla_scan_kernel(q_ref, k_ref, v_ref, r_ref, w2_ref, gb_ref, o_ref, s_ref):
    d = pl.program_id(2)
    dk = q_ref.shape[-1]
    c = GLA_CHUNK
    nck = q_ref.shape[0] // c

    @pl.when(pl.program_id(3) == 0)
    def _():
        s_ref[...] = jnp.zeros_like(s_ref)

    row = lax.broadcasted_iota(jnp.int32, (c, c), 0)
    col = lax.broadcasted_iota(jnp.int32, (c, c), 1)
    mask = jnp.where(d == 0, row - col, col - row) >= 0
    tri = mask.astype(F32)

    def chunk(i, _):
        ci = jnp.where(d == 0, i, nck - 1 - i)
        r0 = pl.multiple_of(ci * c, c)
        gl = _dot(r_ref[pl.ds(r0, c), :].astype(BF16), w2_ref[...]) + gb_ref[...]
        lg = jax.nn.log_sigmoid(gl) * (1.0 / GLA_GATE_NORM)
        bcum = jnp.dot(tri, lg, preferred_element_type=F32, precision=lax.Precision.HIGHEST)
        blast = jnp.sum(lg, axis=0, keepdims=True)
        q = q_ref[pl.ds(r0, c), :].astype(F32) * (float(dk) ** -0.5)
        k = k_ref[pl.ds(r0, c), :].astype(F32)
        v = v_ref[pl.ds(r0, c), :]
        qg = (q * jnp.exp(bcum)).astype(BF16)
        kg = (k * jnp.exp(-bcum)).astype(BF16)
        kdec = k * jnp.exp(blast - bcum)
        att = jnp.where(mask, _dot_nt(qg, kg), 0.0)
        s = s_ref[...]
        o_ref[pl.ds(r0, c), :] = _dot(att.astype(BF16), v) + _dot(qg, s.astype(BF16))
        m = jnp.concatenate([kdec, jnp.broadcast_to(jnp.exp(blast), (dk - c, dk))], axis=0)
        mt = m.T
        s_ref[...] = mt[:, c:c + 1] * s + _dot(mt[:, 0:c].astype(BF16), v)
        return 0

    lax.fori_loop(0, nck, chunk, 0)


def _gla_out_kernel(n_ctx, x_ref, o2_ref, gt_ref, on_ref, w_ref, m_ref, o_ref):
    is_ctx = _is_ctx(pl.program_id(1), x_ref.shape[0], n_ctx)
    o = o2_ref[0] + o2_ref[1]
    gt = gt_ref[...].astype(F32)
    dv = on_ref.shape[-1]
    parts = []
    for h in range(GLA_HEADS):
        sl = slice(h * dv, (h + 1) * dv)
        parts.append(_rms(o[:, sl], on_ref[...]) * (gt[:, sl] * jax.nn.sigmoid(gt[:, sl])))
    y = jnp.concatenate(parts, axis=-1).astype(BF16)
    o_ref[...] = x_ref[...] + _mod_row(m_ref, 2, is_ctx) * _dot(y, w_ref[...])


def mixer_gla(xs, mods, g, n_ctx, w_in, gk_w1, gk_w2, gk_b, o_norm, w_o):
    b, n, d = xs.shape
    kd = d // 2
    vd = d
    dk = kd // GLA_HEADS
    dv = vd // GLA_HEADS
    rk = GLA_GATE_RANK
    w1 = jnp.zeros((d, 128), F32).at[:, :rk].set(gk_w1[0]).at[:, rk:2 * rk].set(gk_w1[1]).astype(BF16)
    w2 = jnp.zeros((2, GLA_HEADS, 128, dk), F32)
    for dd in range(2):
        w2 = w2.at[dd, :, dd * rk:(dd + 1) * rk, :].set(
            jnp.transpose(gk_w2[dd].reshape(rk, GLA_HEADS, dk), (1, 0, 2)))
    w2 = w2.astype(BF16)
    gbias = gk_b.reshape(2, GLA_HEADS, 1, dk)
    nw = w_in.shape[-1]

    qkvg, r = pl.pallas_call(
        functools.partial(_gla_in_kernel, n_ctx),
        grid=(b, n // TILE),
        in_specs=[
            pl.BlockSpec((None, TILE, d), lambda i, t: (i, t, 0)),
            pl.BlockSpec((1, d), lambda i, t: (0, 0)),
            pl.BlockSpec((2, 8, d), lambda i, t: (i, 0, 0)),
            pl.BlockSpec((d, nw), lambda i, t: (0, 0)),
            pl.BlockSpec((d, 128), lambda i, t: (0, 0)),
        ],
        out_specs=[
            pl.BlockSpec((None, TILE, nw), lambda i, t: (i, t, 0)),
            pl.BlockSpec((None, TILE, 128), lambda i, t: (i, t, 0)),
        ],
        out_shape=[
            jax.ShapeDtypeStruct((b, n, nw), BF16),
            jax.ShapeDtypeStruct((b, n, 128), F32),
        ],
        compiler_params=_params("arbitrary", "arbitrary"),
        name="gla_in",
    )(xs, g.reshape(1, d), mods, w_in.astype(BF16), w1)

    nt = n // TILE
    nct = n_ctx // TILE

    def tile_of(dd, s):
        rev = jnp.where(s < nct, nct - 1 - s, nt - 1 - (s - nct))
        return jnp.where(dd == 0, s, rev)

    o2 = pl.pallas_call(
        _gla_scan_kernel,
        grid=(b, GLA_HEADS, 2, nt),
        in_specs=[
            pl.BlockSpec((None, TILE, dk), lambda i, h, dd, s: (i, tile_of(dd, s), h)),
            pl.BlockSpec((None, TILE, dk), lambda i, h, dd, s: (i, tile_of(dd, s), kd // dk + h)),
            pl.BlockSpec((None, TILE, dv), lambda i, h, dd, s: (i, tile_of(dd, s), 2 * kd // dv + h)),
            pl.BlockSpec((None, TILE, 128), lambda i, h, dd, s: (i, tile_of(dd, s), 0)),
            pl.BlockSpec((None, None, 128, dk), lambda i, h, dd, s: (dd, h, 0, 0)),
            pl.BlockSpec((None, None, 1, dk), lambda i, h, dd, s: (dd, h, 0, 0)),
        ],
        out_specs=pl.BlockSpec((None, None, TILE, dv), lambda i, h, dd, s: (dd, i, tile_of(dd, s), h)),
        out_shape=jax.ShapeDtypeStruct((2, b, n, vd), F32),
        scratch_shapes=[pltpu.VMEM((dk, dv), F32)],
        compiler_params=_params("arbitrary", "arbitrary", "arbitrary", "arbitrary"),
        name="gla_scan",
    )(qkvg, qkvg, qkvg, r, w2, gbias)

    return pl.pallas_call(
        functools.partial(_gla_out_kernel, n_ctx),
        grid=(b, nt),
        in_specs=[
            pl.BlockSpec((None, TILE, d), lambda i, t: (i, t, 0)),
            pl.BlockSpec((2, None, TILE, vd), lambda i, t: (0, i, t, 0)),
            pl.BlockSpec((None, TILE, vd), lambda i, t: (i, t, (2 * kd + vd) // vd)),
            pl.BlockSpec((1, dv), lambda i, t: (0, 0)),
            pl.BlockSpec((vd, d), lambda i, t: (0, 0)),
            pl.BlockSpec((2, 8, d), lambda i, t: (i, 0, 0)),
        ],
        out_specs=pl.BlockSpec((None, TILE, d), lambda i, t: (i, t, 0)),
        out_shape=jax.ShapeDtypeStruct(xs.shape, F32),
        input_output_aliases={0: 0},
        compiler_params=_params("arbitrary", "arbitrary"),
        name="gla_out",
    )(xs, o2, qkvg, o_norm.reshape(1, dv), w_o.astype(BF16), mods)


PEER_TT = 768
PEER_EB = 1024
PEER_RANKS = PEER_TOPK + 1


def _top_rows(s, k):
    rows = []
    for _ in range(k):
        m = jnp.max(s, axis=0, keepdims=True)
        rows.append(m)
        s = jnp.where(s == m, -jnp.inf, s)
    return rows


def _stack_rows(rows, nrows):
    t = rows[0].shape[-1]
    ridx = lax.broadcasted_iota(jnp.int32, (nrows, t), 0)
    out = jnp.full((nrows, t), -jnp.inf, F32)
    for r, v in enumerate(rows):
        out = jnp.where(ridx == r, v, out)
    return out


def _pair_threshold(a, b):
    k = PEER_RANKS
    t = a[0].shape[-1]
    nr = 8 * ((k + 7) // 8)
    bcol = _stack_rows(b, nr)
    acol = _stack_rows(a, nr)
    slabs = [a[0] + bcol]
    row8 = lax.broadcasted_iota(jnp.int32, (8, t), 0)
    for i in range(2, 9):
        slabs.append(jnp.where(row8 < k // i, a[i - 1] + bcol[0:8], -jnp.inf))
    slabs.append(acol[8:nr] + b[0])
    return _top_rows(jnp.concatenate(slabs, axis=0), k)


def _peer_score_kernel(n_ctx, x_ref, g_ref, m_ref, wqt_ref, keys_ref, h_ref, e0_ref, e1_ref, th_ref):
    tt = x_ref.shape[0]
    is_ctx = _is_ctx(pl.program_id(1), tt, n_ctx)
    hb = _norm_mod(x_ref[...], g_ref[...], m_ref, 3, is_ctx).astype(BF16)
    h_ref[...] = hb
    qt = _dot_nt(wqt_ref[...], hb).astype(BF16)
    dkey = keys_ref.shape[-1]
    for hd in range(PEER_HEADS):
        s0 = _dot(keys_ref[2 * hd], qt[(2 * hd) * dkey:(2 * hd + 1) * dkey, :])
        s1 = _dot(keys_ref[2 * hd + 1], qt[(2 * hd + 1) * dkey:(2 * hd + 2) * dkey, :])
        a = _top_rows(s0, PEER_RANKS)
        b = _top_rows(s1, PEER_RANKS)
        v = _pair_threshold(a, b)
        tau = 0.5 * (v[PEER_TOPK - 1] + v[PEER_TOPK])
        z = jnp.ones_like(v[0])
        for kk in range(1, PEER_TOPK):
            z = z + jnp.exp(v[kk] - v[0])
        rz = 1.0 / z
        e0_ref[hd] = jnp.exp(s0 - a[0]) * rz
        e1_ref[hd] = jnp.exp(s1 - b[0])
        th_ref[hd:hd + 1, :] = jnp.exp(tau - v[0]) * rz


def _peer_dense_kernel(n_ctx, h_ref, e0_ref, e1_ref, th_ref, u_ref, vt_ref, x_ref, m_ref, o_ref,
                       at_ref, w_ref, acc_ref):
    e = pl.program_id(2)
    eb, tt = at_ref.shape
    nk = e1_ref.shape[1]

    @pl.when(e == 0)
    def _():
        acc_ref[...] = jnp.zeros_like(acc_ref)

    at_ref[...] = _dot_nt(u_ref[...], h_ref[...])

    for i in range(eb // nk):
        rs = slice(i * nk, (i + 1) * nk)
        for lt in range(tt // 128):
            ls = slice(lt * 128, (lt + 1) * 128)
            gsum = jnp.zeros((nk, 128), F32)
            for hd in range(PEER_HEADS):
                p = e0_ref[hd, i:i + 1, ls] * e1_ref[hd, :, ls]
                gsum = gsum + jnp.where(p > th_ref[hd:hd + 1, ls], p, 0.0)
            w_ref[rs, ls] = (_gelu_tanh(at_ref[rs, ls]) * gsum).astype(BF16)
    acc_ref[...] += _dot(vt_ref[...], w_ref[...])

    @pl.when(e == pl.num_programs(2) - 1)
    def _():
        is_ctx = _is_ctx(pl.program_id(1), tt, n_ctx)
        o_ref[...] = x_ref[...] + _mod_row(m_ref, 5, is_ctx) * acc_ref[...].T


def peer_layer(xs, mods, g, n_ctx, w_q, keys, u_tab, v_tab):
    b, n, d = xs.shape
    tt = PEER_TT
    ntt = n // tt
    ne = u_tab.shape[0]
    nk = PEER_NKEYS
    hq = w_q.shape[-1]
    wqt = jnp.transpose(w_q).astype(BF16)
    keys_b = keys.reshape(PEER_HEADS * 2, nk, keys.shape[-1]).astype(BF16)
    u_b = u_tab.astype(BF16)
    vt_b = jnp.transpose(v_tab).astype(BF16)

    h2, e0, e1, th = pl.pallas_call(
        functools.partial(_peer_score_kernel, n_ctx),
        grid=(b, ntt),
        in_specs=[
            pl.BlockSpec((None, tt, d), lambda i, t: (i, t, 0)),
            pl.BlockSpec((1, d), lambda i, t: (0, 0)),
            pl.BlockSpec((2, 8, d), lambda i, t: (i, 0, 0)),
            pl.BlockSpec((hq, d), lambda i, t: (0, 0)),
            pl.BlockSpec(keys_b.shape, lambda i, t: (0, 0, 0)),
        ],
        out_specs=[
            pl.BlockSpec((None, tt, d), lambda i, t: (i, t, 0)),
            pl.BlockSpec((PEER_HEADS, nk, tt), lambda i, t: (0, 0, i * ntt + t)),
            pl.BlockSpec((PEER_HEADS, nk, tt), lambda i, t: (0, 0, i * ntt + t)),
            pl.BlockSpec((PEER_HEADS, tt), lambda i, t: (0, i * ntt + t)),
        ],
        out_shape=[
            jax.ShapeDtypeStruct((b, n, d), BF16),
            jax.ShapeDtypeStruct((PEER_HEADS, nk, b * n), F32),
            jax.ShapeDtypeStruct((PEER_HEADS, nk, b * n), F32),
            jax.ShapeDtypeStruct((PEER_HEADS, b * n), F32),
        ],
        compiler_params=_params("arbitrary", "arbitrary"),
        name="peer_score",
    )(xs, g.reshape(1, d), mods, wqt, keys_b)

    eb = PEER_EB
    return pl.pallas_call(
        functools.partial(_peer_dense_kernel, n_ctx),
        grid=(b, ntt, ne // eb),
        in_specs=[
            pl.BlockSpec((None, tt, d), lambda i, t, e: (i, t, 0)),
            pl.BlockSpec((PEER_HEADS, eb // nk, tt), lambda i, t, e: (0, e, i * ntt + t)),
            pl.BlockSpec((PEER_HEADS, nk, tt), lambda i, t, e: (0, 0, i * ntt + t)),
            pl.BlockSpec((PEER_HEADS, tt), lambda i, t, e: (0, i * ntt + t)),
            pl.BlockSpec((eb, d), lambda i, t, e: (e, 0)),
            pl.BlockSpec((d, eb), lambda i, t, e: (0, e)),
            pl.BlockSpec((None, tt, d), lambda i, t, e: (i, t, 0)),
            pl.BlockSpec((2, 8, d), lambda i, t, e: (i, 0, 0)),
        ],
        out_specs=pl.BlockSpec((None, tt, d), lambda i, t, e: (i, t, 0)),
        out_shape=jax.ShapeDtypeStruct(xs.shape, F32),
        scratch_shapes=[
            pltpu.VMEM((eb, tt), F32),
            pltpu.VMEM((eb, tt), BF16),
            pltpu.VMEM((d, tt), F32),
        ],
        input_output_aliases={6: 0},
        compiler_params=_params("arbitrary", "arbitrary", "arbitrary"),
        name="peer_dense",
    )(h2, e0, e1, th, u_b, vt_b, xs, mods)


def _final_kernel(x_ref, g_ref, o_ref):
    o_ref[...] = _rms(x_ref[...], g_ref[...])


def final_norm(xs, g, n_ctx):
    b, n, d = xs.shape
    off = n_ctx // TILE
    return pl.pallas_call(
        _final_kernel,
        grid=(b, (n - n_ctx) // TILE),
        in_specs=[
            pl.BlockSpec((None, TILE, d), lambda i, t: (i, t + off, 0)),
            pl.BlockSpec((1, d), lambda i, t: (0, 0)),
        ],
        out_specs=pl.BlockSpec((None, TILE, d), lambda i, t: (i, t, 0)),
        out_shape=jax.ShapeDtypeStruct((b, n - n_ctx, d), F32),
        compiler_params=_params("arbitrary", "arbitrary"),
        name="final_norm",
    )(xs, g.reshape(1, d))


def kernel(x, c, ctx, c_ctx, norm_g, ada_w, ada_b, mla_w_in, mla_q_norm, mla_w_uq, mla_kv_norm, mla_w_ukv, mla_w_o, s5_lam_re, s5_lam_im, s5_b_re, s5_b_im, s5_c_re, s5_c_im, s5_log_step, s5_d, s5_w_glu, gla_w_in, gla_gk_w1, gla_gk_w2, gla_gk_b, gla_o_norm, gla_w_o, peer_w_q, peer_keys, peer_u, peer_v, final_g):
    b, n_lat, d = x.shape
    n_ctx = ctx.shape[1]
    depth = ada_w.shape[0]
    xs = jnp.concatenate([ctx, x], axis=1)

    r = 8 * ((b + 1 + 7) // 8)
    cond = jnp.zeros((r, d), F32).at[:b].set(c).at[b].set(c_ctx)
    ada = ada_all(cond, ada_w, ada_b).reshape(depth, r, 6, d)
    lat = ada[:, :b]
    ctxp = jnp.broadcast_to(ada[:, b:b + 1], lat.shape)
    mods_all = jnp.stack([ctxp, lat], axis=2)
    mods_all = jnp.pad(mods_all, ((0, 0), (0, 0), (0, 0), (0, 2), (0, 0))).reshape(depth, b * 2, 8, d)

    for i in range(depth):
        kind, j = i % N_MIXERS, i // N_MIXERS
        mods = mods_all[i]
        if kind == 0:
            xs = mixer_mla(xs, mods, norm_g[i, 0], n_ctx, mla_w_in[j], mla_q_norm[j], mla_w_uq[j],
                           mla_kv_norm[j], mla_w_ukv[j], mla_w_o[j])
        elif kind == 1:
            xs = mixer_s5(xs, mods, norm_g[i, 0], n_ctx, s5_lam_re[j], s5_lam_im[j], s5_b_re[j], s5_b_im[j],
                          s5_c_re[j], s5_c_im[j], s5_log_step[j], s5_d[j], s5_w_glu[j])
        else:
            xs = mixer_gla(xs, mods, norm_g[i, 0], n_ctx, gla_w_in[j], gla_gk_w1[j], gla_gk_w2[j],
                           gla_gk_b[j], gla_o_norm[j], gla_w_o[j])
        xs = peer_layer(xs, mods, norm_g[i, 1], n_ctx, peer_w_q[i], peer_keys[i], peer_u[i], peer_v[i])
    return final_norm(xs, final_g, n_ctx)
```

```python
import functools
import math

import numpy as np
import jax
import jax.numpy as jnp
from jax import lax
from jax.experimental import pallas as pl
from jax.experimental.pallas import tpu as pltpu

F32 = jnp.float32
BF16 = jnp.bfloat16

EPS = 1e-6
GRID_W = 64
N_MIXERS = 3

MLA_HEADS = 16
MLA_NOPE = 64
MLA_ROPE = 32
MLA_V = 64
MLA_Q_LORA = 384
MLA_KV_LORA = 256
ROPE_BASE = 10000.0
MLA_HEAD_PAD = 128
MLA_HEADS_PER_STEP = 4

S5_GROUP = 16
S5_STATE = 64
S5_GROUPS_PER_BLOCK = 8
S5_CHUNK = 128

GLA_HEADS = 4
GLA_GATE_RANK = 16
GLA_GATE_NORM = 16.0
GLA_CHUNK = 64

PEER_HEADS = 8
PEER_NKEYS = 128
PEER_TOPK = 16

LANES = 128
TILE = 256
VMEM_LIMIT = 56 * 1024 * 1024


def _params(*sem):
    return pltpu.CompilerParams(dimension_semantics=sem, vmem_limit_bytes=VMEM_LIMIT)


def _gelu_tanh(x):
    return 0.5 * x * (1.0 + jnp.tanh(0.7978845608028654 * (x + 0.044715 * x * x * x)))


def _is_ctx(tile_idx, rows, n_ctx):
    pos = tile_idx * rows + lax.broadcasted_iota(jnp.int32, (rows, 1), 0)
    return pos < n_ctx


def _mod_row(m_ref, row, is_ctx):
    return jnp.where(is_ctx, m_ref[0, row:row + 1, :], m_ref[1, row:row + 1, :])


def _rms(x, g):
    return x * lax.rsqrt(jnp.mean(x * x, axis=-1, keepdims=True) + EPS) * g


def _norm_mod(x, g, m_ref, row0, is_ctx):
    return _rms(x, g) * (1.0 + _mod_row(m_ref, row0 + 1, is_ctx)) + _mod_row(m_ref, row0, is_ctx)


def _dot(a, b):
    return jnp.dot(a, b, preferred_element_type=F32)


def _dot_nt(a, b):
    return lax.dot_general(a, b, (((1,), (1,)), ((), ())), preferred_element_type=F32)


def _ada_kernel(c_ref, w_ref, b_ref, o_ref):
    c = c_ref[...]
    s = (c * jax.nn.sigmoid(c)).astype(BF16)
    o_ref[...] = _dot(s, w_ref[...].astype(BF16)) + b_ref[...]


def ada_all(cond, ada_w, ada_b):
    depth, d, n6 = ada_w.shape
    r = cond.shape[0]
    tn = 1024
    return pl.pallas_call(
        _ada_kernel,
        grid=(depth, n6 // tn),
        in_specs=[
            pl.BlockSpec((r, d), lambda l, j: (0, 0)),
            pl.BlockSpec((None, d, tn), lambda l, j: (l, 0, j)),
            pl.BlockSpec((None, 1, tn), lambda l, j: (l, 0, j)),
        ],
        out_specs=pl.BlockSpec((None, r, tn), lambda l, j: (l, 0, j)),
        out_shape=jax.ShapeDtypeStruct((depth, r, n6), F32),
        compiler_params=_params("arbitrary", "arbitrary"),
        name="ada",
    )(cond, ada_w, ada_b.reshape(depth, 1, n6))


def _proj_residual_kernel(n_ctx, gate_row, x_ref, y_ref, w_ref, m_ref, o_ref):
    is_ctx = _is_ctx(pl.program_id(1), x_ref.shape[0], n_ctx)
    f = _dot(y_ref[...], w_ref[...])
    o_ref[...] = x_ref[...] + _mod_row(m_ref, gate_row, is_ctx) * f


def proj_residual(xs, y, w, mods, n_ctx, gate_row):
    b, n, d = xs.shape
    k = y.shape[-1]
    return pl.pallas_call(
        functools.partial(_proj_residual_kernel, n_ctx, gate_row),
        grid=(b, n // TILE),
        in_specs=[
            pl.BlockSpec((None, TILE, d), lambda i, t: (i, t, 0)),
            pl.BlockSpec((None, TILE, k), lambda i, t: (i, t, 0)),
            pl.BlockSpec((k, d), lambda i, t: (0, 0)),
            pl.BlockSpec((2, 8, d), lambda i, t: (i, 0, 0)),
        ],
        out_specs=pl.BlockSpec((None, TILE, d), lambda i, t: (i, t, 0)),
        out_shape=jax.ShapeDtypeStruct(xs.shape, F32),
        input_output_aliases={0: 0},
        compiler_params=_params("arbitrary", "arbitrary"),
        name="proj_residual",
    )(xs, y, w, mods)


def _rope_tables(n_ctx, n_lat, scale):
    half = MLA_ROPE // 2
    rows_n = n_lat // GRID_W
    rows = np.repeat(np.arange(rows_n, dtype=np.float32), GRID_W)
    cols = np.tile(np.arange(GRID_W, dtype=np.float32), rows_n)
    inv = (ROPE_BASE ** (-np.arange(0, half, 2, dtype=np.float32) / half)).astype(np.float32)
    ang_r = rows[:, None] * inv
    ang_c = cols[:, None] * inv
    cos = np.concatenate([np.cos(ang_r), np.cos(ang_r), np.cos(ang_c), np.cos(ang_c)], axis=1)
    sin = np.concatenate([-np.sin(ang_r), np.sin(ang_r), -np.sin(ang_c), np.sin(ang_c)], axis=1)
    n = n_ctx + n_lat
    a = np.zeros((n, MLA_HEAD_PAD), np.float32)
    b = np.zeros((n, MLA_HEAD_PAD), np.float32)
    a[:, :MLA_NOPE] = 1.0
    a[:n_ctx, MLA_NOPE:MLA_NOPE + MLA_ROPE] = 1.0
    a[n_ctx:, MLA_NOPE:MLA_NOPE + MLA_ROPE] = cos
    b[n_ctx:, MLA_NOPE:MLA_NOPE + MLA_ROPE] = sin
    return jnp.asarray(a * scale), jnp.asarray(b * scale)


def _rope_swap_index():
    q = MLA_ROPE // 4
    base = np.arange(MLA_ROPE)
    return np.where((base % (2 * q)) < q, base + q, base - q)


def _mla_in_kernel(n_ctx, x_ref, g_ref, m_ref, wq_ref, wkv_ref, wkp_ref, qn_ref, kvn_ref,
                   wuq1_ref, wuq2_ref, wuk_ref, wuv_ref, aq_ref, bq_ref, ak_ref, bk_ref,
                   q_ref, k_ref, v_ref):
    is_ctx = _is_ctx(pl.program_id(1), x_ref.shape[0], n_ctx)
    hb = _norm_mod(x_ref[...], g_ref[...], m_ref, 0, is_ctx).astype(BF16)
    cq = _rms(_dot(hb, wq_ref[...]), qn_ref[...]).astype(BF16)
    ckv = _rms(_dot(hb, wkv_ref[...]), kvn_ref[...]).astype(BF16)
    kp = _dot(hb, wkp_ref[...])
    kpe = kp[:, :MLA_HEAD_PAD] * ak_ref[...] + kp[:, MLA_HEAD_PAD:] * bk_ref[...]
    y1 = _dot(cq, wuq1_ref[...])
    y2 = _dot(cq, wuq2_ref[...])
    kk = _dot(ckv, wuk_ref[...])
    aq = aq_ref[...]
    bq = bq_ref[...]
    for h in range(MLA_HEADS):
        sl = slice(h * MLA_HEAD_PAD, (h + 1) * MLA_HEAD_PAD)
        q_ref[:, sl] = (y1[:, sl] * aq + y2[:, sl] * bq).astype(BF16)
        k_ref[:, sl] = (kk[:, sl] + kpe).astype(BF16)
    v_ref[...] = _dot(ckv, wuv_ref[...]).astype(BF16)


def _mla_attn_kernel(n_ctx, q_ref, k_ref, v_ref, o_ref):
    n = k_ref.shape[0]

    def attend(nk):
        heads = range(MLA_HEADS_PER_STEP)
        ss = [_dot_nt(q_ref[:, h * MLA_HEAD_PAD:(h + 1) * MLA_HEAD_PAD],
                      k_ref[0:nk, h * MLA_HEAD_PAD:(h + 1) * MLA_HEAD_PAD]) for h in heads]
        ps = [jnp.exp(s - jnp.max(s, axis=-1, keepdims=True)) for s in ss]
        ls = [jnp.sum(p, axis=-1, keepdims=True) for p in ps]
        outs = [_dot(ps[h].astype(BF16), v_ref[0:nk, (h // 2) * 2 * MLA_V:(h // 2 + 1) * 2 * MLA_V]) / ls[h]
                for h in heads]
        lane = lax.broadcasted_iota(jnp.int32, outs[0].shape, 1)
        for pr in range(MLA_HEADS_PER_STEP // 2):
            o_ref[:, pr * 2 * MLA_V:(pr + 1) * 2 * MLA_V] = jnp.where(
                lane < MLA_V, outs[2 * pr], outs[2 * pr + 1]).astype(BF16)

    @pl.when(pl.program_id(2) * q_ref.shape[0] < n_ctx)
    def _():
        attend(n_ctx)

    @pl.when(pl.program_id(2) * q_ref.shape[0] >= n_ctx)
    def _():
        attend(n)


def mixer_mla(xs, mods, g, n_ctx, w_in, q_norm, w_uq, kv_norm, w_ukv, w_o):
    b, n, d = xs.shape
    hp = MLA_HEAD_PAD
    dq = MLA_NOPE + MLA_ROPE
    wq = w_in[:, :MLA_Q_LORA].astype(BF16)
    wkv = w_in[:, MLA_Q_LORA:MLA_Q_LORA + MLA_KV_LORA].astype(BF16)
    w_pe = w_in[:, MLA_Q_LORA + MLA_KV_LORA:]
    swap = _rope_swap_index()
    wkp = jnp.zeros((d, 2 * hp), F32)
    wkp = wkp.at[:, MLA_NOPE:dq].set(w_pe).at[:, hp + MLA_NOPE:hp + dq].set(w_pe[:, swap]).astype(BF16)
    uq = w_uq.reshape(MLA_Q_LORA, MLA_HEADS, dq)
    z = jnp.zeros((MLA_Q_LORA, MLA_HEADS, hp - dq), F32)
    wuq1 = jnp.concatenate([uq, z], axis=-1).reshape(MLA_Q_LORA, MLA_HEADS * hp).astype(BF16)
    zn = jnp.zeros((MLA_Q_LORA, MLA_HEADS, MLA_NOPE), F32)
    wuq2 = jnp.concatenate([zn, uq[:, :, MLA_NOPE:][:, :, swap], z], axis=-1)
    wuq2 = wuq2.reshape(MLA_Q_LORA, MLA_HEADS * hp).astype(BF16)
    ukv = w_ukv.reshape(MLA_KV_LORA, MLA_HEADS, MLA_NOPE + MLA_V)
    zk = jnp.zeros((MLA_KV_LORA, MLA_HEADS, hp - MLA_NOPE), F32)
    wuk = jnp.concatenate([ukv[:, :, :MLA_NOPE], zk], axis=-1).reshape(MLA_KV_LORA, MLA_HEADS * hp).astype(BF16)
    wuv = ukv[:, :, MLA_NOPE:].reshape(MLA_KV_LORA, MLA_HEADS * MLA_V).astype(BF16)
    aq, bq = _rope_tables(n_ctx, n - n_ctx, float(dq) ** -0.5)
    ak, bk = _rope_tables(n_ctx, n - n_ctx, 1.0)

    full = lambda shape: pl.BlockSpec(shape, lambda i, t: tuple(0 for _ in shape))
    tab = pl.BlockSpec((TILE, hp), lambda i, t: (t, 0))
    q, k, v = pl.pallas_call(
        functools.partial(_mla_in_kernel, n_ctx),
        grid=(b, n // TILE),
        in_specs=[
            pl.BlockSpec((None, TILE, d), lambda i, t: (i, t, 0)),
            full((1, d)),
            pl.BlockSpec((2, 8, d), lambda i, t: (i, 0, 0)),
            full(wq.shape), full(wkv.shape), full(wkp.shape),
            full((1, MLA_Q_LORA)), full((1, MLA_KV_LORA)),
            full(wuq1.shape), full(wuq2.shape), full(wuk.shape), full(wuv.shape),
            tab, tab, tab, tab,
        ],
        out_specs=[
            pl.BlockSpec((None, TILE, MLA_HEADS * hp), lambda i, t: (i, t, 0)),
            pl.BlockSpec((None, TILE, MLA_HEADS * hp), lambda i, t: (i, t, 0)),
            pl.BlockSpec((None, TILE, MLA_HEADS * MLA_V), lambda i, t: (i, t, 0)),
        ],
        out_shape=[
            jax.ShapeDtypeStruct((b, n, MLA_HEADS * hp), BF16),
            jax.ShapeDtypeStruct((b, n, MLA_HEADS * hp), BF16),
            jax.ShapeDtypeStruct((b, n, MLA_HEADS * MLA_V), BF16),
        ],
        compiler_params=_params("arbitrary", "arbitrary"),
        name="mla_in",
    )(xs, g.reshape(1, d), mods, wq, wkv, wkp, q_norm.reshape(1, -1), kv_norm.reshape(1, -1),
      wuq1, wuq2, wuk, wuv, aq, bq, ak, bk)

    hs = MLA_HEADS_PER_STEP
    o = pl.pallas_call(
        functools.partial(_mla_attn_kernel, n_ctx),
        grid=(b, MLA_HEADS // hs, n // TILE),
        in_specs=[
            pl.BlockSpec((None, TILE, hs * hp), lambda i, h, t: (i, t, h)),
            pl.BlockSpec((None, n, hs * hp), lambda i, h, t: (i, 0, h)),
            pl.BlockSpec((None, n, hs * MLA_V), lambda i, h, t: (i, 0, h)),
        ],
        out_specs=pl.BlockSpec((None, TILE, hs * MLA_V), lambda i, h, t: (i, t, h)),
        out_shape=jax.ShapeDtypeStruct((b, n, MLA_HEADS * MLA_V), BF16),
        compiler_params=_params("arbitrary", "arbitrary", "arbitrary"),
        name="mla_attn",
    )(q, k, v)
    return proj_residual(xs, o, w_o.astype(BF16), mods, n_ctx, 2)


def _s5_disc_kernel(lr_ref, li_ref, ls_ref, bre_ref, bim_ref, ar_ref, ai_ref, br_ref, bi_ref):
    lr = lr_ref[...]
    li = li_ref[...]
    dt = jnp.exp(ls_ref[...])
    mag = jnp.exp(lr * dt)
    ar = mag * jnp.cos(li * dt)
    ai = mag * jnp.sin(li * dt)
    den = lr * lr + li * li
    fr = ((ar - 1.0) * lr + ai * li) / den
    fi = (ai * lr - (ar - 1.0) * li) / den
    ar_ref[...] = ar
    ai_ref[...] = ai
    for c in range(S5_GROUP):
        br_ref[c] = fr * bre_ref[c] - fi * bim_ref[c]
        bi_ref[c] = fr * bim_ref[c] + fi * bre_ref[c]


def _s5_in_kernel(n_ctx, x_ref, g_ref, m_ref, u_ref):
    is_ctx = _is_ctx(pl.program_id(1), x_ref.shape[0], n_ctx)
    u_ref[...] = _norm_mod(x_ref[...], g_ref[...], m_ref, 0, is_ctx).astype(BF16)


def _s5_scan_kernel(nb, u_ref, bcat_ref, ar_ref, ai_ref, ccat_ref, y_ref, bu_ref, xs_ref, st_ref):
    d = pl.program_id(0)
    half = ar_ref.shape[-1]

    @pl.when(pl.program_id(2) == 0)
    def _():
        st_ref[...] = jnp.zeros_like(st_ref)

    bu_ref[...] = _dot(u_ref[...], bcat_ref[...])
    ar = jnp.broadcast_to(ar_ref[...], (nb, half))
    ai = jnp.broadcast_to(ai_ref[...], (nb, half))
    steps = u_ref.shape[0] // nb

    def step(i, carry):
        xr, xi = carry
        tt = jnp.where(d == 0, i, steps - 1 - i)
        r0 = pl.multiple_of(tt * nb, nb)
        nxr = ar * xr - ai * xi + bu_ref[pl.ds(r0, nb), 0:half]
        nxi = ar * xi + ai * xr + bu_ref[pl.ds(r0, nb), half:2 * half]
        xs_ref[pl.ds(r0, nb), 0:half] = nxr.astype(BF16)
        xs_ref[pl.ds(r0, nb), half:2 * half] = nxi.astype(BF16)
        return nxr, nxi

    xr, xi = lax.fori_loop(0, steps, step, (st_ref[:, 0:half], st_ref[:, half:2 * half]), unroll=2)
    st_ref[:, 0:half] = xr
    st_ref[:, half:2 * half] = xi
    y_ref[...] = _dot(xs_ref[...], ccat_ref[...])


def _s5_glu_kernel(n_ctx, x_ref, g_ref, m_ref, y_ref, dsk_ref, w_ref, o_ref):
    is_ctx = _is_ctx(pl.program_id(1), x_ref.shape[0], n_ctx)
    x = x_ref[...]
    u = _norm_mod(x, g_ref[...], m_ref, 0, is_ctx)
    y = y_ref[0] + y_ref[1]
    z = _dot(_gelu_tanh(y + dsk_ref[...] * u).astype(BF16), w_ref[...])
    dm = z.shape[-1] // 2
    out = z[:, :dm] * jax.nn.sigmoid(z[:, dm:])
    o_ref[...] = x + _mod_row(m_ref, 2, is_ctx) * out


def mixer_s5(xs, mods, g, n_ctx, lam_re, lam_im, b_re, b_im, c_re, c_im, log_step, d_skip, w_glu):
    b, n, d = xs.shape
    groups = d // S5_GROUP
    p = S5_STATE
    gb = S5_GROUPS_PER_BLOCK
    nblk = groups // gb
    cin = gb * S5_GROUP
    half = gb * p

    full = lambda shape: pl.BlockSpec(shape, lambda dd: tuple(0 for _ in shape))
    ar, ai, br, bi = pl.pallas_call(
        _s5_disc_kernel,
        grid=(2,),
        in_specs=[
            pl.BlockSpec((None, groups, p), lambda dd: (dd, 0, 0)),
            pl.BlockSpec((None, groups, p), lambda dd: (dd, 0, 0)),
            pl.BlockSpec((None, groups, 1), lambda dd: (dd, 0, 0)),
            pl.BlockSpec((None, S5_GROUP, groups, p), lambda dd: (dd, 0, 0, 0)),
            pl.BlockSpec((None, S5_GROUP, groups, p), lambda dd: (dd, 0, 0, 0)),
        ],
        out_specs=[
            pl.BlockSpec((None, groups, p), lambda dd: (dd, 0, 0)),
            pl.BlockSpec((None, groups, p), lambda dd: (dd, 0, 0)),
            pl.BlockSpec((None, S5_GROUP, groups, p), lambda dd: (dd, 0, 0, 0)),
            pl.BlockSpec((None, S5_GROUP, groups, p), lambda dd: (dd, 0, 0, 0)),
        ],
        out_shape=[
            jax.ShapeDtypeStruct((2, groups, p), F32),
            jax.ShapeDtypeStruct((2, groups, p), F32),
            jax.ShapeDtypeStruct((2, S5_GROUP, groups, p), F32),
            jax.ShapeDtypeStruct((2, S5_GROUP, groups, p), F32),
        ],
        compiler_params=_params("arbitrary"),
        name="s5_disc",
    )(lam_re, lam_im, log_step.reshape(2, groups, 1),
      jnp.transpose(b_re, (0, 3, 1, 2)), jnp.transpose(b_im, (0, 3, 1, 2)))

    eye = jnp.eye(gb, dtype=F32)

    def in_blocks(t):
        t = t.reshape(2, S5_GROUP, nblk, gb, p)
        return jnp.einsum('ab,dcjap->djacbp', eye, t).reshape(2, nblk, cin, half)

    bcat = jnp.concatenate([in_blocks(br), in_blocks(bi)], axis=-1).astype(BF16)

    def out_blocks(t):
        t = t.reshape(2, nblk, gb, S5_GROUP, p)
        return jnp.einsum('ab,djacp->djapbc', eye, t).reshape(2, nblk, half, cin)

    ccat = jnp.concatenate([out_blocks(c_re), -out_blocks(c_im)], axis=2).astype(BF16)
    ar_b = ar.reshape(2, nblk, 1, half)
    ai_b = ai.reshape(2, nblk, 1, half)

    u_tm = pl.pallas_call(
        functools.partial(_s5_in_kernel, n_ctx),
        grid=(b, n // TILE),
        in_specs=[
            pl.BlockSpec((None, TILE, d), lambda i, t: (i, t, 0)),
            pl.BlockSpec((1, d), lambda i, t: (0, 0)),
            pl.BlockSpec((2, 8, d), lambda i, t: (i, 0, 0)),
        ],
        out_specs=pl.BlockSpec((TILE, d), lambda i, t: (t, i)),
        out_shape=jax.ShapeDtypeStruct((n, b * d), BF16),
        compiler_params=_params("arbitrary", "arbitrary"),
        name="s5_in",
    )(xs, g.reshape(1, d), mods)

    tc = S5_CHUNK
    nchunks = n // tc
    ncc = n_ctx // tc
    rows = tc * b

    def chunk_of(dd, s):
        rev = jnp.where(s < ncc, ncc - 1 - s, nchunks - 1 - (s - ncc))
        return jnp.where(dd == 0, s, rev)

    y2 = pl.pallas_call(
        functools.partial(_s5_scan_kernel, b),
        grid=(2, nblk, nchunks),
        in_specs=[
            pl.BlockSpec((rows, cin), lambda dd, j, s: (chunk_of(dd, s), j)),
            pl.BlockSpec((None, None, cin, 2 * half), lambda dd, j, s: (dd, j, 0, 0)),
            pl.BlockSpec((None, None, 1, half), lambda dd, j, s: (dd, j, 0, 0)),
            pl.BlockSpec((None, None, 1, half), lambda dd, j, s: (dd, j, 0, 0)),
            pl.BlockSpec((None, None, 2 * half, cin), lambda dd, j, s: (dd, j, 0, 0)),
        ],
        out_specs=pl.BlockSpec((None, rows, cin), lambda dd, j, s: (dd, chunk_of(dd, s), j)),
        out_shape=jax.ShapeDtypeStruct((2, n * b, d), F32),
        scratch_shapes=[
            pltpu.VMEM((rows, 2 * half), F32),
            pltpu.VMEM((rows, 2 * half), BF16),
            pltpu.VMEM((b, 2 * half), F32),
        ],
        compiler_params=_params("arbitrary", "arbitrary", "arbitrary"),
        name="s5_scan",
    )(u_tm.reshape(n * b, d), bcat, ar_b, ai_b, ccat)

    return pl.pallas_call(
        functools.partial(_s5_glu_kernel, n_ctx),
        grid=(b, n // TILE),
        in_specs=[
            pl.BlockSpec((None, TILE, d), lambda i, t: (i, t, 0)),
            pl.BlockSpec((1, d), lambda i, t: (0, 0)),
            pl.BlockSpec((2, 8, d), lambda i, t: (i, 0, 0)),
            pl.BlockSpec((2, TILE, d), lambda i, t: (0, t, i)),
            pl.BlockSpec((1, d), lambda i, t: (0, 0)),
            pl.BlockSpec((d, 2 * d), lambda i, t: (0, 0)),
        ],
        out_specs=pl.BlockSpec((None, TILE, d), lambda i, t: (i, t, 0)),
        out_shape=jax.ShapeDtypeStruct(xs.shape, F32),
        input_output_aliases={0: 0},
        compiler_params=_params("arbitrary", "arbitrary"),
        name="s5_glu",
    )(xs, g.reshape(1, d), mods, y2.reshape(2, n, b * d), d_skip.reshape(1, d), w_glu.astype(BF16))


def _gla_in_kernel(n_ctx, x_ref, g_ref, m_ref, w_ref, w1_ref, o_ref, r_ref):
    is_ctx = _is_ctx(pl.program_id(1), x_ref.shape[0], n_ctx)
    hb = _norm_mod(x_ref[...], g_ref[...], m_ref, 0, is_ctx).astype(BF16)
    o_ref[...] = _dot(hb, w_ref[...]).astype(BF16)
    r_ref[...] = _dot(hb, w1_ref[...])


def _gla_scan_kernel(dk, dv, qf_ref, rf_ref, qr_ref, rr_ref, w2_ref, gb_ref, of_ref, or_ref, s_ref):
    c = GLA_CHUNK
    nck = qf_ref.shape[0] // c
    kd = GLA_HEADS * dk

    @pl.when(pl.program_id(1) == 0)
    def _():
        s_ref[...] = jnp.zeros_like(s_ref)

    row = lax.broadcasted_iota(jnp.int32, (c, c), 0)
    col = lax.broadcasted_iota(jnp.int32, (c, c), 1)
    masks = (row >= col, row <= col)

    refs = ((qf_ref, rf_ref, of_ref), (qr_ref, rr_ref, or_ref))
    chains = [(dd, h) for dd in range(2) for h in range(GLA_HEADS)]
    for i in range(nck):
        rows = [slice(i * c, (i + 1) * c), slice((nck - 1 - i) * c, (nck - i) * c)]
        decay = []
        for dd in range(2):
            rb = refs[dd][1][rows[dd], :].astype(BF16)
            lg = jax.nn.log_sigmoid(_dot(rb, w2_ref[dd]) + gb_ref[dd]) * (1.0 / GLA_GATE_NORM)
            bcum = jnp.dot(masks[dd].astype(F32), lg, preferred_element_type=F32,
                           precision=lax.Precision.HIGHEST)
            blast = jnp.sum(lg, axis=0, keepdims=True)
            decay.append((jnp.exp(bcum), jnp.exp(-bcum), jnp.exp(blast - bcum), jnp.exp(blast)))
        qg, att, kdec, vv, st = {}, {}, {}, {}, {}
        for ch in chains:
            dd, h = ch
            x_ref = refs[dd][0]
            hs = slice(h * dk, (h + 1) * dk)
            q = x_ref[rows[dd], hs].astype(F32) * (float(dk) ** -0.5)
            k = x_ref[rows[dd], kd + h * dk:kd + (h + 1) * dk].astype(F32)
            vv[ch] = x_ref[rows[dd], 2 * kd + h * dv:2 * kd + (h + 1) * dv]
            qg[ch] = (q * decay[dd][0][:, hs]).astype(BF16)
            kg = (k * decay[dd][1][:, hs]).astype(BF16)
            kdec[ch] = k * decay[dd][2][:, hs]
            att[ch] = jnp.where(masks[dd], _dot_nt(qg[ch], kg), 0.0).astype(BF16)
        for ch in chains:
            dd, h = ch
            st[ch] = s_ref[dd, h]
            refs[dd][2][rows[dd], h * dv:(h + 1) * dv] = (
                _dot(att[ch], vv[ch]) + _dot(qg[ch], st[ch].astype(BF16)))
        for ch in chains:
            dd, h = ch
            eb = jnp.broadcast_to(decay[dd][3][:, h * dk:(h + 1) * dk], (dk - c, dk))
            mt = jnp.concatenate([kdec[ch], eb], axis=0).T
            s_ref[dd, h] = mt[:, c:c + 1] * st[ch] + _dot(mt[:, 0:c].astype(BF16), vv[ch])


def _gla_out_kernel(n_ctx, x_ref, of_ref, or_ref, gt_ref, on_ref, w_ref, m_ref, o_ref):
    is_ctx = _is_ctx(pl.program_id(1), x_ref.shape[0], n_ctx)
    o = of_ref[...] + or_ref[...]
    gt = gt_ref[...].astype(F32)
    dv = on_ref.shape[-1]
    parts = []
    for h in range(GLA_HEADS):
        sl = slice(h * dv, (h + 1) * dv)
        parts.append(_rms(o[:, sl], on_ref[...]) * (gt[:, sl] * jax.nn.sigmoid(gt[:, sl])))
    y = jnp.concatenate(parts, axis=-1).astype(BF16)
    o_ref[...] = x_ref[...] + _mod_row(m_ref, 2, is_ctx) * _dot(y, w_ref[...])


def mixer_gla(xs, mods, g, n_ctx, w_in, gk_w1, gk_w2, gk_b, o_norm, w_o):
    b, n, d = xs.shape
    kd = d // 2
    vd = d
    dk = kd // GLA_HEADS
    dv = vd // GLA_HEADS
    rk = GLA_GATE_RANK
    w1 = jnp.zeros((d, 128), F32).at[:, :rk].set(gk_w1[0]).at[:, rk:2 * rk].set(gk_w1[1]).astype(BF16)
    w2 = jnp.zeros((2, 128, kd), F32)
    for dd in range(2):
        w2 = w2.at[dd, dd * rk:(dd + 1) * rk, :].set(gk_w2[dd])
    w2 = w2.astype(BF16)
    gbias = gk_b.reshape(2, 1, kd)
    nw = w_in.shape[-1]

    qkvg, r = pl.pallas_call(
        functools.partial(_gla_in_kernel, n_ctx),
        grid=(b, n // TILE),
        in_specs=[
            pl.BlockSpec((None, TILE, d), lambda i, t: (i, t, 0)),
            pl.BlockSpec((1, d), lambda i, t: (0, 0)),
            pl.BlockSpec((2, 8, d), lambda i, t: (i, 0, 0)),
            pl.BlockSpec((d, nw), lambda i, t: (0, 0)),
            pl.BlockSpec((d, 128), lambda i, t: (0, 0)),
        ],
        out_specs=[
            pl.BlockSpec((None, TILE, nw), lambda i, t: (i, t, 0)),
            pl.BlockSpec((None, TILE, 128), lambda i, t: (i, t, 0)),
        ],
        out_shape=[
            jax.ShapeDtypeStruct((b, n, nw), BF16),
            jax.ShapeDtypeStruct((b, n, 128), F32),
        ],
        compiler_params=_params("arbitrary", "arbitrary"),
        name="gla_in",
    )(xs, g.reshape(1, d), mods, w_in.astype(BF16), w1)

    nt = n // TILE
    nct = n_ctx // TILE

    def rev_tile(s):
        return jnp.where(s < nct, nct - 1 - s, nt - 1 - (s - nct))

    qkv_w = 2 * kd + vd
    o_fwd, o_rev = pl.pallas_call(
        functools.partial(_gla_scan_kernel, dk, dv),
        grid=(b, nt),
        in_specs=[
            pl.BlockSpec((None, TILE, qkv_w), lambda i, s: (i, s, 0)),
            pl.BlockSpec((None, TILE, 128), lambda i, s: (i, s, 0)),
            pl.BlockSpec((None, TILE, qkv_w), lambda i, s: (i, rev_tile(s), 0)),
            pl.BlockSpec((None, TILE, 128), lambda i, s: (i, rev_tile(s), 0)),
            pl.BlockSpec(w2.shape, lambda i, s: (0, 0, 0)),
            pl.BlockSpec(gbias.shape, lambda i, s: (0, 0, 0)),
        ],
        out_specs=[
            pl.BlockSpec((None, TILE, vd), lambda i, s: (i, s, 0)),
            pl.BlockSpec((None, TILE, vd), lambda i, s: (i, rev_tile(s), 0)),
        ],
        out_shape=[jax.ShapeDtypeStruct((b, n, vd), F32), jax.ShapeDtypeStruct((b, n, vd), F32)],
        scratch_shapes=[pltpu.VMEM((2, GLA_HEADS, dk, dv), F32)],
        compiler_params=_params("arbitrary", "arbitrary"),
        name="gla_scan",
    )(qkvg, r, qkvg, r, w2, gbias)

    return pl.pallas_call(
        functools.partial(_gla_out_kernel, n_ctx),
        grid=(b, nt),
        in_specs=[
            pl.BlockSpec((None, TILE, d), lambda i, t: (i, t, 0)),
            pl.BlockSpec((None, TILE, vd), lambda i, t: (i, t, 0)),
            pl.BlockSpec((None, TILE, vd), lambda i, t: (i, t, 0)),
            pl.BlockSpec((None, TILE, vd), lambda i, t: (i, t, (2 * kd + vd) // vd)),
            pl.BlockSpec((1, dv), lambda i, t: (0, 0)),
            pl.BlockSpec((vd, d), lambda i, t: (0, 0)),
            pl.BlockSpec((2, 8, d), lambda i, t: (i, 0, 0)),
        ],
        out_specs=pl.BlockSpec((None, TILE, d), lambda i, t: (i, t, 0)),
        out_shape=jax.ShapeDtypeStruct(xs.shape, F32),
        input_output_aliases={0: 0},
        compiler_params=_params("arbitrary", "arbitrary"),
        name="gla_out",
    )(xs, o_fwd, o_rev, qkvg, o_norm.reshape(1, dv), w_o.astype(BF16), mods)


PEER_TT = 768
PEER_EB = 2048
PEER_RANKS = PEER_TOPK + 1


def _top_rows(s, k):
    rows = []
    for _ in range(k):
        m = jnp.max(s, axis=0, keepdims=True)
        rows.append(m)
        s = jnp.where(s == m, -jnp.inf, s)
    return rows


def _stack_rows(rows, nrows):
    t = rows[0].shape[-1]
    ridx = lax.broadcasted_iota(jnp.int32, (nrows, t), 0)
    out = jnp.full((nrows, t), -jnp.inf, F32)
    for r, v in enumerate(rows):
        out = jnp.where(ridx == r, v, out)
    return out


def _pair_threshold(a, b):
    k = PEER_RANKS
    t = a[0].shape[-1]
    nr = 8 * ((k + 7) // 8)
    bcol = _stack_rows(b, nr)
    acol = _stack_rows(a, nr)
    slabs = [a[0] + bcol]
    row8 = lax.broadcasted_iota(jnp.int32, (8, t), 0)
    for i in range(2, 9):
        slabs.append(jnp.where(row8 < k // i, a[i - 1] + bcol[0:8], -jnp.inf))
    slabs.append(acol[8:nr] + b[0])
    return _top_rows(jnp.concatenate(slabs, axis=0), k)


def _peer_score_kernel(n_ctx, x_ref, g_ref, m_ref, wqt_ref, keys_ref, h_ref, e0_ref, e1_ref, th_ref):
    tt = x_ref.shape[0]
    is_ctx = _is_ctx(pl.program_id(1), tt, n_ctx)
    hb = _norm_mod(x_ref[...], g_ref[...], m_ref, 3, is_ctx).astype(BF16)
    h_ref[...] = hb
    qt = _dot_nt(wqt_ref[...], hb).astype(BF16)
    dkey = keys_ref.shape[-1]
    for hd in range(PEER_HEADS):
        s0 = _dot(keys_ref[2 * hd], qt[(2 * hd) * dkey:(2 * hd + 1) * dkey, :])
        s1 = _dot(keys_ref[2 * hd + 1], qt[(2 * hd + 1) * dkey:(2 * hd + 2) * dkey, :])
        a = _top_rows(s0, PEER_RANKS)
        b = _top_rows(s1, PEER_RANKS)
        v = _pair_threshold(a, b)
        tau = 0.5 * (v[PEER_TOPK - 1] + v[PEER_TOPK])
        z = jnp.ones_like(v[0])
        for kk in range(1, PEER_TOPK):
            z = z + jnp.exp(v[kk] - v[0])
        rz = 1.0 / z
        e0_ref[hd] = jnp.exp(s0 - a[0]) * rz
        e1_ref[hd] = jnp.exp(s1 - b[0])
        th_ref[hd:hd + 1, :] = jnp.exp(tau - v[0]) * rz


def _gelu_tanh_sigmoid_form(x):
    c0 = -2.0 * 0.7978845608028654 * 1.4426950408889634
    c1 = c0 * 0.044715
    return x / (1.0 + jnp.exp2(x * (c0 + c1 * (x * x))))


def _peer_dense_kernel(n_ctx, h_ref, e0_ref, e1_ref, th_ref, u_ref, vt_ref, x_ref, m_ref, o_ref,
                       w_ref, acc_ref):
    e = pl.program_id(2)
    eb, tt = w_ref.shape
    nk = e1_ref.shape[1]
    sub = 2 * nk

    @pl.when(e == 0)
    def _():
        acc_ref[...] = jnp.zeros_like(acc_ref)

    h = h_ref[...]
    for sb in range(eb // sub):
        a_sb = _dot_nt(u_ref[sb * sub:(sb + 1) * sub, :], h)
        for ii in range(sub // nk):
            i = sb * (sub // nk) + ii
            for lt in range(tt // LANES):
                ls = slice(lt * LANES, (lt + 1) * LANES)
                gsum = jnp.zeros((nk, LANES), F32)
                for hd in range(PEER_HEADS):
                    p = e0_ref[hd, i:i + 1, ls] * e1_ref[hd, :, ls]
                    gsum = gsum + jnp.where(p > th_ref[hd:hd + 1, ls], p, 0.0)
                act = _gelu_tanh_sigmoid_form(a_sb[ii * nk:(ii + 1) * nk, ls])
                w_ref[i * nk:(i + 1) * nk, ls] = (act * gsum).astype(BF16)
    acc_ref[...] += _dot(vt_ref[...], w_ref[...])

    @pl.when(e == pl.num_programs(2) - 1)
    def _():
        is_ctx = _is_ctx(pl.program_id(1), tt, n_ctx)
        o_ref[...] = x_ref[...] + _mod_row(m_ref, 5, is_ctx) * acc_ref[...].T


def peer_layer(xs, mods, g, n_ctx, w_q, keys, u_tab, v_tab):
    b, n, d = xs.shape
    tt = PEER_TT
    ntt = n // tt
    ne = u_tab.shape[0]
    nk = PEER_NKEYS
    hq = w_q.shape[-1]
    wqt = jnp.transpose(w_q).astype(BF16)
    keys_b = keys.reshape(PEER_HEADS * 2, nk, keys.shape[-1]).astype(BF16)
    u_b = u_tab.astype(BF16)
    vt_b = jnp.transpose(v_tab).astype(BF16)

    h2, e0, e1, th = pl.pallas_call(
        functools.partial(_peer_score_kernel, n_ctx),
        grid=(b, ntt),
        in_specs=[
            pl.BlockSpec((None, tt, d), lambda i, t: (i, t, 0)),
            pl.BlockSpec((1, d), lambda i, t: (0, 0)),
            pl.BlockSpec((2, 8, d), lambda i, t: (i, 0, 0)),
            pl.BlockSpec((hq, d), lambda i, t: (0, 0)),
            pl.BlockSpec(keys_b.shape, lambda i, t: (0, 0, 0)),
        ],
        out_specs=[
            pl.BlockSpec((None, tt, d), lambda i, t: (i, t, 0)),
            pl.BlockSpec((PEER_HEADS, nk, tt), lambda i, t: (0, 0, i * ntt + t)),
            pl.BlockSpec((PEER_HEADS, nk, tt), lambda i, t: (0, 0, i * ntt + t)),
            pl.BlockSpec((PEER_HEADS, tt), lambda i, t: (0, i * ntt + t)),
        ],
        out_shape=[
            jax.ShapeDtypeStruct((b, n, d), BF16),
            jax.ShapeDtypeStruct((PEER_HEADS, nk, b * n), F32),
            jax.ShapeDtypeStruct((PEER_HEADS, nk, b * n), F32),
            jax.ShapeDtypeStruct((PEER_HEADS, b * n), F32),
        ],
        compiler_params=_params("arbitrary", "arbitrary"),
        name="peer_score",
    )(xs, g.reshape(1, d), mods, wqt, keys_b)

    eb = PEER_EB
    return pl.pallas_call(
        functools.partial(_peer_dense_kernel, n_ctx),
        grid=(b, ntt, ne // eb),
        in_specs=[
            pl.BlockSpec((None, tt, d), lambda i, t, e: (i, t, 0)),
            pl.BlockSpec((PEER_HEADS, eb // nk, tt), lambda i, t, e: (0, e, i * ntt + t)),
            pl.BlockSpec((PEER_HEADS, nk, tt), lambda i, t, e: (0, 0, i * ntt + t)),
            pl.BlockSpec((PEER_HEADS, tt), lambda i, t, e: (0, i * ntt + t)),
            pl.BlockSpec((eb, d), lambda i, t, e: (e, 0)),
            pl.BlockSpec((d, eb), lambda i, t, e: (0, e)),
            pl.BlockSpec((None, tt, d), lambda i, t, e: (i, t, 0)),
            pl.BlockSpec((2, 8, d), lambda i, t, e: (i, 0, 0)),
        ],
        out_specs=pl.BlockSpec((None, tt, d), lambda i, t, e: (i, t, 0)),
        out_shape=jax.ShapeDtypeStruct(xs.shape, F32),
        scratch_shapes=[
            pltpu.VMEM((eb, tt), BF16),
            pltpu.VMEM((d, tt), F32),
        ],
        input_output_aliases={6: 0},
        compiler_params=_params("arbitrary", "arbitrary", "arbitrary"),
        name="peer_dense",
    )(h2, e0, e1, th, u_b, vt_b, xs, mods)


def _final_kernel(x_ref, g_ref, o_ref):
    o_ref[...] = _rms(x_ref[...], g_ref[...])


def final_norm(xs, g, n_ctx):
    b, n, d = xs.shape
    off = n_ctx // TILE
    return pl.pallas_call(
        _final_kernel,
        grid=(b, (n - n_ctx) // TILE),
        in_specs=[
            pl.BlockSpec((None, TILE, d), lambda i, t: (i, t + off, 0)),
            pl.BlockSpec((1, d), lambda i, t: (0, 0)),
        ],
        out_specs=pl.BlockSpec((None, TILE, d), lambda i, t: (i, t, 0)),
        out_shape=jax.ShapeDtypeStruct((b, n - n_ctx, d), F32),
        compiler_params=_params("arbitrary", "arbitrary"),
        name="final_norm",
    )(xs, g.reshape(1, d))


def kernel(x, c, ctx, c_ctx, norm_g, ada_w, ada_b, mla_w_in, mla_q_norm, mla_w_uq, mla_kv_norm, mla_w_ukv, mla_w_o, s5_lam_re, s5_lam_im, s5_b_re, s5_b_im, s5_c_re, s5_c_im, s5_log_step, s5_d, s5_w_glu, gla_w_in, gla_gk_w1, gla_gk_w2, gla_gk_b, gla_o_norm, gla_w_o, peer_w_q, peer_keys, peer_u, peer_v, final_g):
    b, n_lat, d = x.shape
    n_ctx = ctx.shape[1]
    depth = ada_w.shape[0]
    xs = jnp.concatenate([ctx, x], axis=1)

    r = 8 * ((b + 1 + 7) // 8)
    cond = jnp.zeros((r, d), F32).at[:b].set(c).at[b].set(c_ctx)
    ada = ada_all(cond, ada_w, ada_b).reshape(depth, r, 6, d)
    lat = ada[:, :b]
    ctxp = jnp.broadcast_to(ada[:, b:b + 1], lat.shape)
    mods_all = jnp.stack([ctxp, lat], axis=2)
    mods_all = jnp.pad(mods_all, ((0, 0), (0, 0), (0, 0), (0, 2), (0, 0))).reshape(depth, b * 2, 8, d)

    for i in range(depth):
        kind, j = i % N_MIXERS, i // N_MIXERS
        mods = mods_all[i]
        if kind == 0:
            xs = mixer_mla(xs, mods, norm_g[i, 0], n_ctx, mla_w_in[j], mla_q_norm[j], mla_w_uq[j],
                           mla_kv_norm[j], mla_w_ukv[j], mla_w_o[j])
        elif kind == 1:
            xs = mixer_s5(xs, mods, norm_g[i, 0], n_ctx, s5_lam_re[j], s5_lam_im[j], s5_b_re[j], s5_b_im[j],
                          s5_c_re[j], s5_c_im[j], s5_log_step[j], s5_d[j], s5_w_glu[j])
        else:
            xs = mixer_gla(xs, mods, norm_g[i, 0], n_ctx, gla_w_in[j], gla_gk_w1[j], gla_gk_w2[j],
                           gla_gk_b[j], gla_o_norm[j], gla_w_o[j])
        xs = peer_layer(xs, mods, norm_g[i, 1], n_ctx, peer_w_q[i], peer_keys[i], peer_u[i], peer_v[i])
    return final_norm(xs, final_g, n_ctx)
```

```python
import functools
import math

import numpy as np
import jax
import jax.numpy as jnp
from jax import lax
from jax.experimental import pallas as pl
from jax.experimental.pallas import tpu as pltpu

F32 = jnp.float32
BF16 = jnp.bfloat16

EPS = 1e-6
GRID_W = 64
N_MIXERS = 3

MLA_HEADS = 16
MLA_NOPE = 64
MLA_ROPE = 32
MLA_V = 64
MLA_Q_LORA = 384
MLA_KV_LORA = 256
ROPE_BASE = 10000.0
MLA_HEAD_PAD = 128
MLA_HEADS_PER_STEP = 4

S5_GROUP = 16
S5_STATE = 64
S5_GROUPS_PER_BLOCK = 8
S5_CHUNK = 128

GLA_HEADS = 4
GLA_GATE_RANK = 16
GLA_GATE_NORM = 16.0
GLA_CHUNK = 64

PEER_HEADS = 8
PEER_NKEYS = 128
PEER_TOPK = 16

LANES = 128
TILE = 256
VMEM_LIMIT = 56 * 1024 * 1024


def _params(*sem):
    return pltpu.CompilerParams(dimension_semantics=sem, vmem_limit_bytes=VMEM_LIMIT)


def _gelu_tanh(x):
    return 0.5 * x * (1.0 + jnp.tanh(0.7978845608028654 * (x + 0.044715 * x * x * x)))


def _is_ctx(tile_idx, rows, n_ctx):
    pos = tile_idx * rows + lax.broadcasted_iota(jnp.int32, (rows, 1), 0)
    return pos < n_ctx


def _mod_row(m_ref, row, is_ctx):
    return jnp.where(is_ctx, m_ref[0, row:row + 1, :], m_ref[1, row:row + 1, :])


def _rms(x, g):
    return x * lax.rsqrt(jnp.mean(x * x, axis=-1, keepdims=True) + EPS) * g


def _norm_mod(x, g, m_ref, row0, is_ctx):
    return _rms(x, g) * (1.0 + _mod_row(m_ref, row0 + 1, is_ctx)) + _mod_row(m_ref, row0, is_ctx)


def _dot(a, b):
    return jnp.dot(a, b, preferred_element_type=F32)


def _dot_nt(a, b):
    return lax.dot_general(a, b, (((1,), (1,)), ((), ())), preferred_element_type=F32)


def _ada_kernel(c_ref, w_ref, b_ref, o_ref):
    c = c_ref[...]
    s = (c * jax.nn.sigmoid(c)).astype(BF16)
    o_ref[...] = _dot(s, w_ref[...].astype(BF16)) + b_ref[...]


def ada_all(cond, ada_w, ada_b):
    depth, d, n6 = ada_w.shape
    r = cond.shape[0]
    tn = 1024
    return pl.pallas_call(
        _ada_kernel,
        grid=(depth, n6 // tn),
        in_specs=[
            pl.BlockSpec((r, d), lambda l, j: (0, 0)),
            pl.BlockSpec((None, d, tn), lambda l, j: (l, 0, j)),
            pl.BlockSpec((None, 1, tn), lambda l, j: (l, 0, j)),
        ],
        out_specs=pl.BlockSpec((None, r, tn), lambda l, j: (l, 0, j)),
        out_shape=jax.ShapeDtypeStruct((depth, r, n6), F32),
        compiler_params=_params("arbitrary", "arbitrary"),
        name="ada",
    )(cond, ada_w, ada_b.reshape(depth, 1, n6))


def _proj_residual_kernel(n_ctx, gate_row, x_ref, y_ref, w_ref, m_ref, o_ref):
    is_ctx = _is_ctx(pl.program_id(1), x_ref.shape[0], n_ctx)
    f = _dot(y_ref[...], w_ref[...])
    o_ref[...] = x_ref[...] + _mod_row(m_ref, gate_row, is_ctx) * f


def proj_residual(xs, y, w, mods, n_ctx, gate_row):
    b, n, d = xs.shape
    k = y.shape[-1]
    return pl.pallas_call(
        functools.partial(_proj_residual_kernel, n_ctx, gate_row),
        grid=(b, n // TILE),
        in_specs=[
            pl.BlockSpec((None, TILE, d), lambda i, t: (i, t, 0)),
            pl.BlockSpec((None, TILE, k), lambda i, t: (i, t, 0)),
            pl.BlockSpec((k, d), lambda i, t: (0, 0)),
            pl.BlockSpec((2, 8, d), lambda i, t: (i, 0, 0)),
        ],
        out_specs=pl.BlockSpec((None, TILE, d), lambda i, t: (i, t, 0)),
        out_shape=jax.ShapeDtypeStruct(xs.shape, F32),
        input_output_aliases={0: 0},
        compiler_params=_params("arbitrary", "arbitrary"),
        name="proj_residual",
    )(xs, y, w, mods)


def _rope_tables(n_ctx, n_lat, scale):
    half = MLA_ROPE // 2
    rows_n = n_lat // GRID_W
    rows = np.repeat(np.arange(rows_n, dtype=np.float32), GRID_W)
    cols = np.tile(np.arange(GRID_W, dtype=np.float32), rows_n)
    inv = (ROPE_BASE ** (-np.arange(0, half, 2, dtype=np.float32) / half)).astype(np.float32)
    ang_r = rows[:, None] * inv
    ang_c = cols[:, None] * inv
    cos = np.concatenate([np.cos(ang_r), np.cos(ang_r), np.cos(ang_c), np.cos(ang_c)], axis=1)
    sin = np.concatenate([-np.sin(ang_r), np.sin(ang_r), -np.sin(ang_c), np.sin(ang_c)], axis=1)
    n = n_ctx + n_lat
    a = np.zeros((n, MLA_HEAD_PAD), np.float32)
    b = np.zeros((n, MLA_HEAD_PAD), np.float32)
    a[:, :MLA_NOPE] = 1.0
    a[:n_ctx, MLA_NOPE:MLA_NOPE + MLA_ROPE] = 1.0
    a[n_ctx:, MLA_NOPE:MLA_NOPE + MLA_ROPE] = cos
    b[n_ctx:, MLA_NOPE:MLA_NOPE + MLA_ROPE] = sin
    return jnp.asarray(a * scale), jnp.asarray(b * scale)


def _rope_swap_index():
    q = MLA_ROPE // 4
    base = np.arange(MLA_ROPE)
    return np.where((base % (2 * q)) < q, base + q, base - q)


def _mla_in_kernel(n_ctx, x_ref, g_ref, m_ref, wq_ref, wkv_ref, wkp_ref, qn_ref, kvn_ref,
                   wuq1_ref, wuq2_ref, wuk_ref, wuv_ref, aq_ref, bq_ref, ak_ref, bk_ref,
                   q_ref, k_ref, v_ref):
    is_ctx = _is_ctx(pl.program_id(1), x_ref.shape[0], n_ctx)
    hb = _norm_mod(x_ref[...], g_ref[...], m_ref, 0, is_ctx).astype(BF16)
    cq = _rms(_dot(hb, wq_ref[...]), qn_ref[...]).astype(BF16)
    ckv = _rms(_dot(hb, wkv_ref[...]), kvn_ref[...]).astype(BF16)
    kp = _dot(hb, wkp_ref[...])
    kpe = kp[:, :MLA_HEAD_PAD] * ak_ref[...] + kp[:, MLA_HEAD_PAD:] * bk_ref[...]
    y1 = _dot(cq, wuq1_ref[...])
    y2 = _dot(cq, wuq2_ref[...])
    kk = _dot(ckv, wuk_ref[...])
    aq = aq_ref[...]
    bq = bq_ref[...]
    for h in range(MLA_HEADS):
        sl = slice(h * MLA_HEAD_PAD, (h + 1) * MLA_HEAD_PAD)
        q_ref[:, sl] = (y1[:, sl] * aq + y2[:, sl] * bq).astype(BF16)
        k_ref[:, sl] = (kk[:, sl] + kpe).astype(BF16)
    v_ref[...] = _dot(ckv, wuv_ref[...]).astype(BF16)


def _mla_attn_kernel(n_ctx, q_ref, k_ref, v_ref, o_ref):
    n = k_ref.shape[0]

    def attend(nk):
        heads = range(MLA_HEADS_PER_STEP)
        ss = [_dot_nt(q_ref[:, h * MLA_HEAD_PAD:(h + 1) * MLA_HEAD_PAD],
                      k_ref[0:nk, h * MLA_HEAD_PAD:(h + 1) * MLA_HEAD_PAD]) for h in heads]
        ps = [jnp.exp(s - jnp.max(s, axis=-1, keepdims=True)) for s in ss]
        ls = [jnp.sum(p, axis=-1, keepdims=True) for p in ps]
        outs = [_dot(ps[h].astype(BF16), v_ref[0:nk, (h // 2) * 2 * MLA_V:(h // 2 + 1) * 2 * MLA_V]) / ls[h]
                for h in heads]
        lane = lax.broadcasted_iota(jnp.int32, outs[0].shape, 1)
        for pr in range(MLA_HEADS_PER_STEP // 2):
            o_ref[:, pr * 2 * MLA_V:(pr + 1) * 2 * MLA_V] = jnp.where(
                lane < MLA_V, outs[2 * pr], outs[2 * pr + 1]).astype(BF16)

    @pl.when(pl.program_id(2) * q_ref.shape[0] < n_ctx)
    def _():
        attend(n_ctx)

    @pl.when(pl.program_id(2) * q_ref.shape[0] >= n_ctx)
    def _():
        attend(n)


def mixer_mla(xs, mods, g, n_ctx, w_in, q_norm, w_uq, kv_norm, w_ukv, w_o):
    b, n, d = xs.shape
    hp = MLA_HEAD_PAD
    dq = MLA_NOPE + MLA_ROPE
    wq = w_in[:, :MLA_Q_LORA].astype(BF16)
    wkv = w_in[:, MLA_Q_LORA:MLA_Q_LORA + MLA_KV_LORA].astype(BF16)
    w_pe = w_in[:, MLA_Q_LORA + MLA_KV_LORA:]
    swap = _rope_swap_index()
    wkp = jnp.zeros((d, 2 * hp), F32)
    wkp = wkp.at[:, MLA_NOPE:dq].set(w_pe).at[:, hp + MLA_NOPE:hp + dq].set(w_pe[:, swap]).astype(BF16)
    uq = w_uq.reshape(MLA_Q_LORA, MLA_HEADS, dq)
    z = jnp.zeros((MLA_Q_LORA, MLA_HEADS, hp - dq), F32)
    wuq1 = jnp.concatenate([uq, z], axis=-1).reshape(MLA_Q_LORA, MLA_HEADS * hp).astype(BF16)
    zn = jnp.zeros((MLA_Q_LORA, MLA_HEADS, MLA_NOPE), F32)
    wuq2 = jnp.concatenate([zn, uq[:, :, MLA_NOPE:][:, :, swap], z], axis=-1)
    wuq2 = wuq2.reshape(MLA_Q_LORA, MLA_HEADS * hp).astype(BF16)
    ukv = w_ukv.reshape(MLA_KV_LORA, MLA_HEADS, MLA_NOPE + MLA_V)
    zk = jnp.zeros((MLA_KV_LORA, MLA_HEADS, hp - MLA_NOPE), F32)
    wuk = jnp.concatenate([ukv[:, :, :MLA_NOPE], zk], axis=-1).reshape(MLA_KV_LORA, MLA_HEADS * hp).astype(BF16)
    wuv = ukv[:, :, MLA_NOPE:].reshape(MLA_KV_LORA, MLA_HEADS * MLA_V).astype(BF16)
    aq, bq = _rope_tables(n_ctx, n - n_ctx, float(dq) ** -0.5)
    ak, bk = _rope_tables(n_ctx, n - n_ctx, 1.0)

    full = lambda shape: pl.BlockSpec(shape, lambda i, t: tuple(0 for _ in shape))
    tab = pl.BlockSpec((TILE, hp), lambda i, t: (t, 0))
    q, k, v = pl.pallas_call(
        functools.partial(_mla_in_kernel, n_ctx),
        grid=(b, n // TILE),
        in_specs=[
            pl.BlockSpec((None, TILE, d), lambda i, t: (i, t, 0)),
            full((1, d)),
            pl.BlockSpec((2, 8, d), lambda i, t: (i, 0, 0)),
            full(wq.shape), full(wkv.shape), full(wkp.shape),
            full((1, MLA_Q_LORA)), full((1, MLA_KV_LORA)),
            full(wuq1.shape), full(wuq2.shape), full(wuk.shape), full(wuv.shape),
            tab, tab, tab, tab,
        ],
        out_specs=[
            pl.BlockSpec((None, TILE, MLA_HEADS * hp), lambda i, t: (i, t, 0)),
            pl.BlockSpec((None, TILE, MLA_HEADS * hp), lambda i, t: (i, t, 0)),
            pl.BlockSpec((None, TILE, MLA_HEADS * MLA_V), lambda i, t: (i, t, 0)),
        ],
        out_shape=[
            jax.ShapeDtypeStruct((b, n, MLA_HEADS * hp), BF16),
            jax.ShapeDtypeStruct((b, n, MLA_HEADS * hp), BF16),
            jax.ShapeDtypeStruct((b, n, MLA_HEADS * MLA_V), BF16),
        ],
        compiler_params=_params("arbitrary", "arbitrary"),
        name="mla_in",
    )(xs, g.reshape(1, d), mods, wq, wkv, wkp, q_norm.reshape(1, -1), kv_norm.reshape(1, -1),
      wuq1, wuq2, wuk, wuv, aq, bq, ak, bk)

    hs = MLA_HEADS_PER_STEP
    o = pl.pallas_call(
        functools.partial(_mla_attn_kernel, n_ctx),
        grid=(b, MLA_HEADS // hs, n // TILE),
        in_specs=[
            pl.BlockSpec((None, TILE, hs * hp), lambda i, h, t: (i, t, h)),
            pl.BlockSpec((None, n, hs * hp), lambda i, h, t: (i, 0, h)),
            pl.BlockSpec((None, n, hs * MLA_V), lambda i, h, t: (i, 0, h)),
        ],
        out_specs=pl.BlockSpec((None, TILE, hs * MLA_V), lambda i, h, t: (i, t, h)),
        out_shape=jax.ShapeDtypeStruct((b, n, MLA_HEADS * MLA_V), BF16),
        compiler_params=_params("arbitrary", "arbitrary", "arbitrary"),
        name="mla_attn",
    )(q, k, v)
    return proj_residual(xs, o, w_o.astype(BF16), mods, n_ctx, 2)


def _s5_disc_kernel(lr_ref, li_ref, ls_ref, bre_ref, bim_ref, ar_ref, ai_ref, br_ref, bi_ref):
    lr = lr_ref[...]
    li = li_ref[...]
    dt = jnp.exp(ls_ref[...])
    mag = jnp.exp(lr * dt)
    ar = mag * jnp.cos(li * dt)
    ai = mag * jnp.sin(li * dt)
    den = lr * lr + li * li
    fr = ((ar - 1.0) * lr + ai * li) / den
    fi = (ai * lr - (ar - 1.0) * li) / den
    ar_ref[...] = ar
    ai_ref[...] = ai
    for c in range(S5_GROUP):
        br_ref[c] = fr * bre_ref[c] - fi * bim_ref[c]
        bi_ref[c] = fr * bim_ref[c] + fi * bre_ref[c]


def _s5_in_kernel(n_ctx, x_ref, g_ref, m_ref, u_ref):
    is_ctx = _is_ctx(pl.program_id(1), x_ref.shape[0], n_ctx)
    u_ref[...] = _norm_mod(x_ref[...], g_ref[...], m_ref, 0, is_ctx).astype(BF16)


def _s5_scan_kernel(nb, u_ref, bcat_ref, ar_ref, ai_ref, ccat_ref, y_ref, bu_ref, xs_ref, st_ref):
    d = pl.program_id(0)
    half = ar_ref.shape[-1]

    @pl.when(pl.program_id(2) == 0)
    def _():
        st_ref[...] = jnp.zeros_like(st_ref)

    bu_ref[...] = _dot(u_ref[...], bcat_ref[...])
    ar = jnp.broadcast_to(ar_ref[...], (nb, half))
    ai = jnp.broadcast_to(ai_ref[...], (nb, half))
    steps = u_ref.shape[0] // nb

    def step(i, carry):
        xr, xi = carry
        tt = jnp.where(d == 0, i, steps - 1 - i)
        r0 = pl.multiple_of(tt * nb, nb)
        nxr = ar * xr - ai * xi + bu_ref[pl.ds(r0, nb), 0:half]
        nxi = ar * xi + ai * xr + bu_ref[pl.ds(r0, nb), half:2 * half]
        xs_ref[pl.ds(r0, nb), 0:half] = nxr.astype(BF16)
        xs_ref[pl.ds(r0, nb), half:2 * half] = nxi.astype(BF16)
        return nxr, nxi

    xr, xi = lax.fori_loop(0, steps, step, (st_ref[:, 0:half], st_ref[:, half:2 * half]), unroll=2)
    st_ref[:, 0:half] = xr
    st_ref[:, half:2 * half] = xi
    y_ref[...] = _dot(xs_ref[...], ccat_ref[...])


def _s5_glu_kernel(n_ctx, x_ref, g_ref, m_ref, y_ref, dsk_ref, w_ref, o_ref):
    is_ctx = _is_ctx(pl.program_id(1), x_ref.shape[0], n_ctx)
    x = x_ref[...]
    u = _norm_mod(x, g_ref[...], m_ref, 0, is_ctx)
    y = y_ref[0] + y_ref[1]
    z = _dot(_gelu_tanh(y + dsk_ref[...] * u).astype(BF16), w_ref[...])
    dm = z.shape[-1] // 2
    out = z[:, :dm] * jax.nn.sigmoid(z[:, dm:])
    o_ref[...] = x + _mod_row(m_ref, 2, is_ctx) * out


def mixer_s5(xs, mods, g, n_ctx, lam_re, lam_im, b_re, b_im, c_re, c_im, log_step, d_skip, w_glu):
    b, n, d = xs.shape
    groups = d // S5_GROUP
    p = S5_STATE
    gb = S5_GROUPS_PER_BLOCK
    nblk = groups // gb
    cin = gb * S5_GROUP
    half = gb * p

    full = lambda shape: pl.BlockSpec(shape, lambda dd: tuple(0 for _ in shape))
    ar, ai, br, bi = pl.pallas_call(
        _s5_disc_kernel,
        grid=(2,),
        in_specs=[
            pl.BlockSpec((None, groups, p), lambda dd: (dd, 0, 0)),
            pl.BlockSpec((None, groups, p), lambda dd: (dd, 0, 0)),
            pl.BlockSpec((None, groups, 1), lambda dd: (dd, 0, 0)),
            pl.BlockSpec((None, S5_GROUP, groups, p), lambda dd: (dd, 0, 0, 0)),
            pl.BlockSpec((None, S5_GROUP, groups, p), lambda dd: (dd, 0, 0, 0)),
        ],
        out_specs=[
            pl.BlockSpec((None, groups, p), lambda dd: (dd, 0, 0)),
            pl.BlockSpec((None, groups, p), lambda dd: (dd, 0, 0)),
            pl.BlockSpec((None, S5_GROUP, groups, p), lambda dd: (dd, 0, 0, 0)),
            pl.BlockSpec((None, S5_GROUP, groups, p), lambda dd: (dd, 0, 0, 0)),
        ],
        out_shape=[
            jax.ShapeDtypeStruct((2, groups, p), F32),
            jax.ShapeDtypeStruct((2, groups, p), F32),
            jax.ShapeDtypeStruct((2, S5_GROUP, groups, p), F32),
            jax.ShapeDtypeStruct((2, S5_GROUP, groups, p), F32),
        ],
        compiler_params=_params("arbitrary"),
        name="s5_disc",
    )(lam_re, lam_im, log_step.reshape(2, groups, 1),
      jnp.transpose(b_re, (0, 3, 1, 2)), jnp.transpose(b_im, (0, 3, 1, 2)))

    eye = jnp.eye(gb, dtype=F32)

    def in_blocks(t):
        t = t.reshape(2, S5_GROUP, nblk, gb, p)
        return jnp.einsum('ab,dcjap->djacbp', eye, t).reshape(2, nblk, cin, half)

    bcat = jnp.concatenate([in_blocks(br), in_blocks(bi)], axis=-1).astype(BF16)

    def out_blocks(t):
        t = t.reshape(2, nblk, gb, S5_GROUP, p)
        return jnp.einsum('ab,djacp->djapbc', eye, t).reshape(2, nblk, half, cin)

    ccat = jnp.concatenate([out_blocks(c_re), -out_blocks(c_im)], axis=2).astype(BF16)
    ar_b = ar.reshape(2, nblk, 1, half)
    ai_b = ai.reshape(2, nblk, 1, half)

    u_tm = pl.pallas_call(
        functools.partial(_s5_in_kernel, n_ctx),
        grid=(b, n // TILE),
        in_specs=[
            pl.BlockSpec((None, TILE, d), lambda i, t: (i, t, 0)),
            pl.BlockSpec((1, d), lambda i, t: (0, 0)),
            pl.BlockSpec((2, 8, d), lambda i, t: (i, 0, 0)),
        ],
        out_specs=pl.BlockSpec((TILE, d), lambda i, t: (t, i)),
        out_shape=jax.ShapeDtypeStruct((n, b * d), BF16),
        compiler_params=_params("arbitrary", "arbitrary"),
        name="s5_in",
    )(xs, g.reshape(1, d), mods)

    tc = S5_CHUNK
    nchunks = n // tc
    ncc = n_ctx // tc
    rows = tc * b

    def chunk_of(dd, s):
        rev = jnp.where(s < ncc, ncc - 1 - s, nchunks - 1 - (s - ncc))
        return jnp.where(dd == 0, s, rev)

    y2 = pl.pallas_call(
        functools.partial(_s5_scan_kernel, b),
        grid=(2, nblk, nchunks),
        in_specs=[
            pl.BlockSpec((rows, cin), lambda dd, j, s: (chunk_of(dd, s), j)),
            pl.BlockSpec((None, None, cin, 2 * half), lambda dd, j, s: (dd, j, 0, 0)),
            pl.BlockSpec((None, None, 1, half), lambda dd, j, s: (dd, j, 0, 0)),
            pl.BlockSpec((None, None, 1, half), lambda dd, j, s: (dd, j, 0, 0)),
            pl.BlockSpec((None, None, 2 * half, cin), lambda dd, j, s: (dd, j, 0, 0)),
        ],
        out_specs=pl.BlockSpec((None, rows, cin), lambda dd, j, s: (dd, chunk_of(dd, s), j)),
        out_shape=jax.ShapeDtypeStruct((2, n * b, d), F32),
        scratch_shapes=[
            pltpu.VMEM((rows, 2 * half), F32),
            pltpu.VMEM((rows, 2 * half), BF16),
            pltpu.VMEM((b, 2 * half), F32),
        ],
        compiler_params=_params("arbitrary", "arbitrary", "arbitrary"),
        name="s5_scan",
    )(u_tm.reshape(n * b, d), bcat, ar_b, ai_b, ccat)

    return pl.pallas_call(
        functools.partial(_s5_glu_kernel, n_ctx),
        grid=(b, n // TILE),
        in_specs=[
            pl.BlockSpec((None, TILE, d), lambda i, t: (i, t, 0)),
            pl.BlockSpec((1, d), lambda i, t: (0, 0)),
            pl.BlockSpec((2, 8, d), lambda i, t: (i, 0, 0)),
            pl.BlockSpec((2, TILE, d), lambda i, t: (0, t, i)),
            pl.BlockSpec((1, d), lambda i, t: (0, 0)),
            pl.BlockSpec((d, 2 * d), lambda i, t: (0, 0)),
        ],
        out_specs=pl.BlockSpec((None, TILE, d), lambda i, t: (i, t, 0)),
        out_shape=jax.ShapeDtypeStruct(xs.shape, F32),
        input_output_aliases={0: 0},
        compiler_params=_params("arbitrary", "arbitrary"),
        name="s5_glu",
    )(xs, g.reshape(1, d), mods, y2.reshape(2, n, b * d), d_skip.reshape(1, d), w_glu.astype(BF16))


def _gla_in_kernel(n_ctx, x_ref, g_ref, m_ref, w_ref, w1_ref, o_ref, r_ref):
    is_ctx = _is_ctx(pl.program_id(1), x_ref.shape[0], n_ctx)
    hb = _norm_mod(x_ref[...], g_ref[...], m_ref, 0, is_ctx).astype(BF16)
    o_ref[...] = _dot(hb, w_ref[...]).astype(BF16)
    r_ref[...] = _dot(hb, w1_ref[...])


def _gla_scan_kernel(dk, dv, qf_ref, rf_ref, qr_ref, rr_ref, w2_ref, gb_ref, of_ref, or_ref, s_ref):
    c = GLA_CHUNK
    nck = qf_ref.shape[0] // c
    kd = GLA_HEADS * dk

    @pl.when(pl.program_id(1) == 0)
    def _():
        s_ref[...] = jnp.zeros_like(s_ref)

    row = lax.broadcasted_iota(jnp.int32, (c, c), 0)
    col = lax.broadcasted_iota(jnp.int32, (c, c), 1)
    masks = (row >= col, row <= col)

    refs = ((qf_ref, rf_ref, of_ref), (qr_ref, rr_ref, or_ref))
    chains = [(dd, h) for dd in range(2) for h in range(GLA_HEADS)]
    for i in range(nck):
        rows = [slice(i * c, (i + 1) * c), slice((nck - 1 - i) * c, (nck - i) * c)]
        decay = []
        for dd in range(2):
            rb = refs[dd][1][rows[dd], :].astype(BF16)
            lg = jax.nn.log_sigmoid(_dot(rb, w2_ref[dd]) + gb_ref[dd]) * (1.0 / GLA_GATE_NORM)
            bcum = jnp.dot(masks[dd].astype(F32), lg, preferred_element_type=F32,
                           precision=lax.Precision.HIGHEST)
            blast = jnp.sum(lg, axis=0, keepdims=True)
            decay.append((jnp.exp(bcum), jnp.exp(-bcum), jnp.exp(blast - bcum), jnp.exp(blast)))
        qg, att, kdec, vv, st = {}, {}, {}, {}, {}
        for ch in chains:
            dd, h = ch
            x_ref = refs[dd][0]
            hs = slice(h * dk, (h + 1) * dk)
            q = x_ref[rows[dd], hs].astype(F32) * (float(dk) ** -0.5)
            k = x_ref[rows[dd], kd + h * dk:kd + (h + 1) * dk].astype(F32)
            vv[ch] = x_ref[rows[dd], 2 * kd + h * dv:2 * kd + (h + 1) * dv]
            qg[ch] = (q * decay[dd][0][:, hs]).astype(BF16)
            kg = (k * decay[dd][1][:, hs]).astype(BF16)
            kdec[ch] = k * decay[dd][2][:, hs]
            att[ch] = jnp.where(masks[dd], _dot_nt(qg[ch], kg), 0.0).astype(BF16)
        for ch in chains:
            dd, h = ch
            st[ch] = s_ref[dd, h]
            refs[dd][2][rows[dd], h * dv:(h + 1) * dv] = (
                _dot(att[ch], vv[ch]) + _dot(qg[ch], st[ch].astype(BF16)))
        for ch in chains:
            dd, h = ch
            eb = jnp.broadcast_to(decay[dd][3][:, h * dk:(h + 1) * dk], (dk - c, dk))
            mt = jnp.concatenate([kdec[ch], eb], axis=0).T
            s_ref[dd, h] = mt[:, c:c + 1] * st[ch] + _dot(mt[:, 0:c].astype(BF16), vv[ch])


def _gla_out_kernel(n_ctx, x_ref, of_ref, or_ref, gt_ref, on_ref, w_ref, m_ref, o_ref):
    is_ctx = _is_ctx(pl.program_id(1), x_ref.shape[0], n_ctx)
    o = of_ref[...] + or_ref[...]
    gt = gt_ref[...].astype(F32)
    dv = on_ref.shape[-1]
    parts = []
    for h in range(GLA_HEADS):
        sl = slice(h * dv, (h + 1) * dv)
        parts.append(_rms(o[:, sl], on_ref[...]) * (gt[:, sl] * jax.nn.sigmoid(gt[:, sl])))
    y = jnp.concatenate(parts, axis=-1).astype(BF16)
    o_ref[...] = x_ref[...] + _mod_row(m_ref, 2, is_ctx) * _dot(y, w_ref[...])


def mixer_gla(xs, mods, g, n_ctx, w_in, gk_w1, gk_w2, gk_b, o_norm, w_o):
    b, n, d = xs.shape
    kd = d // 2
    vd = d
    dk = kd // GLA_HEADS
    dv = vd // GLA_HEADS
    rk = GLA_GATE_RANK
    w1 = jnp.zeros((d, 128), F32).at[:, :rk].set(gk_w1[0]).at[:, rk:2 * rk].set(gk_w1[1]).astype(BF16)
    w2 = jnp.zeros((2, 128, kd), F32)
    for dd in range(2):
        w2 = w2.at[dd, dd * rk:(dd + 1) * rk, :].set(gk_w2[dd])
    w2 = w2.astype(BF16)
    gbias = gk_b.reshape(2, 1, kd)
    nw = w_in.shape[-1]

    qkvg, r = pl.pallas_call(
        functools.partial(_gla_in_kernel, n_ctx),
        grid=(b, n // TILE),
        in_specs=[
            pl.BlockSpec((None, TILE, d), lambda i, t: (i, t, 0)),
            pl.BlockSpec((1, d), lambda i, t: (0, 0)),
            pl.BlockSpec((2, 8, d), lambda i, t: (i, 0, 0)),
            pl.BlockSpec((d, nw), lambda i, t: (0, 0)),
            pl.BlockSpec((d, 128), lambda i, t: (0, 0)),
        ],
        out_specs=[
            pl.BlockSpec((None, TILE, nw), lambda i, t: (i, t, 0)),
            pl.BlockSpec((None, TILE, 128), lambda i, t: (i, t, 0)),
        ],
        out_shape=[
            jax.ShapeDtypeStruct((b, n, nw), BF16),
            jax.ShapeDtypeStruct((b, n, 128), F32),
        ],
        compiler_params=_params("arbitrary", "arbitrary"),
        name="gla_in",
    )(xs, g.reshape(1, d), mods, w_in.astype(BF16), w1)

    nt = n // TILE
    nct = n_ctx // TILE

    def rev_tile(s):
        return jnp.where(s < nct, nct - 1 - s, nt - 1 - (s - nct))

    qkv_w = 2 * kd + vd
    o_fwd, o_rev = pl.pallas_call(
        functools.partial(_gla_scan_kernel, dk, dv),
        grid=(b, nt),
        in_specs=[
            pl.BlockSpec((None, TILE, qkv_w), lambda i, s: (i, s, 0)),
            pl.BlockSpec((None, TILE, 128), lambda i, s: (i, s, 0)),
            pl.BlockSpec((None, TILE, qkv_w), lambda i, s: (i, rev_tile(s), 0)),
            pl.BlockSpec((None, TILE, 128), lambda i, s: (i, rev_tile(s), 0)),
            pl.BlockSpec(w2.shape, lambda i, s: (0, 0, 0)),
            pl.BlockSpec(gbias.shape, lambda i, s: (0, 0, 0)),
        ],
        out_specs=[
            pl.BlockSpec((None, TILE, vd), lambda i, s: (i, s, 0)),
            pl.BlockSpec((None, TILE, vd), lambda i, s: (i, rev_tile(s), 0)),
        ],
        out_shape=[jax.ShapeDtypeStruct((b, n, vd), F32), jax.ShapeDtypeStruct((b, n, vd), F32)],
        scratch_shapes=[pltpu.VMEM((2, GLA_HEADS, dk, dv), F32)],
        compiler_params=_params("arbitrary", "arbitrary"),
        name="gla_scan",
    )(qkvg, r, qkvg, r, w2, gbias)

    return pl.pallas_call(
        functools.partial(_gla_out_kernel, n_ctx),
        grid=(b, nt),
        in_specs=[
            pl.BlockSpec((None, TILE, d), lambda i, t: (i, t, 0)),
            pl.BlockSpec((None, TILE, vd), lambda i, t: (i, t, 0)),
            pl.BlockSpec((None, TILE, vd), lambda i, t: (i, t, 0)),
            pl.BlockSpec((None, TILE, vd), lambda i, t: (i, t, (2 * kd + vd) // vd)),
            pl.BlockSpec((1, dv), lambda i, t: (0, 0)),
            pl.BlockSpec((vd, d), lambda i, t: (0, 0)),
            pl.BlockSpec((2, 8, d), lambda i, t: (i, 0, 0)),
        ],
        out_specs=pl.BlockSpec((None, TILE, d), lambda i, t: (i, t, 0)),
        out_shape=jax.ShapeDtypeStruct(xs.shape, F32),
        input_output_aliases={0: 0},
        compiler_params=_params("arbitrary", "arbitrary"),
        name="gla_out",
    )(xs, o_fwd, o_rev, qkvg, o_norm.reshape(1, dv), w_o.astype(BF16), mods)


PEER_TT = 768
PEER_EB = 2048
PEER_SUB = 256
PEER_LAYER_CONFIG = ((2048, 256), (2048, 1024), (1024, 512), (1024, 1024))
PEER_RANKS = PEER_TOPK + 1


SUBLANES = 8


def _sorting_network(n):
    pairs = []
    p = 1
    while p < n:
        k = p
        while k >= 1:
            for j in range(k % p, n - k, 2 * k):
                for i in range(min(k, n - j - k)):
                    if (i + j) // (2 * p) == (i + j + k) // (2 * p):
                        pairs.append((i + j, i + j + k))
            k //= 2
        p *= 2
    return pairs


def _sublane_max_all(x):
    for shift in (4, 2, 1):
        x = jnp.maximum(x, pltpu.roll(x, shift, 0))
    return x


def _pop_top(lists, k):
    neg = jnp.full(lists[0].shape, -jnp.inf, F32)
    out = []
    for r in range(k):
        m = _sublane_max_all(lists[0])
        out.append(m)
        if r == k - 1:
            break
        hit = lists[0] == m
        keep = min(len(lists), k - 1 - r)
        lists = [jnp.where(hit, lists[p + 1] if p + 1 < len(lists) else neg, lists[p]) for p in range(keep)]
    return out


def _top_rows(s, k):
    tiles = [s[SUBLANES * v:SUBLANES * (v + 1), :] for v in range(s.shape[0] // SUBLANES)]
    for i, j in _sorting_network(len(tiles)):
        tiles[i], tiles[j] = jnp.maximum(tiles[i], tiles[j]), jnp.minimum(tiles[i], tiles[j])
    return _pop_top(tiles, k)


def _pair_threshold(a, b):
    k = PEER_RANKS
    lens = (k, k // 2, k // 3, k // 4, k - 4, k // 2 - 4, k // 3 - 4, 0)
    sub = lax.broadcasted_iota(jnp.int32, a[0].shape, 0)
    length = jnp.zeros(a[0].shape, jnp.int32)
    for c, ln in enumerate(lens):
        length = jnp.where(sub == c, ln, length)
    a_fix = jnp.where(sub == 0, a[0], jnp.where(sub == 1, a[1], jnp.where(sub == 2, a[2], a[3])))
    b_fix = jnp.where(sub == 4, b[0], jnp.where(sub == 5, b[1], b[2]))
    lists = []
    for p in range(k):
        a_p = jnp.where(sub < 4, a_fix, jnp.where(sub < 6, a[min(4 + p, k - 1)], a[4]))
        b_p = jnp.where(sub < 4, b[p], b_fix)
        lists.append(jnp.where(length > p, a_p + b_p, -jnp.inf))
    return _pop_top(lists, k)


def _peer_score_kernel(n_ctx, x_ref, g_ref, m_ref, wqt_ref, keys_ref, h_ref, e0_ref, e1_ref, th_ref, sc_ref):
    tt = x_ref.shape[0]
    is_ctx = _is_ctx(pl.program_id(1), tt, n_ctx)
    ht = _norm_mod(x_ref[...], g_ref[...], m_ref, 3, is_ctx).T.astype(BF16)
    h_ref[...] = ht
    qt = _dot(wqt_ref[...], ht).astype(BF16)
    dkey = keys_ref.shape[-1]
    for hp in range(2 * PEER_HEADS):
        sc_ref[hp] = _dot(keys_ref[hp], qt[hp * dkey:(hp + 1) * dkey, :])
    head_row = lax.broadcasted_iota(jnp.int32, th_ref.shape, 0)

    def head(hd, th_acc):
        th_parts = []
        for lt in range(tt // LANES):
            ls = slice(lt * LANES, (lt + 1) * LANES)
            s0 = sc_ref[2 * hd, :, ls]
            s1 = sc_ref[2 * hd + 1, :, ls]
            a = _top_rows(s0, PEER_RANKS)
            b = _top_rows(s1, PEER_RANKS)
            v = _pair_threshold(a, b)
            tau = 0.5 * (v[PEER_TOPK - 1] + v[PEER_TOPK])
            z = jnp.ones_like(v[0])
            for kk in range(1, PEER_TOPK):
                z = z + jnp.exp(v[kk] - v[0])
            rz = (1.0 / z)[0:1, :]
            e0_ref[hd, :, ls] = jnp.exp(s0 - a[0][0:1, :]) * rz
            e1_ref[hd, :, ls] = jnp.exp(s1 - b[0][0:1, :])
            th_parts.append(jnp.exp(tau - v[0])[0:1, :] * rz)
        return jnp.where(head_row == hd, jnp.concatenate(th_parts, axis=1), th_acc)

    th_ref[...] = lax.fori_loop(0, PEER_HEADS, head, jnp.zeros(th_ref.shape, F32))


def _gelu_tanh_sigmoid_form(x):
    c0 = -2.0 * 0.7978845608028654 * 1.4426950408889634
    c1 = c0 * 0.044715
    return x / (1.0 + jnp.exp2(x * (c0 + c1 * (x * x))))


def _peer_dense_kernel(n_ctx, sub, ht_ref, e0_ref, e1_ref, th_ref, u_ref, vt_ref, x_ref, m_ref, o_ref,
                       w_ref, acc_ref):
    e = pl.program_id(2)
    eb, tt = w_ref.shape
    nk = e1_ref.shape[1]

    @pl.when(e == 0)
    def _():
        acc_ref[...] = jnp.zeros_like(acc_ref)

    ht = ht_ref[...]
    for sb in range(eb // sub):
        a_sb = _dot(u_ref[sb * sub:(sb + 1) * sub, :], ht)
        for ii in range(sub // nk):
            i = sb * (sub // nk) + ii
            for lt in range(tt // LANES):
                ls = slice(lt * LANES, (lt + 1) * LANES)
                gsum = jnp.zeros((nk, LANES), F32)
                for hd in range(PEER_HEADS):
                    p = e0_ref[hd, i:i + 1, ls] * e1_ref[hd, :, ls]
                    gsum = gsum + jnp.where(p > th_ref[hd:hd + 1, ls], p, 0.0)
                act = _gelu_tanh_sigmoid_form(a_sb[ii * nk:(ii + 1) * nk, ls])
                w_ref[i * nk:(i + 1) * nk, ls] = (act * gsum).astype(BF16)
    acc_ref[...] += _dot(vt_ref[...], w_ref[...])

    @pl.when(e == pl.num_programs(2) - 1)
    def _():
        is_ctx = _is_ctx(pl.program_id(1), tt, n_ctx)
        o_ref[...] = x_ref[...] + _mod_row(m_ref, 5, is_ctx) * acc_ref[...].T


def peer_layer(xs, mods, g, n_ctx, w_q, keys, u_tab, v_tab, eb=PEER_EB, sub=PEER_SUB):
    b, n, d = xs.shape
    tt = PEER_TT
    ntt = n // tt
    ne = u_tab.shape[0]
    nk = PEER_NKEYS
    hq = w_q.shape[-1]
    wqt = jnp.transpose(w_q).astype(BF16)
    keys_b = keys.reshape(PEER_HEADS * 2, nk, keys.shape[-1]).astype(BF16)
    u_b = u_tab.astype(BF16)

    h2t, e0, e1, th = pl.pallas_call(
        functools.partial(_peer_score_kernel, n_ctx),
        grid=(b, ntt),
        in_specs=[
            pl.BlockSpec((None, tt, d), lambda i, t: (i, t, 0)),
            pl.BlockSpec((1, d), lambda i, t: (0, 0)),
            pl.BlockSpec((2, 8, d), lambda i, t: (i, 0, 0)),
            pl.BlockSpec((hq, d), lambda i, t: (0, 0)),
            pl.BlockSpec(keys_b.shape, lambda i, t: (0, 0, 0)),
        ],
        out_specs=[
            pl.BlockSpec((d, tt), lambda i, t: (0, i * ntt + t)),
            pl.BlockSpec((PEER_HEADS, nk, tt), lambda i, t: (0, 0, i * ntt + t)),
            pl.BlockSpec((PEER_HEADS, nk, tt), lambda i, t: (0, 0, i * ntt + t)),
            pl.BlockSpec((PEER_HEADS, tt), lambda i, t: (0, i * ntt + t)),
        ],
        out_shape=[
            jax.ShapeDtypeStruct((d, b * n), BF16),
            jax.ShapeDtypeStruct((PEER_HEADS, nk, b * n), F32),
            jax.ShapeDtypeStruct((PEER_HEADS, nk, b * n), F32),
            jax.ShapeDtypeStruct((PEER_HEADS, b * n), F32),
        ],
        scratch_shapes=[pltpu.VMEM((2 * PEER_HEADS, nk, tt), F32)],
        compiler_params=_params("arbitrary", "arbitrary"),
        name="peer_score",
    )(xs, g.reshape(1, d), mods, wqt, keys_b)

    nblk = ne // eb
    vt_b = jnp.transpose(v_tab.reshape(nblk, eb, d), (0, 2, 1)).astype(BF16)
    return pl.pallas_call(
        functools.partial(_peer_dense_kernel, n_ctx, sub),
        grid=(b, ntt, nblk),
        in_specs=[
            pl.BlockSpec((d, tt), lambda i, t, e: (0, i * ntt + t)),
            pl.BlockSpec((PEER_HEADS, eb // nk, tt), lambda i, t, e: (0, e, i * ntt + t)),
            pl.BlockSpec((PEER_HEADS, nk, tt), lambda i, t, e: (0, 0, i * ntt + t)),
            pl.BlockSpec((PEER_HEADS, tt), lambda i, t, e: (0, i * ntt + t)),
            pl.BlockSpec((eb, d), lambda i, t, e: (e, 0)),
            pl.BlockSpec((None, d, eb), lambda i, t, e: (e, 0, 0)),
            pl.BlockSpec((None, tt, d), lambda i, t, e: (i, t, 0)),
            pl.BlockSpec((2, 8, d), lambda i, t, e: (i, 0, 0)),
        ],
        out_specs=pl.BlockSpec((None, tt, d), lambda i, t, e: (i, t, 0)),
        out_shape=jax.ShapeDtypeStruct(xs.shape, F32),
        scratch_shapes=[
            pltpu.VMEM((eb, tt), BF16),
            pltpu.VMEM((d, tt), F32),
        ],
        input_output_aliases={6: 0},
        compiler_params=_params("arbitrary", "arbitrary", "arbitrary"),
        name="peer_dense",
    )(h2t, e0, e1, th, u_b, vt_b, xs, mods)


def _final_kernel(x_ref, g_ref, o_ref):
    o_ref[...] = _rms(x_ref[...], g_ref[...])


def final_norm(xs, g, n_ctx):
    b, n, d = xs.shape
    off = n_ctx // TILE
    return pl.pallas_call(
        _final_kernel,
        grid=(b, (n - n_ctx) // TILE),
        in_specs=[
            pl.BlockSpec((None, TILE, d), lambda i, t: (i, t + off, 0)),
            pl.BlockSpec((1, d), lambda i, t: (0, 0)),
        ],
        out_specs=pl.BlockSpec((None, TILE, d), lambda i, t: (i, t, 0)),
        out_shape=jax.ShapeDtypeStruct((b, n - n_ctx, d), F32),
        compiler_params=_params("arbitrary", "arbitrary"),
        name="final_norm",
    )(xs, g.reshape(1, d))


def kernel(x, c, ctx, c_ctx, norm_g, ada_w, ada_b, mla_w_in, mla_q_norm, mla_w_uq, mla_kv_norm, mla_w_ukv, mla_w_o, s5_lam_re, s5_lam_im, s5_b_re, s5_b_im, s5_c_re, s5_c_im, s5_log_step, s5_d, s5_w_glu, gla_w_in, gla_gk_w1, gla_gk_w2, gla_gk_b, gla_o_norm, gla_w_o, peer_w_q, peer_keys, peer_u, peer_v, final_g):
    b, n_lat, d = x.shape
    n_ctx = ctx.shape[1]
    depth = ada_w.shape[0]
    xs = jnp.concatenate([ctx, x], axis=1)

    r = 8 * ((b + 1 + 7) // 8)
    cond = jnp.zeros((r, d), F32).at[:b].set(c).at[b].set(c_ctx)
    ada = ada_all(cond, ada_w, ada_b).reshape(depth, r, 6, d)
    lat = ada[:, :b]
    ctxp = jnp.broadcast_to(ada[:, b:b + 1], lat.shape)
    mods_all = jnp.stack([ctxp, lat], axis=2)
    mods_all = jnp.pad(mods_all, ((0, 0), (0, 0), (0, 0), (0, 2), (0, 0))).reshape(depth, b * 2, 8, d)

    for i in range(depth):
        kind, j = i % N_MIXERS, i // N_MIXERS
        mods = mods_all[i]
        if kind == 0:
            xs = mixer_mla(xs, mods, norm_g[i, 0], n_ctx, mla_w_in[j], mla_q_norm[j], mla_w_uq[j],
                           mla_kv_norm[j], mla_w_ukv[j], mla_w_o[j])
        elif kind == 1:
            xs = mixer_s5(xs, mods, norm_g[i, 0], n_ctx, s5_lam_re[j], s5_lam_im[j], s5_b_re[j], s5_b_im[j],
                          s5_c_re[j], s5_c_im[j], s5_log_step[j], s5_d[j], s5_w_glu[j])
        else:
            xs = mixer_gla(xs, mods, norm_g[i, 0], n_ctx, gla_w_in[j], gla_gk_w1[j], gla_gk_w2[j],
                           gla_gk_b[j], gla_o_norm[j], gla_w_o[j])
        eb, sub = PEER_LAYER_CONFIG[i % len(PEER_LAYER_CONFIG)]
        xs = peer_layer(xs, mods, norm_g[i, 1], n_ctx, peer_w_q[i], peer_keys[i], peer_u[i], peer_v[i], eb, sub)
    return final_norm(xs, final_g, n_ctx)
```

```python
import functools
import math

import numpy as np
import jax
import jax.numpy as jnp
from jax import lax
from jax.experimental import pallas as pl
from jax.experimental.pallas import tpu as pltpu

F32 = jnp.float32
BF16 = jnp.bfloat16

EPS = 1e-6
GRID_W = 64
N_MIXERS = 3

MLA_HEADS = 16
MLA_NOPE = 64
MLA_ROPE = 32
MLA_V = 64
MLA_Q_LORA = 384
MLA_KV_LORA = 256
ROPE_BASE = 10000.0
MLA_HEAD_PAD = 128
MLA_HEADS_PER_STEP = 4

S5_GROUP = 16
S5_STATE = 64
S5_GROUPS_PER_BLOCK = 8
S5_CHUNK = 128

GLA_HEADS = 4
GLA_GATE_RANK = 16
GLA_GATE_NORM = 16.0
GLA_CHUNK = 64

PEER_HEADS = 8
PEER_NKEYS = 128
PEER_TOPK = 16

LANES = 128
TILE = 256
VMEM_LIMIT = 56 * 1024 * 1024


def _params(*sem):
    return pltpu.CompilerParams(dimension_semantics=sem, vmem_limit_bytes=VMEM_LIMIT)


def _gelu_tanh(x):
    return 0.5 * x * (1.0 + jnp.tanh(0.7978845608028654 * (x + 0.044715 * x * x * x)))


def _is_ctx(tile_idx, rows, n_ctx):
    pos = tile_idx * rows + lax.broadcasted_iota(jnp.int32, (rows, 1), 0)
    return pos < n_ctx


def _mod_row(m_ref, row, is_ctx):
    return jnp.where(is_ctx, m_ref[0, row:row + 1, :], m_ref[1, row:row + 1, :])


def _rms(x, g):
    return x * lax.rsqrt(jnp.mean(x * x, axis=-1, keepdims=True) + EPS) * g


def _norm_mod(x, g, m_ref, row0, is_ctx):
    return _rms(x, g) * (1.0 + _mod_row(m_ref, row0 + 1, is_ctx)) + _mod_row(m_ref, row0, is_ctx)


def _dot(a, b):
    return jnp.dot(a, b, preferred_element_type=F32)


def _dot_nt(a, b):
    return lax.dot_general(a, b, (((1,), (1,)), ((), ())), preferred_element_type=F32)


def _ada_kernel(c_ref, w_ref, b_ref, o_ref):
    c = c_ref[...]
    s = (c * jax.nn.sigmoid(c)).astype(BF16)
    o_ref[...] = _dot(s, w_ref[...].astype(BF16)) + b_ref[...]


def ada_all(cond, ada_w, ada_b):
    depth, d, n6 = ada_w.shape
    r = cond.shape[0]
    tn = 1024
    return pl.pallas_call(
        _ada_kernel,
        grid=(depth, n6 // tn),
        in_specs=[
            pl.BlockSpec((r, d), lambda l, j: (0, 0)),
            pl.BlockSpec((None, d, tn), lambda l, j: (l, 0, j)),
            pl.BlockSpec((None, 1, tn), lambda l, j: (l, 0, j)),
        ],
        out_specs=pl.BlockSpec((None, r, tn), lambda l, j: (l, 0, j)),
        out_shape=jax.ShapeDtypeStruct((depth, r, n6), F32),
        compiler_params=_params("arbitrary", "arbitrary"),
        name="ada",
    )(cond, ada_w, ada_b.reshape(depth, 1, n6))


def _proj_residual_kernel(n_ctx, gate_row, x_ref, y_ref, w_ref, m_ref, o_ref):
    is_ctx = _is_ctx(pl.program_id(1), x_ref.shape[0], n_ctx)
    f = _dot(y_ref[...], w_ref[...])
    o_ref[...] = x_ref[...] + _mod_row(m_ref, gate_row, is_ctx) * f


def proj_residual(xs, y, w, mods, n_ctx, gate_row):
    b, n, d = xs.shape
    k = y.shape[-1]
    return pl.pallas_call(
        functools.partial(_proj_residual_kernel, n_ctx, gate_row),
        grid=(b, n // TILE),
        in_specs=[
            pl.BlockSpec((None, TILE, d), lambda i, t: (i, t, 0)),
            pl.BlockSpec((None, TILE, k), lambda i, t: (i, t, 0)),
            pl.BlockSpec((k, d), lambda i, t: (0, 0)),
            pl.BlockSpec((2, 8, d), lambda i, t: (i, 0, 0)),
        ],
        out_specs=pl.BlockSpec((None, TILE, d), lambda i, t: (i, t, 0)),
        out_shape=jax.ShapeDtypeStruct(xs.shape, F32),
        input_output_aliases={0: 0},
        compiler_params=_params("arbitrary", "arbitrary"),
        name="proj_residual",
    )(xs, y, w, mods)


def _rope_tables(n_ctx, n_lat, scale):
    half = MLA_ROPE // 2
    rows_n = n_lat // GRID_W
    rows = np.repeat(np.arange(rows_n, dtype=np.float32), GRID_W)
    cols = np.tile(np.arange(GRID_W, dtype=np.float32), rows_n)
    inv = (ROPE_BASE ** (-np.arange(0, half, 2, dtype=np.float32) / half)).astype(np.float32)
    ang_r = rows[:, None] * inv
    ang_c = cols[:, None] * inv
    cos = np.concatenate([np.cos(ang_r), np.cos(ang_r), np.cos(ang_c), np.cos(ang_c)], axis=1)
    sin = np.concatenate([-np.sin(ang_r), np.sin(ang_r), -np.sin(ang_c), np.sin(ang_c)], axis=1)
    n = n_ctx + n_lat
    a = np.zeros((n, MLA_HEAD_PAD), np.float32)
    b = np.zeros((n, MLA_HEAD_PAD), np.float32)
    a[:, :MLA_NOPE] = 1.0
    a[:n_ctx, MLA_NOPE:MLA_NOPE + MLA_ROPE] = 1.0
    a[n_ctx:, MLA_NOPE:MLA_NOPE + MLA_ROPE] = cos
    b[n_ctx:, MLA_NOPE:MLA_NOPE + MLA_ROPE] = sin
    return jnp.asarray(a * scale), jnp.asarray(b * scale)


def _rope_swap_index():
    q = MLA_ROPE // 4
    base = np.arange(MLA_ROPE)
    return np.where((base % (2 * q)) < q, base + q, base - q)


def _mla_in_kernel(n_ctx, x_ref, g_ref, m_ref, wq_ref, wkv_ref, wkp_ref, qn_ref, kvn_ref,
                   wuq1_ref, wuq2_ref, wuk_ref, wuv_ref, aq_ref, bq_ref, ak_ref, bk_ref,
                   q_ref, k_ref, v_ref):
    is_ctx = _is_ctx(pl.program_id(1), x_ref.shape[0], n_ctx)
    hb = _norm_mod(x_ref[...], g_ref[...], m_ref, 0, is_ctx).astype(BF16)
    cq = _rms(_dot(hb, wq_ref[...]), qn_ref[...]).astype(BF16)
    ckv = _rms(_dot(hb, wkv_ref[...]), kvn_ref[...]).astype(BF16)
    kp = _dot(hb, wkp_ref[...])
    kpe = kp[:, :MLA_HEAD_PAD] * ak_ref[...] + kp[:, MLA_HEAD_PAD:] * bk_ref[...]
    y1 = _dot(cq, wuq1_ref[...])
    y2 = _dot(cq, wuq2_ref[...])
    kk = _dot(ckv, wuk_ref[...])
    aq = aq_ref[...]
    bq = bq_ref[...]
    for h in range(MLA_HEADS):
        sl = slice(h * MLA_HEAD_PAD, (h + 1) * MLA_HEAD_PAD)
        q_ref[:, sl] = (y1[:, sl] * aq + y2[:, sl] * bq).astype(BF16)
        k_ref[:, sl] = (kk[:, sl] + kpe).astype(BF16)
    v_ref[...] = _dot(ckv, wuv_ref[...]).astype(BF16)


def _mla_attn_kernel(n_ctx, q_ref, k_ref, v_ref, o_ref):
    n = k_ref.shape[0]

    def attend(nk):
        heads = range(MLA_HEADS_PER_STEP)
        ss = [_dot_nt(q_ref[:, h * MLA_HEAD_PAD:(h + 1) * MLA_HEAD_PAD],
                      k_ref[0:nk, h * MLA_HEAD_PAD:(h + 1) * MLA_HEAD_PAD]) for h in heads]
        ps = [jnp.exp(s - jnp.max(s, axis=-1, keepdims=True)) for s in ss]
        ls = [jnp.sum(p, axis=-1, keepdims=True) for p in ps]
        outs = [_dot(ps[h].astype(BF16), v_ref[0:nk, (h // 2) * 2 * MLA_V:(h // 2 + 1) * 2 * MLA_V]) / ls[h]
                for h in heads]
        lane = lax.broadcasted_iota(jnp.int32, outs[0].shape, 1)
        for pr in range(MLA_HEADS_PER_STEP // 2):
            o_ref[:, pr * 2 * MLA_V:(pr + 1) * 2 * MLA_V] = jnp.where(
                lane < MLA_V, outs[2 * pr], outs[2 * pr + 1]).astype(BF16)

    @pl.when(pl.program_id(2) * q_ref.shape[0] < n_ctx)
    def _():
        attend(n_ctx)

    @pl.when(pl.program_id(2) * q_ref.shape[0] >= n_ctx)
    def _():
        attend(n)


def mixer_mla(xs, mods, g, n_ctx, w_in, q_norm, w_uq, kv_norm, w_ukv, w_o):
    b, n, d = xs.shape
    hp = MLA_HEAD_PAD
    dq = MLA_NOPE + MLA_ROPE
    wq = w_in[:, :MLA_Q_LORA].astype(BF16)
    wkv = w_in[:, MLA_Q_LORA:MLA_Q_LORA + MLA_KV_LORA].astype(BF16)
    w_pe = w_in[:, MLA_Q_LORA + MLA_KV_LORA:]
    swap = _rope_swap_index()
    wkp = jnp.zeros((d, 2 * hp), F32)
    wkp = wkp.at[:, MLA_NOPE:dq].set(w_pe).at[:, hp + MLA_NOPE:hp + dq].set(w_pe[:, swap]).astype(BF16)
    uq = w_uq.reshape(MLA_Q_LORA, MLA_HEADS, dq)
    z = jnp.zeros((MLA_Q_LORA, MLA_HEADS, hp - dq), F32)
    wuq1 = jnp.concatenate([uq, z], axis=-1).reshape(MLA_Q_LORA, MLA_HEADS * hp).astype(BF16)
    zn = jnp.zeros((MLA_Q_LORA, MLA_HEADS, MLA_NOPE), F32)
    wuq2 = jnp.concatenate([zn, uq[:, :, MLA_NOPE:][:, :, swap], z], axis=-1)
    wuq2 = wuq2.reshape(MLA_Q_LORA, MLA_HEADS * hp).astype(BF16)
    ukv = w_ukv.reshape(MLA_KV_LORA, MLA_HEADS, MLA_NOPE + MLA_V)
    zk = jnp.zeros((MLA_KV_LORA, MLA_HEADS, hp - MLA_NOPE), F32)
    wuk = jnp.concatenate([ukv[:, :, :MLA_NOPE], zk], axis=-1).reshape(MLA_KV_LORA, MLA_HEADS * hp).astype(BF16)
    wuv = ukv[:, :, MLA_NOPE:].reshape(MLA_KV_LORA, MLA_HEADS * MLA_V).astype(BF16)
    aq, bq = _rope_tables(n_ctx, n - n_ctx, float(dq) ** -0.5)
    ak, bk = _rope_tables(n_ctx, n - n_ctx, 1.0)

    full = lambda shape: pl.BlockSpec(shape, lambda i, t: tuple(0 for _ in shape))
    tab = pl.BlockSpec((TILE, hp), lambda i, t: (t, 0))
    q, k, v = pl.pallas_call(
        functools.partial(_mla_in_kernel, n_ctx),
        grid=(b, n // TILE),
        in_specs=[
            pl.BlockSpec((None, TILE, d), lambda i, t: (i, t, 0)),
            full((1, d)),
            pl.BlockSpec((2, 8, d), lambda i, t: (i, 0, 0)),
            full(wq.shape), full(wkv.shape), full(wkp.shape),
            full((1, MLA_Q_LORA)), full((1, MLA_KV_LORA)),
            full(wuq1.shape), full(wuq2.shape), full(wuk.shape), full(wuv.shape),
            tab, tab, tab, tab,
        ],
        out_specs=[
            pl.BlockSpec((None, TILE, MLA_HEADS * hp), lambda i, t: (i, t, 0)),
            pl.BlockSpec((None, TILE, MLA_HEADS * hp), lambda i, t: (i, t, 0)),
            pl.BlockSpec((None, TILE, MLA_HEADS * MLA_V), lambda i, t: (i, t, 0)),
        ],
        out_shape=[
            jax.ShapeDtypeStruct((b, n, MLA_HEADS * hp), BF16),
            jax.ShapeDtypeStruct((b, n, MLA_HEADS * hp), BF16),
            jax.ShapeDtypeStruct((b, n, MLA_HEADS * MLA_V), BF16),
        ],
        compiler_params=_params("arbitrary", "arbitrary"),
        name="mla_in",
    )(xs, g.reshape(1, d), mods, wq, wkv, wkp, q_norm.reshape(1, -1), kv_norm.reshape(1, -1),
      wuq1, wuq2, wuk, wuv, aq, bq, ak, bk)

    hs = MLA_HEADS_PER_STEP
    o = pl.pallas_call(
        functools.partial(_mla_attn_kernel, n_ctx),
        grid=(b, MLA_HEADS // hs, n // TILE),
        in_specs=[
            pl.BlockSpec((None, TILE, hs * hp), lambda i, h, t: (i, t, h)),
            pl.BlockSpec((None, n, hs * hp), lambda i, h, t: (i, 0, h)),
            pl.BlockSpec((None, n, hs * MLA_V), lambda i, h, t: (i, 0, h)),
        ],
        out_specs=pl.BlockSpec((None, TILE, hs * MLA_V), lambda i, h, t: (i, t, h)),
        out_shape=jax.ShapeDtypeStruct((b, n, MLA_HEADS * MLA_V), BF16),
        compiler_params=_params("arbitrary", "arbitrary", "arbitrary"),
        name="mla_attn",
    )(q, k, v)
    return proj_residual(xs, o, w_o.astype(BF16), mods, n_ctx, 2)


def _s5_disc_kernel(lr_ref, li_ref, ls_ref, bre_ref, bim_ref, ar_ref, ai_ref, br_ref, bi_ref):
    lr = lr_ref[...]
    li = li_ref[...]
    dt = jnp.exp(ls_ref[...])
    mag = jnp.exp(lr * dt)
    ar = mag * jnp.cos(li * dt)
    ai = mag * jnp.sin(li * dt)
    den = lr * lr + li * li
    fr = ((ar - 1.0) * lr + ai * li) / den
    fi = (ai * lr - (ar - 1.0) * li) / den
    ar_ref[...] = ar
    ai_ref[...] = ai
    for c in range(S5_GROUP):
        br_ref[c] = fr * bre_ref[c] - fi * bim_ref[c]
        bi_ref[c] = fr * bim_ref[c] + fi * bre_ref[c]


def _s5_in_kernel(n_ctx, x_ref, g_ref, m_ref, u_ref):
    is_ctx = _is_ctx(pl.program_id(1), x_ref.shape[0], n_ctx)
    u_ref[...] = _norm_mod(x_ref[...], g_ref[...], m_ref, 0, is_ctx).astype(BF16)


def _s5_scan_kernel(nb, u_ref, bcat_ref, ar_ref, ai_ref, ccat_ref, y_ref, bu_ref, xs_ref, st_ref):
    d = pl.program_id(0)
    half = ar_ref.shape[-1]

    @pl.when(pl.program_id(2) == 0)
    def _():
        st_ref[...] = jnp.zeros_like(st_ref)

    bu_ref[...] = _dot(u_ref[...], bcat_ref[...])
    ar = jnp.broadcast_to(ar_ref[...], (nb, half))
    ai = jnp.broadcast_to(ai_ref[...], (nb, half))
    steps = u_ref.shape[0] // nb

    def step(i, carry):
        xr, xi = carry
        tt = jnp.where(d == 0, i, steps - 1 - i)
        r0 = pl.multiple_of(tt * nb, nb)
        nxr = ar * xr - ai * xi + bu_ref[pl.ds(r0, nb), 0:half]
        nxi = ar * xi + ai * xr + bu_ref[pl.ds(r0, nb), half:2 * half]
        xs_ref[pl.ds(r0, nb), 0:half] = nxr.astype(BF16)
        xs_ref[pl.ds(r0, nb), half:2 * half] = nxi.astype(BF16)
        return nxr, nxi

    xr, xi = lax.fori_loop(0, steps, step, (st_ref[:, 0:half], st_ref[:, half:2 * half]), unroll=2)
    st_ref[:, 0:half] = xr
    st_ref[:, half:2 * half] = xi
    y_ref[...] = _dot(xs_ref[...], ccat_ref[...])


def _s5_glu_kernel(n_ctx, x_ref, g_ref, m_ref, y_ref, dsk_ref, w_ref, o_ref):
    is_ctx = _is_ctx(pl.program_id(1), x_ref.shape[0], n_ctx)
    x = x_ref[...]
    u = _norm_mod(x, g_ref[...], m_ref, 0, is_ctx)
    y = y_ref[0] + y_ref[1]
    z = _dot(_gelu_tanh(y + dsk_ref[...] * u).astype(BF16), w_ref[...])
    dm = z.shape[-1] // 2
    out = z[:, :dm] * jax.nn.sigmoid(z[:, dm:])
    o_ref[...] = x + _mod_row(m_ref, 2, is_ctx) * out


def mixer_s5(xs, mods, g, n_ctx, lam_re, lam_im, b_re, b_im, c_re, c_im, log_step, d_skip, w_glu):
    b, n, d = xs.shape
    groups = d // S5_GROUP
    p = S5_STATE
    gb = S5_GROUPS_PER_BLOCK
    nblk = groups // gb
    cin = gb * S5_GROUP
    half = gb * p

    full = lambda shape: pl.BlockSpec(shape, lambda dd: tuple(0 for _ in shape))
    ar, ai, br, bi = pl.pallas_call(
        _s5_disc_kernel,
        grid=(2,),
        in_specs=[
            pl.BlockSpec((None, groups, p), lambda dd: (dd, 0, 0)),
            pl.BlockSpec((None, groups, p), lambda dd: (dd, 0, 0)),
            pl.BlockSpec((None, groups, 1), lambda dd: (dd, 0, 0)),
            pl.BlockSpec((None, S5_GROUP, groups, p), lambda dd: (dd, 0, 0, 0)),
            pl.BlockSpec((None, S5_GROUP, groups, p), lambda dd: (dd, 0, 0, 0)),
        ],
        out_specs=[
            pl.BlockSpec((None, groups, p), lambda dd: (dd, 0, 0)),
            pl.BlockSpec((None, groups, p), lambda dd: (dd, 0, 0)),
            pl.BlockSpec((None, S5_GROUP, groups, p), lambda dd: (dd, 0, 0, 0)),
            pl.BlockSpec((None, S5_GROUP, groups, p), lambda dd: (dd, 0, 0, 0)),
        ],
        out_shape=[
            jax.ShapeDtypeStruct((2, groups, p), F32),
            jax.ShapeDtypeStruct((2, groups, p), F32),
            jax.ShapeDtypeStruct((2, S5_GROUP, groups, p), F32),
            jax.ShapeDtypeStruct((2, S5_GROUP, groups, p), F32),
        ],
        compiler_params=_params("arbitrary"),
        name="s5_disc",
    )(lam_re, lam_im, log_step.reshape(2, groups, 1),
      jnp.transpose(b_re, (0, 3, 1, 2)), jnp.transpose(b_im, (0, 3, 1, 2)))

    eye = jnp.eye(gb, dtype=F32)

    def in_blocks(t):
        t = t.reshape(2, S5_GROUP, nblk, gb, p)
        return jnp.einsum('ab,dcjap->djacbp', eye, t).reshape(2, nblk, cin, half)

    bcat = jnp.concatenate([in_blocks(br), in_blocks(bi)], axis=-1).astype(BF16)

    def out_blocks(t):
        t = t.reshape(2, nblk, gb, S5_GROUP, p)
        return jnp.einsum('ab,djacp->djapbc', eye, t).reshape(2, nblk, half, cin)

    ccat = jnp.concatenate([out_blocks(c_re), -out_blocks(c_im)], axis=2).astype(BF16)
    ar_b = ar.reshape(2, nblk, 1, half)
    ai_b = ai.reshape(2, nblk, 1, half)

    u_tm = pl.pallas_call(
        functools.partial(_s5_in_kernel, n_ctx),
        grid=(b, n // TILE),
        in_specs=[
            pl.BlockSpec((None, TILE, d), lambda i, t: (i, t, 0)),
            pl.BlockSpec((1, d), lambda i, t: (0, 0)),
            pl.BlockSpec((2, 8, d), lambda i, t: (i, 0, 0)),
        ],
        out_specs=pl.BlockSpec((TILE, d), lambda i, t: (t, i)),
        out_shape=jax.ShapeDtypeStruct((n, b * d), BF16),
        compiler_params=_params("arbitrary", "arbitrary"),
        name="s5_in",
    )(xs, g.reshape(1, d), mods)

    tc = S5_CHUNK
    nchunks = n // tc
    ncc = n_ctx // tc
    rows = tc * b

    def chunk_of(dd, s):
        rev = jnp.where(s < ncc, ncc - 1 - s, nchunks - 1 - (s - ncc))
        return jnp.where(dd == 0, s, rev)

    y2 = pl.pallas_call(
        functools.partial(_s5_scan_kernel, b),
        grid=(2, nblk, nchunks),
        in_specs=[
            pl.BlockSpec((rows, cin), lambda dd, j, s: (chunk_of(dd, s), j)),
            pl.BlockSpec((None, None, cin, 2 * half), lambda dd, j, s: (dd, j, 0, 0)),
            pl.BlockSpec((None, None, 1, half), lambda dd, j, s: (dd, j, 0, 0)),
            pl.BlockSpec((None, None, 1, half), lambda dd, j, s: (dd, j, 0, 0)),
            pl.BlockSpec((None, None, 2 * half, cin), lambda dd, j, s: (dd, j, 0, 0)),
        ],
        out_specs=pl.BlockSpec((None, rows, cin), lambda dd, j, s: (dd, chunk_of(dd, s), j)),
        out_shape=jax.ShapeDtypeStruct((2, n * b, d), F32),
        scratch_shapes=[
            pltpu.VMEM((rows, 2 * half), F32),
            pltpu.VMEM((rows, 2 * half), BF16),
            pltpu.VMEM((b, 2 * half), F32),
        ],
        compiler_params=_params("arbitrary", "arbitrary", "arbitrary"),
        name="s5_scan",
    )(u_tm.reshape(n * b, d), bcat, ar_b, ai_b, ccat)

    return pl.pallas_call(
        functools.partial(_s5_glu_kernel, n_ctx),
        grid=(b, n // TILE),
        in_specs=[
            pl.BlockSpec((None, TILE, d), lambda i, t: (i, t, 0)),
            pl.BlockSpec((1, d), lambda i, t: (0, 0)),
            pl.BlockSpec((2, 8, d), lambda i, t: (i, 0, 0)),
            pl.BlockSpec((2, TILE, d), lambda i, t: (0, t, i)),
            pl.BlockSpec((1, d), lambda i, t: (0, 0)),
            pl.BlockSpec((d, 2 * d), lambda i, t: (0, 0)),
        ],
        out_specs=pl.BlockSpec((None, TILE, d), lambda i, t: (i, t, 0)),
        out_shape=jax.ShapeDtypeStruct(xs.shape, F32),
        input_output_aliases={0: 0},
        compiler_params=_params("arbitrary", "arbitrary"),
        name="s5_glu",
    )(xs, g.reshape(1, d), mods, y2.reshape(2, n, b * d), d_skip.reshape(1, d), w_glu.astype(BF16))


def _gla_in_kernel(n_ctx, x_ref, g_ref, m_ref, w_ref, w1_ref, o_ref, r_ref):
    is_ctx = _is_ctx(pl.program_id(1), x_ref.shape[0], n_ctx)
    hb = _norm_mod(x_ref[...], g_ref[...], m_ref, 0, is_ctx).astype(BF16)
    o_ref[...] = _dot(hb, w_ref[...]).astype(BF16)
    r_ref[...] = _dot(hb, w1_ref[...])


def _gla_scan_kernel(dk, dv, qf_ref, rf_ref, qr_ref, rr_ref, w2_ref, gb_ref, of_ref, or_ref, s_ref):
    c = GLA_CHUNK
    nck = qf_ref.shape[0] // c
    kd = GLA_HEADS * dk

    @pl.when(pl.program_id(1) == 0)
    def _():
        s_ref[...] = jnp.zeros_like(s_ref)

    row = lax.broadcasted_iota(jnp.int32, (c, c), 0)
    col = lax.broadcasted_iota(jnp.int32, (c, c), 1)
    masks = (row >= col, row <= col)

    refs = ((qf_ref, rf_ref, of_ref), (qr_ref, rr_ref, or_ref))
    chains = [(dd, h) for dd in range(2) for h in range(GLA_HEADS)]
    for i in range(nck):
        rows = [slice(i * c, (i + 1) * c), slice((nck - 1 - i) * c, (nck - i) * c)]
        decay = []
        for dd in range(2):
            rb = refs[dd][1][rows[dd], :].astype(BF16)
            lg = jax.nn.log_sigmoid(_dot(rb, w2_ref[dd]) + gb_ref[dd]) * (1.0 / GLA_GATE_NORM)
            bcum = jnp.dot(masks[dd].astype(F32), lg, preferred_element_type=F32,
                           precision=lax.Precision.HIGHEST)
            blast = jnp.sum(lg, axis=0, keepdims=True)
            decay.append((jnp.exp(bcum), jnp.exp(-bcum), jnp.exp(blast - bcum), jnp.exp(blast)))
        qg, att, kdec, vv, st = {}, {}, {}, {}, {}
        for ch in chains:
            dd, h = ch
            x_ref = refs[dd][0]
            hs = slice(h * dk, (h + 1) * dk)
            q = x_ref[rows[dd], hs].astype(F32) * (float(dk) ** -0.5)
            k = x_ref[rows[dd], kd + h * dk:kd + (h + 1) * dk].astype(F32)
            vv[ch] = x_ref[rows[dd], 2 * kd + h * dv:2 * kd + (h + 1) * dv]
            qg[ch] = (q * decay[dd][0][:, hs]).astype(BF16)
            kg = (k * decay[dd][1][:, hs]).astype(BF16)
            kdec[ch] = k * decay[dd][2][:, hs]
            att[ch] = jnp.where(masks[dd], _dot_nt(qg[ch], kg), 0.0).astype(BF16)
        for ch in chains:
            dd, h = ch
            st[ch] = s_ref[dd, h]
            refs[dd][2][rows[dd], h * dv:(h + 1) * dv] = (
                _dot(att[ch], vv[ch]) + _dot(qg[ch], st[ch].astype(BF16)))
        for ch in chains:
            dd, h = ch
            eb = jnp.broadcast_to(decay[dd][3][:, h * dk:(h + 1) * dk], (dk - c, dk))
            mt = jnp.concatenate([kdec[ch], eb], axis=0).T
            s_ref[dd, h] = mt[:, c:c + 1] * st[ch] + _dot(mt[:, 0:c].astype(BF16), vv[ch])


def _gla_out_kernel(n_ctx, x_ref, of_ref, or_ref, gt_ref, on_ref, w_ref, m_ref, o_ref):
    is_ctx = _is_ctx(pl.program_id(1), x_ref.shape[0], n_ctx)
    o = of_ref[...] + or_ref[...]
    gt = gt_ref[...].astype(F32)
    dv = on_ref.shape[-1]
    parts = []
    for h in range(GLA_HEADS):
        sl = slice(h * dv, (h + 1) * dv)
        parts.append(_rms(o[:, sl], on_ref[...]) * (gt[:, sl] * jax.nn.sigmoid(gt[:, sl])))
    y = jnp.concatenate(parts, axis=-1).astype(BF16)
    o_ref[...] = x_ref[...] + _mod_row(m_ref, 2, is_ctx) * _dot(y, w_ref[...])


def mixer_gla(xs, mods, g, n_ctx, w_in, gk_w1, gk_w2, gk_b, o_norm, w_o):
    b, n, d = xs.shape
    kd = d // 2
    vd = d
    dk = kd // GLA_HEADS
    dv = vd // GLA_HEADS
    rk = GLA_GATE_RANK
    w1 = jnp.zeros((d, 128), F32).at[:, :rk].set(gk_w1[0]).at[:, rk:2 * rk].set(gk_w1[1]).astype(BF16)
    w2 = jnp.zeros((2, 128, kd), F32)
    for dd in range(2):
        w2 = w2.at[dd, dd * rk:(dd + 1) * rk, :].set(gk_w2[dd])
    w2 = w2.astype(BF16)
    gbias = gk_b.reshape(2, 1, kd)
    nw = w_in.shape[-1]

    qkvg, r = pl.pallas_call(
        functools.partial(_gla_in_kernel, n_ctx),
        grid=(b, n // TILE),
        in_specs=[
            pl.BlockSpec((None, TILE, d), lambda i, t: (i, t, 0)),
            pl.BlockSpec((1, d), lambda i, t: (0, 0)),
            pl.BlockSpec((2, 8, d), lambda i, t: (i, 0, 0)),
            pl.BlockSpec((d, nw), lambda i, t: (0, 0)),
            pl.BlockSpec((d, 128), lambda i, t: (0, 0)),
        ],
        out_specs=[
            pl.BlockSpec((None, TILE, nw), lambda i, t: (i, t, 0)),
            pl.BlockSpec((None, TILE, 128), lambda i, t: (i, t, 0)),
        ],
        out_shape=[
            jax.ShapeDtypeStruct((b, n, nw), BF16),
            jax.ShapeDtypeStruct((b, n, 128), F32),
        ],
        compiler_params=_params("arbitrary", "arbitrary"),
        name="gla_in",
    )(xs, g.reshape(1, d), mods, w_in.astype(BF16), w1)

    nt = n // TILE
    nct = n_ctx // TILE

    def rev_tile(s):
        return jnp.where(s < nct, nct - 1 - s, nt - 1 - (s - nct))

    qkv_w = 2 * kd + vd
    o_fwd, o_rev = pl.pallas_call(
        functools.partial(_gla_scan_kernel, dk, dv),
        grid=(b, nt),
        in_specs=[
            pl.BlockSpec((None, TILE, qkv_w), lambda i, s: (i, s, 0)),
            pl.BlockSpec((None, TILE, 128), lambda i, s: (i, s, 0)),
            pl.BlockSpec((None, TILE, qkv_w), lambda i, s: (i, rev_tile(s), 0)),
            pl.BlockSpec((None, TILE, 128), lambda i, s: (i, rev_tile(s), 0)),
            pl.BlockSpec(w2.shape, lambda i, s: (0, 0, 0)),
            pl.BlockSpec(gbias.shape, lambda i, s: (0, 0, 0)),
        ],
        out_specs=[
            pl.BlockSpec((None, TILE, vd), lambda i, s: (i, s, 0)),
            pl.BlockSpec((None, TILE, vd), lambda i, s: (i, rev_tile(s), 0)),
        ],
        out_shape=[jax.ShapeDtypeStruct((b, n, vd), F32), jax.ShapeDtypeStruct((b, n, vd), F32)],
        scratch_shapes=[pltpu.VMEM((2, GLA_HEADS, dk, dv), F32)],
        compiler_params=_params("arbitrary", "arbitrary"),
        name="gla_scan",
    )(qkvg, r, qkvg, r, w2, gbias)

    return pl.pallas_call(
        functools.partial(_gla_out_kernel, n_ctx),
        grid=(b, nt),
        in_specs=[
            pl.BlockSpec((None, TILE, d), lambda i, t: (i, t, 0)),
            pl.BlockSpec((None, TILE, vd), lambda i, t: (i, t, 0)),
            pl.BlockSpec((None, TILE, vd), lambda i, t: (i, t, 0)),
            pl.BlockSpec((None, TILE, vd), lambda i, t: (i, t, (2 * kd + vd) // vd)),
            pl.BlockSpec((1, dv), lambda i, t: (0, 0)),
            pl.BlockSpec((vd, d), lambda i, t: (0, 0)),
            pl.BlockSpec((2, 8, d), lambda i, t: (i, 0, 0)),
        ],
        out_specs=pl.BlockSpec((None, TILE, d), lambda i, t: (i, t, 0)),
        out_shape=jax.ShapeDtypeStruct(xs.shape, F32),
        input_output_aliases={0: 0},
        compiler_params=_params("arbitrary", "arbitrary"),
        name="gla_out",
    )(xs, o_fwd, o_rev, qkvg, o_norm.reshape(1, dv), w_o.astype(BF16), mods)


PEER_TT = 768
PEER_EB = 2048
PEER_SUB = 1024
PEER_LAYER_CONFIG = ((2048, 1024), (2048, 2048), (2048, 512), (2048, 1024))
PEER_RANKS = PEER_TOPK + 1


SUBLANES = 8


def _sorting_network(n):
    pairs = []
    p = 1
    while p < n:
        k = p
        while k >= 1:
            for j in range(k % p, n - k, 2 * k):
                for i in range(min(k, n - j - k)):
                    if (i + j) // (2 * p) == (i + j + k) // (2 * p):
                        pairs.append((i + j, i + j + k))
            k //= 2
        p *= 2
    return pairs


def _sublane_max_all(x):
    for shift in (4, 2, 1):
        x = jnp.maximum(x, pltpu.roll(x, shift, 0))
    return x


def _pop_top(lists, k):
    neg = jnp.full(lists[0].shape, -jnp.inf, F32)
    out = []
    for r in range(k):
        m = _sublane_max_all(lists[0])
        out.append(m)
        if r == k - 1:
            break
        hit = lists[0] == m
        keep = min(len(lists), k - 1 - r)
        lists = [jnp.where(hit, lists[p + 1] if p + 1 < len(lists) else neg, lists[p]) for p in range(keep)]
    return out


def _top_rows(s, k):
    tiles = [s[SUBLANES * v:SUBLANES * (v + 1), :] for v in range(s.shape[0] // SUBLANES)]
    for i, j in _sorting_network(len(tiles)):
        tiles[i], tiles[j] = jnp.maximum(tiles[i], tiles[j]), jnp.minimum(tiles[i], tiles[j])
    return _pop_top(tiles, k)


def _pair_threshold(a, b):
    k = PEER_RANKS
    lens = (k, k // 2, k // 3, k // 4, k - 4, k // 2 - 4, k // 3 - 4, 0)
    sub = lax.broadcasted_iota(jnp.int32, a[0].shape, 0)
    length = jnp.zeros(a[0].shape, jnp.int32)
    for c, ln in enumerate(lens):
        length = jnp.where(sub == c, ln, length)
    a_fix = jnp.where(sub == 0, a[0], jnp.where(sub == 1, a[1], jnp.where(sub == 2, a[2], a[3])))
    b_fix = jnp.where(sub == 4, b[0], jnp.where(sub == 5, b[1], b[2]))
    lists = []
    for p in range(k):
        a_p = jnp.where(sub < 4, a_fix, jnp.where(sub < 6, a[min(4 + p, k - 1)], a[4]))
        b_p = jnp.where(sub < 4, b[p], b_fix)
        lists.append(jnp.where(length > p, a_p + b_p, -jnp.inf))
    return _pop_top(lists, k)


def _peer_score_kernel(n_ctx, x_ref, g_ref, m_ref, wqt_ref, keys_ref, h_ref, e0_ref, e1_ref, th_ref, sc_ref):
    tt = x_ref.shape[0]
    is_ctx = _is_ctx(pl.program_id(1), tt, n_ctx)
    ht = _norm_mod(x_ref[...], g_ref[...], m_ref, 3, is_ctx).T.astype(BF16)
    h_ref[...] = ht
    qt = _dot(wqt_ref[...], ht).astype(BF16)
    dkey = keys_ref.shape[-1]
    for hp in range(2 * PEER_HEADS):
        sc_ref[hp] = _dot(keys_ref[hp], qt[hp * dkey:(hp + 1) * dkey, :])
    head_row = lax.broadcasted_iota(jnp.int32, th_ref.shape, 0)

    def head(hd, th_acc):
        th_parts = []
        for lt in range(tt // LANES):
            ls = slice(lt * LANES, (lt + 1) * LANES)
            s0 = sc_ref[2 * hd, :, ls]
            s1 = sc_ref[2 * hd + 1, :, ls]
            a = _top_rows(s0, PEER_RANKS)
            b = _top_rows(s1, PEER_RANKS)
            v = _pair_threshold(a, b)
            tau = 0.5 * (v[PEER_TOPK - 1] + v[PEER_TOPK])
            z = jnp.ones_like(v[0])
            for kk in range(1, PEER_TOPK):
                z = z + jnp.exp(v[kk] - v[0])
            rz = (1.0 / z)[0:1, :]
            e0_ref[hd, :, ls] = jnp.exp(s0 - a[0][0:1, :]) * rz
            e1_ref[hd, :, ls] = jnp.exp(s1 - b[0][0:1, :])
            th_parts.append(jnp.exp(tau - v[0])[0:1, :] * rz)
        return jnp.where(head_row == hd, jnp.concatenate(th_parts, axis=1), th_acc)

    th_ref[...] = lax.fori_loop(0, PEER_HEADS, head, jnp.zeros(th_ref.shape, F32))


def _gelu_tanh_sigmoid_form(x):
    c0 = -2.0 * 0.7978845608028654 * 1.4426950408889634
    c1 = c0 * 0.044715
    return x / (1.0 + jnp.exp2(x * (c0 + c1 * (x * x))))


def _peer_dense_kernel(n_ctx, sub, ht_ref, e0_ref, e1_ref, th_ref, u_ref, vt_ref, x_ref, m_ref, o_ref,
                       w_ref, acc_ref):
    e = pl.program_id(2)
    eb, tt = w_ref.shape
    nk = e1_ref.shape[1]

    @pl.when(e == 0)
    def _():
        acc_ref[...] = jnp.zeros_like(acc_ref)

    ht = ht_ref[...]
    for sb in range(eb // sub):
        a_sb = _dot(u_ref[sb * sub:(sb + 1) * sub, :], ht)
        for ii in range(sub // nk):
            i = sb * (sub // nk) + ii
            for lt in range(tt // LANES):
                ls = slice(lt * LANES, (lt + 1) * LANES)
                gsum = jnp.zeros((nk, LANES), F32)
                for hd in range(PEER_HEADS):
                    p = e0_ref[hd, i:i + 1, ls] * e1_ref[hd, :, ls]
                    gsum = gsum + jnp.where(p > th_ref[hd:hd + 1, ls], p, 0.0)
                act = _gelu_tanh_sigmoid_form(a_sb[ii * nk:(ii + 1) * nk, ls])
                w_ref[i * nk:(i + 1) * nk, ls] = (act * gsum).astype(BF16)
    acc_ref[...] += _dot(vt_ref[...], w_ref[...])

    @pl.when(e == pl.num_programs(2) - 1)
    def _():
        is_ctx = _is_ctx(pl.program_id(1), tt, n_ctx)
        o_ref[...] = x_ref[...] + _mod_row(m_ref, 5, is_ctx) * acc_ref[...].T


def peer_layer(xs, mods, g, n_ctx, w_q, keys, u_tab, v_tab, eb=PEER_EB, sub=PEER_SUB):
    b, n, d = xs.shape
    tt = PEER_TT
    ntt = n // tt
    ne = u_tab.shape[0]
    nk = PEER_NKEYS
    hq = w_q.shape[-1]
    wqt = jnp.transpose(w_q).astype(BF16)
    keys_b = keys.reshape(PEER_HEADS * 2, nk, keys.shape[-1]).astype(BF16)
    u_b = u_tab.astype(BF16)

    h2t, e0, e1, th = pl.pallas_call(
        functools.partial(_peer_score_kernel, n_ctx),
        grid=(b, ntt),
        in_specs=[
            pl.BlockSpec((None, tt, d), lambda i, t: (i, t, 0)),
            pl.BlockSpec((1, d), lambda i, t: (0, 0)),
            pl.BlockSpec((2, 8, d), lambda i, t: (i, 0, 0)),
            pl.BlockSpec((hq, d), lambda i, t: (0, 0)),
            pl.BlockSpec(keys_b.shape, lambda i, t: (0, 0, 0)),
        ],
        out_specs=[
            pl.BlockSpec((d, tt), lambda i, t: (0, i * ntt + t)),
            pl.BlockSpec((PEER_HEADS, nk, tt), lambda i, t: (0, 0, i * ntt + t)),
            pl.BlockSpec((PEER_HEADS, nk, tt), lambda i, t: (0, 0, i * ntt + t)),
            pl.BlockSpec((PEER_HEADS, tt), lambda i, t: (0, i * ntt + t)),
        ],
        out_shape=[
            jax.ShapeDtypeStruct((d, b * n), BF16),
            jax.ShapeDtypeStruct((PEER_HEADS, nk, b * n), F32),
            jax.ShapeDtypeStruct((PEER_HEADS, nk, b * n), F32),
            jax.ShapeDtypeStruct((PEER_HEADS, b * n), F32),
        ],
        scratch_shapes=[pltpu.VMEM((2 * PEER_HEADS, nk, tt), F32)],
        compiler_params=_params("arbitrary", "arbitrary"),
        name="peer_score",
    )(xs, g.reshape(1, d), mods, wqt, keys_b)

    nblk = ne // eb
    vt_b = jnp.transpose(v_tab.reshape(nblk, eb, d), (0, 2, 1)).astype(BF16)
    return pl.pallas_call(
        functools.partial(_peer_dense_kernel, n_ctx, sub),
        grid=(b, ntt, nblk),
        in_specs=[
            pl.BlockSpec((d, tt), lambda i, t, e: (0, i * ntt + t)),
            pl.BlockSpec((PEER_HEADS, eb // nk, tt), lambda i, t, e: (0, e, i * ntt + t)),
            pl.BlockSpec((PEER_HEADS, nk, tt), lambda i, t, e: (0, 0, i * ntt + t)),
            pl.BlockSpec((PEER_HEADS, tt), lambda i, t, e: (0, i * ntt + t)),
            pl.BlockSpec((eb, d), lambda i, t, e: (e, 0)),
            pl.BlockSpec((None, d, eb), lambda i, t, e: (e, 0, 0)),
            pl.BlockSpec((None, tt, d), lambda i, t, e: (i, t, 0)),
            pl.BlockSpec((2, 8, d), lambda i, t, e: (i, 0, 0)),
        ],
        out_specs=pl.BlockSpec((None, tt, d), lambda i, t, e: (i, t, 0)),
        out_shape=jax.ShapeDtypeStruct(xs.shape, F32),
        scratch_shapes=[
            pltpu.VMEM((eb, tt), BF16),
            pltpu.VMEM((d, tt), F32),
        ],
        input_output_aliases={6: 0},
        compiler_params=_params("arbitrary", "arbitrary", "arbitrary"),
        name="peer_dense",
    )(h2t, e0, e1, th, u_b, vt_b, xs, mods)


def _final_kernel(x_ref, g_ref, o_ref):
    o_ref[...] = _rms(x_ref[...], g_ref[...])


def final_norm(xs, g, n_ctx):
    b, n, d = xs.shape
    off = n_ctx // TILE
    return pl.pallas_call(
        _final_kernel,
        grid=(b, (n - n_ctx) // TILE),
        in_specs=[
            pl.BlockSpec((None, TILE, d), lambda i, t: (i, t + off, 0)),
            pl.BlockSpec((1, d), lambda i, t: (0, 0)),
        ],
        out_specs=pl.BlockSpec((None, TILE, d), lambda i, t: (i, t, 0)),
        out_shape=jax.ShapeDtypeStruct((b, n - n_ctx, d), F32),
        compiler_params=_params("arbitrary", "arbitrary"),
        name="final_norm",
    )(xs, g.reshape(1, d))


def kernel(x, c, ctx, c_ctx, norm_g, ada_w, ada_b, mla_w_in, mla_q_norm, mla_w_uq, mla_kv_norm, mla_w_ukv, mla_w_o, s5_lam_re, s5_lam_im, s5_b_re, s5_b_im, s5_c_re, s5_c_im, s5_log_step, s5_d, s5_w_glu, gla_w_in, gla_gk_w1, gla_gk_w2, gla_gk_b, gla_o_norm, gla_w_o, peer_w_q, peer_keys, peer_u, peer_v, final_g):
    b, n_lat, d = x.shape
    n_ctx = ctx.shape[1]
    depth = ada_w.shape[0]
    xs = jnp.concatenate([ctx, x], axis=1)

    r = 8 * ((b + 1 + 7) // 8)
    cond = jnp.zeros((r, d), F32).at[:b].set(c).at[b].set(c_ctx)
    ada = ada_all(cond, ada_w, ada_b).reshape(depth, r, 6, d)
    lat = ada[:, :b]
    ctxp = jnp.broadcast_to(ada[:, b:b + 1], lat.shape)
    mods_all = jnp.stack([ctxp, lat], axis=2)
    mods_all = jnp.pad(mods_all, ((0, 0), (0, 0), (0, 0), (0, 2), (0, 0))).reshape(depth, b * 2, 8, d)

    for i in range(depth):
        kind, j = i % N_MIXERS, i // N_MIXERS
        mods = mods_all[i]
        if kind == 0:
            xs = mixer_mla(xs, mods, norm_g[i, 0], n_ctx, mla_w_in[j], mla_q_norm[j], mla_w_uq[j],
                           mla_kv_norm[j], mla_w_ukv[j], mla_w_o[j])
        elif kind == 1:
            xs = mixer_s5(xs, mods, norm_g[i, 0], n_ctx, s5_lam_re[j], s5_lam_im[j], s5_b_re[j], s5_b_im[j],
                          s5_c_re[j], s5_c_im[j], s5_log_step[j], s5_d[j], s5_w_glu[j])
        else:
            xs = mixer_gla(xs, mods, norm_g[i, 0], n_ctx, gla_w_in[j], gla_gk_w1[j], gla_gk_w2[j],
                           gla_gk_b[j], gla_o_norm[j], gla_w_o[j])
        eb, sub = PEER_LAYER_CONFIG[i % len(PEER_LAYER_CONFIG)]
        xs = peer_layer(xs, mods, norm_g[i, 1], n_ctx, peer_w_q[i], peer_keys[i], peer_u[i], peer_v[i], eb, sub)
    return final_norm(xs, final_g, n_ctx)
```

```python
import functools
import math

import numpy as np
import jax
import jax.numpy as jnp
from jax import lax
from jax.experimental import pallas as pl
from jax.experimental.pallas import tpu as pltpu

F32 = jnp.float32
BF16 = jnp.bfloat16

EPS = 1e-6
GRID_W = 64
N_MIXERS = 3

MLA_HEADS = 16
MLA_NOPE = 64
MLA_ROPE = 32
MLA_V = 64
MLA_Q_LORA = 384
MLA_KV_LORA = 256
ROPE_BASE = 10000.0
MLA_HEAD_PAD = 128
MLA_HEADS_PER_STEP = 2
MLA_Q_ROWS = 1024

S5_GROUP = 16
S5_STATE = 64
S5_GROUPS_PER_BLOCK = 8
S5_CHUNK = 128

GLA_HEADS = 4
GLA_GATE_RANK = 16
GLA_GATE_NORM = 16.0
GLA_CHUNK = 64

PEER_HEADS = 8
PEER_NKEYS = 128
PEER_TOPK = 16

LANES = 128
TILE = 256
VMEM_LIMIT = 56 * 1024 * 1024


def _params(*sem):
    return pltpu.CompilerParams(dimension_semantics=sem, vmem_limit_bytes=VMEM_LIMIT)


def _gelu_tanh(x):
    return 0.5 * x * (1.0 + jnp.tanh(0.7978845608028654 * (x + 0.044715 * x * x * x)))


def _is_ctx(tile_idx, rows, n_ctx):
    pos = tile_idx * rows + lax.broadcasted_iota(jnp.int32, (rows, 1), 0)
    return pos < n_ctx


def _mod_row(m_ref, row, is_ctx):
    return jnp.where(is_ctx, m_ref[0, row:row + 1, :], m_ref[1, row:row + 1, :])


def _rms(x, g):
    return x * lax.rsqrt(jnp.mean(x * x, axis=-1, keepdims=True) + EPS) * g


def _norm_mod(x, g, m_ref, row0, is_ctx):
    return _rms(x, g) * (1.0 + _mod_row(m_ref, row0 + 1, is_ctx)) + _mod_row(m_ref, row0, is_ctx)


def _dot(a, b):
    return jnp.dot(a, b, preferred_element_type=F32)


def _dot_nt(a, b):
    return lax.dot_general(a, b, (((1,), (1,)), ((), ())), preferred_element_type=F32)


def _ada_kernel(c_ref, w_ref, b_ref, o_ref):
    c = c_ref[...]
    s = (c * jax.nn.sigmoid(c)).astype(BF16)
    o_ref[...] = _dot(s, w_ref[...].astype(BF16)) + b_ref[...]


def ada_all(cond, ada_w, ada_b):
    depth, d, n6 = ada_w.shape
    r = cond.shape[0]
    tn = 1024
    return pl.pallas_call(
        _ada_kernel,
        grid=(depth, n6 // tn),
        in_specs=[
            pl.BlockSpec((r, d), lambda l, j: (0, 0)),
            pl.BlockSpec((None, d, tn), lambda l, j: (l, 0, j)),
            pl.BlockSpec((None, 1, tn), lambda l, j: (l, 0, j)),
        ],
        out_specs=pl.BlockSpec((None, r, tn), lambda l, j: (l, 0, j)),
        out_shape=jax.ShapeDtypeStruct((depth, r, n6), F32),
        compiler_params=_params("arbitrary", "arbitrary"),
        name="ada",
    )(cond, ada_w, ada_b.reshape(depth, 1, n6))


def _proj_residual_kernel(n_ctx, gate_row, tile_off, x_ref, y_ref, w_ref, m_ref, o_ref):
    is_ctx = _is_ctx(pl.program_id(1) + tile_off, x_ref.shape[0], n_ctx)
    f = _dot(y_ref[...], w_ref[...])
    o_ref[...] = x_ref[...] + _mod_row(m_ref, gate_row, is_ctx) * f


def proj_residual(xs, y, w, mods, n_ctx, gate_row, tile_off=0):
    b, n, d = xs.shape
    k = y.shape[-1]
    return pl.pallas_call(
        functools.partial(_proj_residual_kernel, n_ctx, gate_row, tile_off),
        grid=(b, y.shape[1] // TILE),
        in_specs=[
            pl.BlockSpec((None, TILE, d), lambda i, t: (i, t + tile_off, 0)),
            pl.BlockSpec((None, TILE, k), lambda i, t: (i, t, 0)),
            pl.BlockSpec((k, d), lambda i, t: (0, 0)),
            pl.BlockSpec((2, 8, d), lambda i, t: (i, 0, 0)),
        ],
        out_specs=pl.BlockSpec((None, TILE, d), lambda i, t: (i, t + tile_off, 0)),
        out_shape=jax.ShapeDtypeStruct(xs.shape, F32),
        input_output_aliases={0: 0},
        compiler_params=_params("arbitrary", "arbitrary"),
        name="proj_residual",
    )(xs, y, w, mods)


def _rope_tables(n_ctx, n_lat, scale):
    half = MLA_ROPE // 2
    rows_n = n_lat // GRID_W
    rows = np.repeat(np.arange(rows_n, dtype=np.float32), GRID_W)
    cols = np.tile(np.arange(GRID_W, dtype=np.float32), rows_n)
    inv = (ROPE_BASE ** (-np.arange(0, half, 2, dtype=np.float32) / half)).astype(np.float32)
    ang_r = rows[:, None] * inv
    ang_c = cols[:, None] * inv
    cos = np.concatenate([np.cos(ang_r), np.cos(ang_r), np.cos(ang_c), np.cos(ang_c)], axis=1)
    sin = np.concatenate([-np.sin(ang_r), np.sin(ang_r), -np.sin(ang_c), np.sin(ang_c)], axis=1)
    n = n_ctx + n_lat
    a = np.zeros((n, MLA_HEAD_PAD), np.float32)
    b = np.zeros((n, MLA_HEAD_PAD), np.float32)
    a[:, :MLA_NOPE] = 1.0
    a[:n_ctx, MLA_NOPE:MLA_NOPE + MLA_ROPE] = 1.0
    a[n_ctx:, MLA_NOPE:MLA_NOPE + MLA_ROPE] = cos
    b[n_ctx:, MLA_NOPE:MLA_NOPE + MLA_ROPE] = sin
    return jnp.asarray(a * scale), jnp.asarray(b * scale)


def _rope_swap_index():
    q = MLA_ROPE // 4
    base = np.arange(MLA_ROPE)
    return np.where((base % (2 * q)) < q, base + q, base - q)


def _mla_in_kernel(n_ctx, x_ref, g_ref, m_ref, wq_ref, wkv_ref, wkp_ref, qn_ref, kvn_ref,
                   wuq1_ref, wuq2_ref, wuk_ref, wuv_ref, aq_ref, bq_ref, ak_ref, bk_ref,
                   qc_ref, ql_ref, k_ref, v_ref):
    is_ctx = _is_ctx(pl.program_id(1), x_ref.shape[0], n_ctx)
    hb = _norm_mod(x_ref[...], g_ref[...], m_ref, 0, is_ctx).astype(BF16)
    cq = _rms(_dot(hb, wq_ref[...]), qn_ref[...]).astype(BF16)
    ckv = _rms(_dot(hb, wkv_ref[...]), kvn_ref[...]).astype(BF16)
    kp = _dot(hb, wkp_ref[...])
    kpe = kp[:, :MLA_HEAD_PAD] * ak_ref[...] + kp[:, MLA_HEAD_PAD:] * bk_ref[...]
    y1 = _dot(cq, wuq1_ref[...])
    y2 = _dot(cq, wuq2_ref[...])
    kk = _dot(ckv, wuk_ref[...])
    aq = aq_ref[...]
    bq = bq_ref[...]
    qs = []
    for h in range(MLA_HEADS):
        sl = slice(h * MLA_HEAD_PAD, (h + 1) * MLA_HEAD_PAD)
        qs.append((y1[:, sl] * aq + y2[:, sl] * bq).astype(BF16))
        k_ref[:, sl] = (kk[:, sl] + kpe).astype(BF16)
    v_ref[...] = _dot(ckv, wuv_ref[...]).astype(BF16)
    in_ctx = pl.program_id(1) * x_ref.shape[0] < n_ctx

    @pl.when(in_ctx)
    def _():
        for h in range(MLA_HEADS):
            qc_ref[:, h * MLA_HEAD_PAD:(h + 1) * MLA_HEAD_PAD] = qs[h]

    @pl.when(jnp.logical_not(in_ctx))
    def _():
        for h in range(MLA_HEADS):
            ql_ref[:, h * MLA_HEAD_PAD:(h + 1) * MLA_HEAD_PAD] = qs[h]


def _mla_attn_kernel(q_ref, k_ref, v_ref, o_ref):
    heads = range(q_ref.shape[-1] // MLA_HEAD_PAD)
    ss = [_dot_nt(q_ref[:, h * MLA_HEAD_PAD:(h + 1) * MLA_HEAD_PAD],
                  k_ref[:, h * MLA_HEAD_PAD:(h + 1) * MLA_HEAD_PAD]) for h in heads]
    ps = [jnp.exp(s - jnp.max(s, axis=-1, keepdims=True)) for s in ss]
    ls = [jnp.sum(p, axis=-1, keepdims=True) for p in ps]
    outs = [_dot(ps[h].astype(BF16), v_ref[:, (h // 2) * 2 * MLA_V:(h // 2 + 1) * 2 * MLA_V]) / ls[h]
            for h in heads]
    lane = lax.broadcasted_iota(jnp.int32, outs[0].shape, 1)
    for pr in range(len(heads) // 2):
        o_ref[:, pr * 2 * MLA_V:(pr + 1) * 2 * MLA_V] = jnp.where(
            lane < MLA_V, outs[2 * pr], outs[2 * pr + 1]).astype(BF16)


def _mla_attention(q, k, v, n_keys, q_rows, hs):
    b, nq, _ = q.shape
    hp = MLA_HEAD_PAD
    return pl.pallas_call(
        _mla_attn_kernel,
        grid=(b, MLA_HEADS // hs, nq // q_rows),
        in_specs=[
            pl.BlockSpec((None, q_rows, hs * hp), lambda i, h, t: (i, t, h)),
            pl.BlockSpec((None, n_keys, hs * hp), lambda i, h, t: (i, 0, h)),
            pl.BlockSpec((None, n_keys, hs * MLA_V), lambda i, h, t: (i, 0, h)),
        ],
        out_specs=pl.BlockSpec((None, q_rows, hs * MLA_V), lambda i, h, t: (i, t, h)),
        out_shape=jax.ShapeDtypeStruct((b, nq, MLA_HEADS * MLA_V), BF16),
        compiler_params=_params("arbitrary", "arbitrary", "arbitrary"),
        name="mla_attn",
    )(q, k, v)


def mixer_mla(xs, mods, g, n_ctx, ctx_out, w_in, q_norm, w_uq, kv_norm, w_ukv, w_o):
    b, n, d = xs.shape
    hp = MLA_HEAD_PAD
    dq = MLA_NOPE + MLA_ROPE
    wq = w_in[:, :MLA_Q_LORA].astype(BF16)
    wkv = w_in[:, MLA_Q_LORA:MLA_Q_LORA + MLA_KV_LORA].astype(BF16)
    w_pe = w_in[:, MLA_Q_LORA + MLA_KV_LORA:]
    swap = _rope_swap_index()
    wkp = jnp.zeros((d, 2 * hp), F32)
    wkp = wkp.at[:, MLA_NOPE:dq].set(w_pe).at[:, hp + MLA_NOPE:hp + dq].set(w_pe[:, swap]).astype(BF16)
    uq = w_uq.reshape(MLA_Q_LORA, MLA_HEADS, dq)
    z = jnp.zeros((MLA_Q_LORA, MLA_HEADS, hp - dq), F32)
    wuq1 = jnp.concatenate([uq, z], axis=-1).reshape(MLA_Q_LORA, MLA_HEADS * hp).astype(BF16)
    zn = jnp.zeros((MLA_Q_LORA, MLA_HEADS, MLA_NOPE), F32)
    wuq2 = jnp.concatenate([zn, uq[:, :, MLA_NOPE:][:, :, swap], z], axis=-1)
    wuq2 = wuq2.reshape(MLA_Q_LORA, MLA_HEADS * hp).astype(BF16)
    ukv = w_ukv.reshape(MLA_KV_LORA, MLA_HEADS, MLA_NOPE + MLA_V)
    zk = jnp.zeros((MLA_KV_LORA, MLA_HEADS, hp - MLA_NOPE), F32)
    wuk = jnp.concatenate([ukv[:, :, :MLA_NOPE], zk], axis=-1).reshape(MLA_KV_LORA, MLA_HEADS * hp).astype(BF16)
    wuv = ukv[:, :, MLA_NOPE:].reshape(MLA_KV_LORA, MLA_HEADS * MLA_V).astype(BF16)
    aq, bq = _rope_tables(n_ctx, n - n_ctx, float(dq) ** -0.5)
    ak, bk = _rope_tables(n_ctx, n - n_ctx, 1.0)

    full = lambda shape: pl.BlockSpec(shape, lambda i, t: tuple(0 for _ in shape))
    tab = pl.BlockSpec((TILE, hp), lambda i, t: (t, 0))
    nct = n_ctx // TILE
    q_ctx, q_lat, k, v = pl.pallas_call(
        functools.partial(_mla_in_kernel, n_ctx),
        grid=(b, n // TILE),
        in_specs=[
            pl.BlockSpec((None, TILE, d), lambda i, t: (i, t, 0)),
            full((1, d)),
            pl.BlockSpec((2, 8, d), lambda i, t: (i, 0, 0)),
            full(wq.shape), full(wkv.shape), full(wkp.shape),
            full((1, MLA_Q_LORA)), full((1, MLA_KV_LORA)),
            full(wuq1.shape), full(wuq2.shape), full(wuk.shape), full(wuv.shape),
            tab, tab, tab, tab,
        ],
        out_specs=[
            pl.BlockSpec((None, TILE, MLA_HEADS * hp), lambda i, t: (i, jnp.minimum(t, nct - 1), 0)),
            pl.BlockSpec((None, TILE, MLA_HEADS * hp), lambda i, t: (i, jnp.maximum(t - nct, 0), 0)),
            pl.BlockSpec((None, TILE, MLA_HEADS * hp), lambda i, t: (i, t, 0)),
            pl.BlockSpec((None, TILE, MLA_HEADS * MLA_V), lambda i, t: (i, t, 0)),
        ],
        out_shape=[
            jax.ShapeDtypeStruct((b, n_ctx, MLA_HEADS * hp), BF16),
            jax.ShapeDtypeStruct((b, n - n_ctx, MLA_HEADS * hp), BF16),
            jax.ShapeDtypeStruct((b, n, MLA_HEADS * hp), BF16),
            jax.ShapeDtypeStruct((b, n, MLA_HEADS * MLA_V), BF16),
        ],
        compiler_params=_params("arbitrary", "arbitrary"),
        name="mla_in",
    )(xs, g.reshape(1, d), mods, wq, wkv, wkp, q_norm.reshape(1, -1), kv_norm.reshape(1, -1),
      wuq1, wuq2, wuk, wuv, aq, bq, ak, bk)

    wo = w_o.astype(BF16)
    o_lat = _mla_attention(q_lat, k, v, n, min(MLA_Q_ROWS, n - n_ctx), MLA_HEADS_PER_STEP)
    xs = proj_residual(xs, o_lat, wo, mods, n_ctx, 2, tile_off=nct)
    if ctx_out:
        o_ctx = _mla_attention(q_ctx, k, v, n_ctx, n_ctx, MLA_HEADS_PER_STEP)
        xs = proj_residual(xs, o_ctx, wo, mods, n_ctx, 2, tile_off=0)
    return xs


def _s5_disc_kernel(lr_ref, li_ref, ls_ref, bre_ref, bim_ref, ar_ref, ai_ref, br_ref, bi_ref):
    lr = lr_ref[...]
    li = li_ref[...]
    dt = jnp.exp(ls_ref[...])
    mag = jnp.exp(lr * dt)
    ar = mag * jnp.cos(li * dt)
    ai = mag * jnp.sin(li * dt)
    den = lr * lr + li * li
    fr = ((ar - 1.0) * lr + ai * li) / den
    fi = (ai * lr - (ar - 1.0) * li) / den
    ar_ref[...] = ar
    ai_ref[...] = ai
    for c in range(S5_GROUP):
        br_ref[c] = fr * bre_ref[c] - fi * bim_ref[c]
        bi_ref[c] = fr * bim_ref[c] + fi * bre_ref[c]


def _s5_in_kernel(n_ctx, x_ref, g_ref, m_ref, u_ref):
    is_ctx = _is_ctx(pl.program_id(1), x_ref.shape[0], n_ctx)
    u_ref[...] = _norm_mod(x_ref[...], g_ref[...], m_ref, 0, is_ctx).astype(BF16)


def _s5_scan_kernel(nb, u_ref, bcat_ref, ar_ref, ai_ref, ccat_ref, y_ref, bu_ref, xs_ref, st_ref):
    d = pl.program_id(0)
    half = ar_ref.shape[-1]

    @pl.when(pl.program_id(2) == 0)
    def _():
        st_ref[...] = jnp.zeros_like(st_ref)

    bu_ref[...] = _dot(u_ref[...], bcat_ref[...])
    ar = jnp.broadcast_to(ar_ref[...], (nb, half))
    ai = jnp.broadcast_to(ai_ref[...], (nb, half))
    steps = u_ref.shape[0] // nb

    def step(i, carry):
        xr, xi = carry
        tt = jnp.where(d == 0, i, steps - 1 - i)
        r0 = pl.multiple_of(tt * nb, nb)
        nxr = ar * xr - ai * xi + bu_ref[pl.ds(r0, nb), 0:half]
        nxi = ar * xi + ai * xr + bu_ref[pl.ds(r0, nb), half:2 * half]
        xs_ref[pl.ds(r0, nb), 0:half] = nxr.astype(BF16)
        xs_ref[pl.ds(r0, nb), half:2 * half] = nxi.astype(BF16)
        return nxr, nxi

    xr, xi = lax.fori_loop(0, steps, step, (st_ref[:, 0:half], st_ref[:, half:2 * half]), unroll=2)
    st_ref[:, 0:half] = xr
    st_ref[:, half:2 * half] = xi
    y_ref[...] = _dot(xs_ref[...], ccat_ref[...])


def _s5_glu_kernel(n_ctx, x_ref, g_ref, m_ref, y_ref, dsk_ref, w_ref, o_ref):
    is_ctx = _is_ctx(pl.program_id(1), x_ref.shape[0], n_ctx)
    x = x_ref[...]
    u = _norm_mod(x, g_ref[...], m_ref, 0, is_ctx)
    y = y_ref[0] + y_ref[1]
    z = _dot(_gelu_tanh(y + dsk_ref[...] * u).astype(BF16), w_ref[...])
    dm = z.shape[-1] // 2
    out = z[:, :dm] * jax.nn.sigmoid(z[:, dm:])
    o_ref[...] = x + _mod_row(m_ref, 2, is_ctx) * out


def mixer_s5(xs, mods, g, n_ctx, lam_re, lam_im, b_re, b_im, c_re, c_im, log_step, d_skip, w_glu):
    b, n, d = xs.shape
    groups = d // S5_GROUP
    p = S5_STATE
    gb = S5_GROUPS_PER_BLOCK
    nblk = groups // gb
    cin = gb * S5_GROUP
    half = gb * p

    full = lambda shape: pl.BlockSpec(shape, lambda dd: tuple(0 for _ in shape))
    ar, ai, br, bi = pl.pallas_call(
        _s5_disc_kernel,
        grid=(2,),
        in_specs=[
            pl.BlockSpec((None, groups, p), lambda dd: (dd, 0, 0)),
            pl.BlockSpec((None, groups, p), lambda dd: (dd, 0, 0)),
            pl.BlockSpec((None, groups, 1), lambda dd: (dd, 0, 0)),
            pl.BlockSpec((None, S5_GROUP, groups, p), lambda dd: (dd, 0, 0, 0)),
            pl.BlockSpec((None, S5_GROUP, groups, p), lambda dd: (dd, 0, 0, 0)),
        ],
        out_specs=[
            pl.BlockSpec((None, groups, p), lambda dd: (dd, 0, 0)),
            pl.BlockSpec((None, groups, p), lambda dd: (dd, 0, 0)),
            pl.BlockSpec((None, S5_GROUP, groups, p), lambda dd: (dd, 0, 0, 0)),
            pl.BlockSpec((None, S5_GROUP, groups, p), lambda dd: (dd, 0, 0, 0)),
        ],
        out_shape=[
            jax.ShapeDtypeStruct((2, groups, p), F32),
            jax.ShapeDtypeStruct((2, groups, p), F32),
            jax.ShapeDtypeStruct((2, S5_GROUP, groups, p), F32),
            jax.ShapeDtypeStruct((2, S5_GROUP, groups, p), F32),
        ],
        compiler_params=_params("arbitrary"),
        name="s5_disc",
    )(lam_re, lam_im, log_step.reshape(2, groups, 1),
      jnp.transpose(b_re, (0, 3, 1, 2)), jnp.transpose(b_im, (0, 3, 1, 2)))

    eye = jnp.eye(gb, dtype=F32)

    def in_blocks(t):
        t = t.reshape(2, S5_GROUP, nblk, gb, p)
        return jnp.einsum('ab,dcjap->djacbp', eye, t).reshape(2, nblk, cin, half)

    bcat = jnp.concatenate([in_blocks(br), in_blocks(bi)], axis=-1).astype(BF16)

    def out_blocks(t):
        t = t.reshape(2, nblk, gb, S5_GROUP, p)
        return jnp.einsum('ab,djacp->djapbc', eye, t).reshape(2, nblk, half, cin)

    ccat = jnp.concatenate([out_blocks(c_re), -out_blocks(c_im)], axis=2).astype(BF16)
    ar_b = ar.reshape(2, nblk, 1, half)
    ai_b = ai.reshape(2, nblk, 1, half)

    u_tm = pl.pallas_call(
        functools.partial(_s5_in_kernel, n_ctx),
        grid=(b, n // TILE),
        in_specs=[
            pl.BlockSpec((None, TILE, d), lambda i, t: (i, t, 0)),
            pl.BlockSpec((1, d), lambda i, t: (0, 0)),
            pl.BlockSpec((2, 8, d), lambda i, t: (i, 0, 0)),
        ],
        out_specs=pl.BlockSpec((TILE, d), lambda i, t: (t, i)),
        out_shape=jax.ShapeDtypeStruct((n, b * d), BF16),
        compiler_params=_params("arbitrary", "arbitrary"),
        name="s5_in",
    )(xs, g.reshape(1, d), mods)

    tc = S5_CHUNK
    nchunks = n // tc
    ncc = n_ctx // tc
    rows = tc * b

    def chunk_of(dd, s):
        rev = jnp.where(s < ncc, ncc - 1 - s, nchunks - 1 - (s - ncc))
        return jnp.where(dd == 0, s, rev)

    y2 = pl.pallas_call(
        functools.partial(_s5_scan_kernel, b),
        grid=(2, nblk, nchunks),
        in_specs=[
            pl.BlockSpec((rows, cin), lambda dd, j, s: (chunk_of(dd, s), j)),
            pl.BlockSpec((None, None, cin, 2 * half), lambda dd, j, s: (dd, j, 0, 0)),
            pl.BlockSpec((None, None, 1, half), lambda dd, j, s: (dd, j, 0, 0)),
            pl.BlockSpec((None, None, 1, half), lambda dd, j, s: (dd, j, 0, 0)),
            pl.BlockSpec((None, None, 2 * half, cin), lambda dd, j, s: (dd, j, 0, 0)),
        ],
        out_specs=pl.BlockSpec((None, rows, cin), lambda dd, j, s: (dd, chunk_of(dd, s), j)),
        out_shape=jax.ShapeDtypeStruct((2, n * b, d), F32),
        scratch_shapes=[
            pltpu.VMEM((rows, 2 * half), F32),
            pltpu.VMEM((rows, 2 * half), BF16),
            pltpu.VMEM((b, 2 * half), F32),
        ],
        compiler_params=_params("arbitrary", "arbitrary", "arbitrary"),
        name="s5_scan",
    )(u_tm.reshape(n * b, d), bcat, ar_b, ai_b, ccat)

    return pl.pallas_call(
        functools.partial(_s5_glu_kernel, n_ctx),
        grid=(b, n // TILE),
        in_specs=[
            pl.BlockSpec((None, TILE, d), lambda i, t: (i, t, 0)),
            pl.BlockSpec((1, d), lambda i, t: (0, 0)),
            pl.BlockSpec((2, 8, d), lambda i, t: (i, 0, 0)),
            pl.BlockSpec((2, TILE, d), lambda i, t: (0, t, i)),
            pl.BlockSpec((1, d), lambda i, t: (0, 0)),
            pl.BlockSpec((d, 2 * d), lambda i, t: (0, 0)),
        ],
        out_specs=pl.BlockSpec((None, TILE, d), lambda i, t: (i, t, 0)),
        out_shape=jax.ShapeDtypeStruct(xs.shape, F32),
        input_output_aliases={0: 0},
        compiler_params=_params("arbitrary", "arbitrary"),
        name="s5_glu",
    )(xs, g.reshape(1, d), mods, y2.reshape(2, n, b * d), d_skip.reshape(1, d), w_glu.astype(BF16))


def _gla_in_kernel(n_ctx, x_ref, g_ref, m_ref, w_ref, w1_ref, o_ref, r_ref):
    is_ctx = _is_ctx(pl.program_id(1), x_ref.shape[0], n_ctx)
    hb = _norm_mod(x_ref[...], g_ref[...], m_ref, 0, is_ctx).astype(BF16)
    o_ref[...] = _dot(hb, w_ref[...]).astype(BF16)
    r_ref[...] = _dot(hb, w1_ref[...])


def _gla_scan_kernel(dk, dv, qf_ref, rf_ref, qr_ref, rr_ref, w2_ref, gb_ref, of_ref, or_ref, s_ref):
    c = GLA_CHUNK
    nck = qf_ref.shape[0] // c
    kd = GLA_HEADS * dk

    @pl.when(pl.program_id(1) == 0)
    def _():
        s_ref[...] = jnp.zeros_like(s_ref)

    row = lax.broadcasted_iota(jnp.int32, (c, c), 0)
    col = lax.broadcasted_iota(jnp.int32, (c, c), 1)
    masks = (row >= col, row <= col)

    refs = ((qf_ref, rf_ref, of_ref), (qr_ref, rr_ref, or_ref))
    chains = [(dd, h) for dd in range(2) for h in range(GLA_HEADS)]
    for i in range(nck):
        rows = [slice(i * c, (i + 1) * c), slice((nck - 1 - i) * c, (nck - i) * c)]
        decay = []
        for dd in range(2):
            rb = refs[dd][1][rows[dd], :].astype(BF16)
            lg = jax.nn.log_sigmoid(_dot(rb, w2_ref[dd]) + gb_ref[dd]) * (1.0 / GLA_GATE_NORM)
            bcum = jnp.dot(masks[dd].astype(F32), lg, preferred_element_type=F32,
                           precision=lax.Precision.HIGHEST)
            blast = jnp.sum(lg, axis=0, keepdims=True)
            decay.append((jnp.exp(bcum), jnp.exp(-bcum), jnp.exp(blast - bcum), jnp.exp(blast)))
        qg, att, kdec, vv, st = {}, {}, {}, {}, {}
        for ch in chains:
            dd, h = ch
            x_ref = refs[dd][0]
            hs = slice(h * dk, (h + 1) * dk)
            q = x_ref[rows[dd], hs].astype(F32) * (float(dk) ** -0.5)
            k = x_ref[rows[dd], kd + h * dk:kd + (h + 1) * dk].astype(F32)
            vv[ch] = x_ref[rows[dd], 2 * kd + h * dv:2 * kd + (h + 1) * dv]
            qg[ch] = (q * decay[dd][0][:, hs]).astype(BF16)
            kg = (k * decay[dd][1][:, hs]).astype(BF16)
            kdec[ch] = k * decay[dd][2][:, hs]
            att[ch] = jnp.where(masks[dd], _dot_nt(qg[ch], kg), 0.0).astype(BF16)
        for ch in chains:
            dd, h = ch
            st[ch] = s_ref[dd, h]
            refs[dd][2][rows[dd], h * dv:(h + 1) * dv] = (
                _dot(att[ch], vv[ch]) + _dot(qg[ch], st[ch].astype(BF16)))
        for ch in chains:
            dd, h = ch
            eb = jnp.broadcast_to(decay[dd][3][:, h * dk:(h + 1) * dk], (dk - c, dk))
            mt = jnp.concatenate([kdec[ch], eb], axis=0).T
            s_ref[dd, h] = mt[:, c:c + 1] * st[ch] + _dot(mt[:, 0:c].astype(BF16), vv[ch])


def _gla_out_kernel(n_ctx, x_ref, of_ref, or_ref, gt_ref, on_ref, w_ref, m_ref, o_ref):
    is_ctx = _is_ctx(pl.program_id(1), x_ref.shape[0], n_ctx)
    o = of_ref[...] + or_ref[...]
    gt = gt_ref[...].astype(F32)
    dv = on_ref.shape[-1]
    parts = []
    for h in range(GLA_HEADS):
        sl = slice(h * dv, (h + 1) * dv)
        parts.append(_rms(o[:, sl], on_ref[...]) * (gt[:, sl] * jax.nn.sigmoid(gt[:, sl])))
    y = jnp.concatenate(parts, axis=-1).astype(BF16)
    o_ref[...] = x_ref[...] + _mod_row(m_ref, 2, is_ctx) * _dot(y, w_ref[...])


def mixer_gla(xs, mods, g, n_ctx, w_in, gk_w1, gk_w2, gk_b, o_norm, w_o):
    b, n, d = xs.shape
    kd = d // 2
    vd = d
    dk = kd // GLA_HEADS
    dv = vd // GLA_HEADS
    rk = GLA_GATE_RANK
    w1 = jnp.zeros((d, 128), F32).at[:, :rk].set(gk_w1[0]).at[:, rk:2 * rk].set(gk_w1[1]).astype(BF16)
    w2 = jnp.zeros((2, 128, kd), F32)
    for dd in range(2):
        w2 = w2.at[dd, dd * rk:(dd + 1) * rk, :].set(gk_w2[dd])
    w2 = w2.astype(BF16)
    gbias = gk_b.reshape(2, 1, kd)
    nw = w_in.shape[-1]

    qkvg, r = pl.pallas_call(
        functools.partial(_gla_in_kernel, n_ctx),
        grid=(b, n // TILE),
        in_specs=[
            pl.BlockSpec((None, TILE, d), lambda i, t: (i, t, 0)),
            pl.BlockSpec((1, d), lambda i, t: (0, 0)),
            pl.BlockSpec((2, 8, d), lambda i, t: (i, 0, 0)),
            pl.BlockSpec((d, nw), lambda i, t: (0, 0)),
            pl.BlockSpec((d, 128), lambda i, t: (0, 0)),
        ],
        out_specs=[
            pl.BlockSpec((None, TILE, nw), lambda i, t: (i, t, 0)),
            pl.BlockSpec((None, TILE, 128), lambda i, t: (i, t, 0)),
        ],
        out_shape=[
            jax.ShapeDtypeStruct((b, n, nw), BF16),
            jax.ShapeDtypeStruct((b, n, 128), F32),
        ],
        compiler_params=_params("arbitrary", "arbitrary"),
        name="gla_in",
    )(xs, g.reshape(1, d), mods, w_in.astype(BF16), w1)

    nt = n // TILE
    nct = n_ctx // TILE

    def rev_tile(s):
        return jnp.where(s < nct, nct - 1 - s, nt - 1 - (s - nct))

    qkv_w = 2 * kd + vd
    o_fwd, o_rev = pl.pallas_call(
        functools.partial(_gla_scan_kernel, dk, dv),
        grid=(b, nt),
        in_specs=[
            pl.BlockSpec((None, TILE, qkv_w), lambda i, s: (i, s, 0)),
            pl.BlockSpec((None, TILE, 128), lambda i, s: (i, s, 0)),
            pl.BlockSpec((None, TILE, qkv_w), lambda i, s: (i, rev_tile(s), 0)),
            pl.BlockSpec((None, TILE, 128), lambda i, s: (i, rev_tile(s), 0)),
            pl.BlockSpec(w2.shape, lambda i, s: (0, 0, 0)),
            pl.BlockSpec(gbias.shape, lambda i, s: (0, 0, 0)),
        ],
        out_specs=[
            pl.BlockSpec((None, TILE, vd), lambda i, s: (i, s, 0)),
            pl.BlockSpec((None, TILE, vd), lambda i, s: (i, rev_tile(s), 0)),
        ],
        out_shape=[jax.ShapeDtypeStruct((b, n, vd), F32), jax.ShapeDtypeStruct((b, n, vd), F32)],
        scratch_shapes=[pltpu.VMEM((2, GLA_HEADS, dk, dv), F32)],
        compiler_params=_params("arbitrary", "arbitrary"),
        name="gla_scan",
    )(qkvg, r, qkvg, r, w2, gbias)

    return pl.pallas_call(
        functools.partial(_gla_out_kernel, n_ctx),
        grid=(b, nt),
        in_specs=[
            pl.BlockSpec((None, TILE, d), lambda i, t: (i, t, 0)),
            pl.BlockSpec((None, TILE, vd), lambda i, t: (i, t, 0)),
            pl.BlockSpec((None, TILE, vd), lambda i, t: (i, t, 0)),
            pl.BlockSpec((None, TILE, vd), lambda i, t: (i, t, (2 * kd + vd) // vd)),
            pl.BlockSpec((1, dv), lambda i, t: (0, 0)),
            pl.BlockSpec((vd, d), lambda i, t: (0, 0)),
            pl.BlockSpec((2, 8, d), lambda i, t: (i, 0, 0)),
        ],
        out_specs=pl.BlockSpec((None, TILE, d), lambda i, t: (i, t, 0)),
        out_shape=jax.ShapeDtypeStruct(xs.shape, F32),
        input_output_aliases={0: 0},
        compiler_params=_params("arbitrary", "arbitrary"),
        name="gla_out",
    )(xs, o_fwd, o_rev, qkvg, o_norm.reshape(1, dv), w_o.astype(BF16), mods)


PEER_TT = 768
PEER_EB = 2048
PEER_SUB = 1024
PEER_RANKS = PEER_TOPK + 1


SUBLANES = 8


def _sorting_network(n):
    pairs = []
    p = 1
    while p < n:
        k = p
        while k >= 1:
            for j in range(k % p, n - k, 2 * k):
                for i in range(min(k, n - j - k)):
                    if (i + j) // (2 * p) == (i + j + k) // (2 * p):
                        pairs.append((i + j, i + j + k))
            k //= 2
        p *= 2
    return pairs


def _sublane_max_all(x):
    for shift in (4, 2, 1):
        x = jnp.maximum(x, pltpu.roll(x, shift, 0))
    return x


def _pop_top(lists, k):
    neg = jnp.full(lists[0].shape, -jnp.inf, F32)
    out = []
    for r in range(k):
        m = _sublane_max_all(lists[0])
        out.append(m)
        if r == k - 1:
            break
        hit = lists[0] == m
        keep = min(len(lists), k - 1 - r)
        lists = [jnp.where(hit, lists[p + 1] if p + 1 < len(lists) else neg, lists[p]) for p in range(keep)]
    return out


def _top_rows(s, k):
    tiles = [s[SUBLANES * v:SUBLANES * (v + 1), :] for v in range(s.shape[0] // SUBLANES)]
    for i, j in _sorting_network(len(tiles)):
        tiles[i], tiles[j] = jnp.maximum(tiles[i], tiles[j]), jnp.minimum(tiles[i], tiles[j])
    return _pop_top(tiles, k)


def _pair_threshold(a, b):
    k = PEER_RANKS
    lens = (k, k // 2, k // 3, k // 4, k - 4, k // 2 - 4, k // 3 - 4, 0)
    sub = lax.broadcasted_iota(jnp.int32, a[0].shape, 0)
    length = jnp.zeros(a[0].shape, jnp.int32)
    for c, ln in enumerate(lens):
        length = jnp.where(sub == c, ln, length)
    a_fix = jnp.where(sub == 0, a[0], jnp.where(sub == 1, a[1], jnp.where(sub == 2, a[2], a[3])))
    b_fix = jnp.where(sub == 4, b[0], jnp.where(sub == 5, b[1], b[2]))
    lists = []
    for p in range(k):
        a_p = jnp.where(sub < 4, a_fix, jnp.where(sub < 6, a[min(4 + p, k - 1)], a[4]))
        b_p = jnp.where(sub < 4, b[p], b_fix)
        lists.append(jnp.where(length > p, a_p + b_p, -jnp.inf))
    return _pop_top(lists, k)


def _peer_score_kernel(n_ctx, x_ref, g_ref, m_ref, wqt_ref, keys_ref, h_ref, e0_ref, e1_ref, th_ref, sc_ref):
    tt = x_ref.shape[0]
    is_ctx = _is_ctx(pl.program_id(1), tt, n_ctx)
    ht = _norm_mod(x_ref[...], g_ref[...], m_ref, 3, is_ctx).T.astype(BF16)
    h_ref[...] = ht
    qt = _dot(wqt_ref[...], ht).astype(BF16)
    dkey = keys_ref.shape[-1]
    for hp in range(2 * PEER_HEADS):
        sc_ref[hp] = _dot(keys_ref[hp], qt[hp * dkey:(hp + 1) * dkey, :])
    head_row = lax.broadcasted_iota(jnp.int32, th_ref.shape, 0)

    def head(hd, th_acc):
        th_parts = []
        for lt in range(tt // LANES):
            ls = slice(lt * LANES, (lt + 1) * LANES)
            s0 = sc_ref[2 * hd, :, ls]
            s1 = sc_ref[2 * hd + 1, :, ls]
            a = _top_rows(s0, PEER_RANKS)
            b = _top_rows(s1, PEER_RANKS)
            v = _pair_threshold(a, b)
            tau = 0.5 * (v[PEER_TOPK - 1] + v[PEER_TOPK])
            z = jnp.ones_like(v[0])
            for kk in range(1, PEER_TOPK):
                z = z + jnp.exp(v[kk] - v[0])
            rz = (1.0 / z)[0:1, :]
            e0_ref[hd, :, ls] = jnp.exp(s0 - a[0][0:1, :]) * rz
            e1_ref[hd, :, ls] = jnp.exp(s1 - b[0][0:1, :])
            th_parts.append(jnp.exp(tau - v[0])[0:1, :] * rz)
        return jnp.where(head_row == hd, jnp.concatenate(th_parts, axis=1), th_acc)

    th_ref[...] = lax.fori_loop(0, PEER_HEADS, head, jnp.zeros(th_ref.shape, F32))


def _gelu_tanh_sigmoid_form(x):
    c0 = -2.0 * 0.7978845608028654 * 1.4426950408889634
    c1 = c0 * 0.044715
    return x / (1.0 + jnp.exp2(x * (c0 + c1 * (x * x))))


def _peer_dense_kernel(n_ctx, sub, ht_ref, e0_ref, e1_ref, th_ref, u_ref, vt_ref, x_ref, m_ref, o_ref,
                       w_ref, acc_ref):
    e = pl.program_id(2)
    eb, tt = w_ref.shape
    nk = e1_ref.shape[1]

    @pl.when(e == 0)
    def _():
        acc_ref[...] = jnp.zeros_like(acc_ref)

    ht = ht_ref[...]
    for sb in range(eb // sub):
        a_sb = _dot(u_ref[sb * sub:(sb + 1) * sub, :], ht)
        for ii in range(sub // nk):
            i = sb * (sub // nk) + ii
            for lt in range(tt // LANES):
                ls = slice(lt * LANES, (lt + 1) * LANES)
                gsum = jnp.zeros((nk, LANES), F32)
                for hd in range(PEER_HEADS):
                    p = e0_ref[hd, i:i + 1, ls] * e1_ref[hd, :, ls]
                    gsum = gsum + jnp.where(p > th_ref[hd:hd + 1, ls], p, 0.0)
                act = _gelu_tanh_sigmoid_form(a_sb[ii * nk:(ii + 1) * nk, ls])
                w_ref[i * nk:(i + 1) * nk, ls] = (act * gsum).astype(BF16)
    acc_ref[...] += _dot(vt_ref[...], w_ref[...])

    @pl.when(e == pl.num_programs(2) - 1)
    def _():
        is_ctx = _is_ctx(pl.program_id(1), tt, n_ctx)
        o_ref[...] = x_ref[...] + _mod_row(m_ref, 5, is_ctx) * acc_ref[...].T


def peer_layer(xs, mods, g, n_ctx, w_q, keys, u_tab, v_tab, eb=PEER_EB, sub=PEER_SUB):
    b, n, d = xs.shape
    tt = PEER_TT
    ntt = n // tt
    ne = u_tab.shape[0]
    nk = PEER_NKEYS
    hq = w_q.shape[-1]
    wqt = jnp.transpose(w_q).astype(BF16)
    keys_b = keys.reshape(PEER_HEADS * 2, nk, keys.shape[-1]).astype(BF16)
    u_b = u_tab.astype(BF16)

    h2t, e0, e1, th = pl.pallas_call(
        functools.partial(_peer_score_kernel, n_ctx),
        grid=(b, ntt),
        in_specs=[
            pl.BlockSpec((None, tt, d), lambda i, t: (i, t, 0)),
            pl.BlockSpec((1, d), lambda i, t: (0, 0)),
            pl.BlockSpec((2, 8, d), lambda i, t: (i, 0, 0)),
            pl.BlockSpec((hq, d), lambda i, t: (0, 0)),
            pl.BlockSpec(keys_b.shape, lambda i, t: (0, 0, 0)),
        ],
        out_specs=[
            pl.BlockSpec((d, tt), lambda i, t: (0, i * ntt + t)),
            pl.BlockSpec((PEER_HEADS, nk, tt), lambda i, t: (0, 0, i * ntt + t)),
            pl.BlockSpec((PEER_HEADS, nk, tt), lambda i, t: (0, 0, i * ntt + t)),
            pl.BlockSpec((PEER_HEADS, tt), lambda i, t: (0, i * ntt + t)),
        ],
        out_shape=[
            jax.ShapeDtypeStruct((d, b * n), BF16),
            jax.ShapeDtypeStruct((PEER_HEADS, nk, b * n), F32),
            jax.ShapeDtypeStruct((PEER_HEADS, nk, b * n), F32),
            jax.ShapeDtypeStruct((PEER_HEADS, b * n), F32),
        ],
        scratch_shapes=[pltpu.VMEM((2 * PEER_HEADS, nk, tt), F32)],
        compiler_params=_params("arbitrary", "arbitrary"),
        name="peer_score",
    )(xs, g.reshape(1, d), mods, wqt, keys_b)

    nblk = ne // eb
    vt_b = jnp.transpose(v_tab.reshape(nblk, eb, d), (0, 2, 1)).astype(BF16)
    return pl.pallas_call(
        functools.partial(_peer_dense_kernel, n_ctx, sub),
        grid=(b, ntt, nblk),
        in_specs=[
            pl.BlockSpec((d, tt), lambda i, t, e: (0, i * ntt + t)),
            pl.BlockSpec((PEER_HEADS, eb // nk, tt), lambda i, t, e: (0, e, i * ntt + t)),
            pl.BlockSpec((PEER_HEADS, nk, tt), lambda i, t, e: (0, 0, i * ntt + t)),
            pl.BlockSpec((PEER_HEADS, tt), lambda i, t, e: (0, i * ntt + t)),
            pl.BlockSpec((eb, d), lambda i, t, e: (e, 0)),
            pl.BlockSpec((None, d, eb), lambda i, t, e: (e, 0, 0)),
            pl.BlockSpec((None, tt, d), lambda i, t, e: (i, t, 0)),
            pl.BlockSpec((2, 8, d), lambda i, t, e: (i, 0, 0)),
        ],
        out_specs=pl.BlockSpec((None, tt, d), lambda i, t, e: (i, t, 0)),
        out_shape=jax.ShapeDtypeStruct(xs.shape, F32),
        scratch_shapes=[
            pltpu.VMEM((eb, tt), BF16),
            pltpu.VMEM((d, tt), F32),
        ],
        input_output_aliases={6: 0},
        compiler_params=_params("arbitrary", "arbitrary", "arbitrary"),
        name="peer_dense",
    )(h2t, e0, e1, th, u_b, vt_b, xs, mods)


def _final_kernel(x_ref, g_ref, o_ref):
    o_ref[...] = _rms(x_ref[...], g_ref[...])


def final_norm(xs, g, n_ctx):
    b, n, d = xs.shape
    off = n_ctx // TILE
    return pl.pallas_call(
        _final_kernel,
        grid=(b, (n - n_ctx) // TILE),
        in_specs=[
            pl.BlockSpec((None, TILE, d), lambda i, t: (i, t + off, 0)),
            pl.BlockSpec((1, d), lambda i, t: (0, 0)),
        ],
        out_specs=pl.BlockSpec((None, TILE, d), lambda i, t: (i, t, 0)),
        out_shape=jax.ShapeDtypeStruct((b, n - n_ctx, d), F32),
        compiler_params=_params("arbitrary", "arbitrary"),
        name="final_norm",
    )(xs, g.reshape(1, d))


def kernel(x, c, ctx, c_ctx, norm_g, ada_w, ada_b, mla_w_in, mla_q_norm, mla_w_uq, mla_kv_norm, mla_w_ukv, mla_w_o, s5_lam_re, s5_lam_im, s5_b_re, s5_b_im, s5_c_re, s5_c_im, s5_log_step, s5_d, s5_w_glu, gla_w_in, gla_gk_w1, gla_gk_w2, gla_gk_b, gla_o_norm, gla_w_o, peer_w_q, peer_keys, peer_u, peer_v, final_g):
    b, n_lat, d = x.shape
    n_ctx = ctx.shape[1]
    depth = ada_w.shape[0]
    xs = jnp.concatenate([ctx, x], axis=1)

    r = 8 * ((b + 1 + 7) // 8)
    cond = jnp.zeros((r, d), F32).at[:b].set(c).at[b].set(c_ctx)
    ada = ada_all(cond, ada_w, ada_b).reshape(depth, r, 6, d)
    lat = ada[:, :b]
    ctxp = jnp.broadcast_to(ada[:, b:b + 1], lat.shape)
    mods_all = jnp.stack([ctxp, lat], axis=2)
    mods_all = jnp.pad(mods_all, ((0, 0), (0, 0), (0, 0), (0, 2), (0, 0))).reshape(depth, b * 2, 8, d)

    for i in range(depth):
        kind, j = i % N_MIXERS, i // N_MIXERS
        mods = mods_all[i]
        if kind == 0:
            xs = mixer_mla(xs, mods, norm_g[i, 0], n_ctx, i < depth - 1, mla_w_in[j], mla_q_norm[j], mla_w_uq[j],
                           mla_kv_norm[j], mla_w_ukv[j], mla_w_o[j])
        elif kind == 1:
            xs = mixer_s5(xs, mods, norm_g[i, 0], n_ctx, s5_lam_re[j], s5_lam_im[j], s5_b_re[j], s5_b_im[j],
                          s5_c_re[j], s5_c_im[j], s5_log_step[j], s5_d[j], s5_w_glu[j])
        else:
            xs = mixer_gla(xs, mods, norm_g[i, 0], n_ctx, gla_w_in[j], gla_gk_w1[j], gla_gk_w2[j],
                           gla_gk_b[j], gla_o_norm[j], gla_w_o[j])
        xs = peer_layer(xs, mods, norm_g[i, 1], n_ctx, peer_w_q[i], peer_keys[i], peer_u[i], peer_v[i])
    return final_norm(xs, final_g, n_ctx)
```

```python
import functools
import math

import numpy as np
import jax
import jax.numpy as jnp
from jax import lax
from jax.experimental import pallas as pl
from jax.experimental.pallas import tpu as pltpu

F32 = jnp.float32
BF16 = jnp.bfloat16

EPS = 1e-6
GRID_W = 64
N_MIXERS = 3

MLA_HEADS = 16
MLA_NOPE = 64
MLA_ROPE = 32
MLA_V = 64
MLA_Q_LORA = 384
MLA_KV_LORA = 256
ROPE_BASE = 10000.0
MLA_HEAD_PAD = 128
MLA_HEADS_PER_STEP = 2
MLA_Q_ROWS = 1024

S5_GROUP = 16
S5_STATE = 64
S5_GROUPS_PER_BLOCK = 8
S5_CHUNK = 128

GLA_HEADS = 4
GLA_GATE_RANK = 16
GLA_GATE_NORM = 16.0
GLA_CHUNK = 64

PEER_HEADS = 8
PEER_NKEYS = 128
PEER_TOPK = 16

LANES = 128
TILE = 256
VMEM_LIMIT = 56 * 1024 * 1024


def _params(*sem):
    return pltpu.CompilerParams(dimension_semantics=sem, vmem_limit_bytes=VMEM_LIMIT)


def _gelu_tanh(x):
    return 0.5 * x * (1.0 + jnp.tanh(0.7978845608028654 * (x + 0.044715 * x * x * x)))


def _is_ctx(tile_idx, rows, n_ctx):
    pos = tile_idx * rows + lax.broadcasted_iota(jnp.int32, (rows, 1), 0)
    return pos < n_ctx


def _mod_row(m_ref, row, is_ctx):
    return jnp.where(is_ctx, m_ref[0, row:row + 1, :], m_ref[1, row:row + 1, :])


def _rms(x, g):
    return x * lax.rsqrt(jnp.mean(x * x, axis=-1, keepdims=True) + EPS) * g


def _norm_mod(x, g, m_ref, row0, is_ctx):
    return _rms(x, g) * (1.0 + _mod_row(m_ref, row0 + 1, is_ctx)) + _mod_row(m_ref, row0, is_ctx)


def _dot(a, b):
    return jnp.dot(a, b, preferred_element_type=F32)


def _dot_nt(a, b):
    return lax.dot_general(a, b, (((1,), (1,)), ((), ())), preferred_element_type=F32)


def _ada_kernel(c_ref, w_ref, b_ref, o_ref):
    c = c_ref[...]
    s = (c * jax.nn.sigmoid(c)).astype(BF16)
    o_ref[...] = _dot(s, w_ref[...].astype(BF16)) + b_ref[...]


def ada_all(cond, ada_w, ada_b):
    depth, d, n6 = ada_w.shape
    r = cond.shape[0]
    tn = 1024
    return pl.pallas_call(
        _ada_kernel,
        grid=(depth, n6 // tn),
        in_specs=[
            pl.BlockSpec((r, d), lambda l, j: (0, 0)),
            pl.BlockSpec((None, d, tn), lambda l, j: (l, 0, j)),
            pl.BlockSpec((None, 1, tn), lambda l, j: (l, 0, j)),
        ],
        out_specs=pl.BlockSpec((None, r, tn), lambda l, j: (l, 0, j)),
        out_shape=jax.ShapeDtypeStruct((depth, r, n6), F32),
        compiler_params=_params("arbitrary", "arbitrary"),
        name="ada",
    )(cond, ada_w, ada_b.reshape(depth, 1, n6))


def _proj_residual_kernel(n_ctx, gate_row, tile_off, x_ref, y_ref, w_ref, m_ref, o_ref):
    is_ctx = _is_ctx(pl.program_id(1) + tile_off, x_ref.shape[0], n_ctx)
    f = _dot(y_ref[...], w_ref[...])
    o_ref[...] = x_ref[...] + _mod_row(m_ref, gate_row, is_ctx) * f


def proj_residual(xs, y, w, mods, n_ctx, gate_row, tile_off=0):
    b, n, d = xs.shape
    k = y.shape[-1]
    return pl.pallas_call(
        functools.partial(_proj_residual_kernel, n_ctx, gate_row, tile_off),
        grid=(b, y.shape[1] // TILE),
        in_specs=[
            pl.BlockSpec((None, TILE, d), lambda i, t: (i, t + tile_off, 0)),
            pl.BlockSpec((None, TILE, k), lambda i, t: (i, t, 0)),
            pl.BlockSpec((k, d), lambda i, t: (0, 0)),
            pl.BlockSpec((2, 8, d), lambda i, t: (i, 0, 0)),
        ],
        out_specs=pl.BlockSpec((None, TILE, d), lambda i, t: (i, t + tile_off, 0)),
        out_shape=jax.ShapeDtypeStruct(xs.shape, F32),
        input_output_aliases={0: 0},
        compiler_params=_params("arbitrary", "arbitrary"),
        name="proj_residual",
    )(xs, y, w, mods)


def _rope_tables(n_ctx, n_lat, scale):
    half = MLA_ROPE // 2
    rows_n = n_lat // GRID_W
    rows = np.repeat(np.arange(rows_n, dtype=np.float32), GRID_W)
    cols = np.tile(np.arange(GRID_W, dtype=np.float32), rows_n)
    inv = (ROPE_BASE ** (-np.arange(0, half, 2, dtype=np.float32) / half)).astype(np.float32)
    ang_r = rows[:, None] * inv
    ang_c = cols[:, None] * inv
    cos = np.concatenate([np.cos(ang_r), np.cos(ang_r), np.cos(ang_c), np.cos(ang_c)], axis=1)
    sin = np.concatenate([-np.sin(ang_r), np.sin(ang_r), -np.sin(ang_c), np.sin(ang_c)], axis=1)
    n = n_ctx + n_lat
    a = np.zeros((n, MLA_HEAD_PAD), np.float32)
    b = np.zeros((n, MLA_HEAD_PAD), np.float32)
    a[:, :MLA_NOPE] = 1.0
    a[:n_ctx, MLA_NOPE:MLA_NOPE + MLA_ROPE] = 1.0
    a[n_ctx:, MLA_NOPE:MLA_NOPE + MLA_ROPE] = cos
    b[n_ctx:, MLA_NOPE:MLA_NOPE + MLA_ROPE] = sin
    return jnp.asarray(a * scale), jnp.asarray(b * scale)


def _rope_swap_index():
    q = MLA_ROPE // 4
    base = np.arange(MLA_ROPE)
    return np.where((base % (2 * q)) < q, base + q, base - q)


def _mla_in_kernel(n_ctx, x_ref, g_ref, m_ref, wq_ref, wkv_ref, wkp_ref, qn_ref, kvn_ref,
                   wuq1_ref, wuq2_ref, wuk_ref, wuv_ref, aq_ref, bq_ref, ak_ref, bk_ref,
                   qc_ref, ql_ref, k_ref, v_ref):
    is_ctx = _is_ctx(pl.program_id(1), x_ref.shape[0], n_ctx)
    hb = _norm_mod(x_ref[...], g_ref[...], m_ref, 0, is_ctx).astype(BF16)
    cq = _rms(_dot(hb, wq_ref[...]), qn_ref[...]).astype(BF16)
    ckv = _rms(_dot(hb, wkv_ref[...]), kvn_ref[...]).astype(BF16)
    kp = _dot(hb, wkp_ref[...])
    kpe = kp[:, :MLA_HEAD_PAD] * ak_ref[...] + kp[:, MLA_HEAD_PAD:] * bk_ref[...]
    y1 = _dot(cq, wuq1_ref[...])
    y2 = _dot(cq, wuq2_ref[...])
    kk = _dot(ckv, wuk_ref[...])
    aq = aq_ref[...]
    bq = bq_ref[...]
    qs = []
    for h in range(MLA_HEADS):
        sl = slice(h * MLA_HEAD_PAD, (h + 1) * MLA_HEAD_PAD)
        qs.append((y1[:, sl] * aq + y2[:, sl] * bq).astype(BF16))
        k_ref[:, sl] = (kk[:, sl] + kpe).astype(BF16)
    v_ref[...] = _dot(ckv, wuv_ref[...]).astype(BF16)
    in_ctx = pl.program_id(1) * x_ref.shape[0] < n_ctx

    @pl.when(in_ctx)
    def _():
        for h in range(MLA_HEADS):
            qc_ref[:, h * MLA_HEAD_PAD:(h + 1) * MLA_HEAD_PAD] = qs[h]

    @pl.when(jnp.logical_not(in_ctx))
    def _():
        for h in range(MLA_HEADS):
            ql_ref[:, h * MLA_HEAD_PAD:(h + 1) * MLA_HEAD_PAD] = qs[h]


def _mla_attn_kernel(q_ref, k_ref, v_ref, o_ref):
    heads = range(q_ref.shape[-1] // MLA_HEAD_PAD)
    ss = [_dot_nt(q_ref[:, h * MLA_HEAD_PAD:(h + 1) * MLA_HEAD_PAD],
                  k_ref[:, h * MLA_HEAD_PAD:(h + 1) * MLA_HEAD_PAD]) for h in heads]
    ps = [jnp.exp(s - jnp.max(s, axis=-1, keepdims=True)) for s in ss]
    ls = [jnp.sum(p, axis=-1, keepdims=True) for p in ps]
    outs = [_dot(ps[h].astype(BF16), v_ref[:, (h // 2) * 2 * MLA_V:(h // 2 + 1) * 2 * MLA_V]) / ls[h]
            for h in heads]
    lane = lax.broadcasted_iota(jnp.int32, outs[0].shape, 1)
    for pr in range(len(heads) // 2):
        o_ref[:, pr * 2 * MLA_V:(pr + 1) * 2 * MLA_V] = jnp.where(
            lane < MLA_V, outs[2 * pr], outs[2 * pr + 1]).astype(BF16)


def _mla_attention(q, k, v, n_keys, q_rows, hs):
    b, nq, _ = q.shape
    hp = MLA_HEAD_PAD
    return pl.pallas_call(
        _mla_attn_kernel,
        grid=(b, MLA_HEADS // hs, nq // q_rows),
        in_specs=[
            pl.BlockSpec((None, q_rows, hs * hp), lambda i, h, t: (i, t, h)),
            pl.BlockSpec((None, n_keys, hs * hp), lambda i, h, t: (i, 0, h)),
            pl.BlockSpec((None, n_keys, hs * MLA_V), lambda i, h, t: (i, 0, h)),
        ],
        out_specs=pl.BlockSpec((None, q_rows, hs * MLA_V), lambda i, h, t: (i, t, h)),
        out_shape=jax.ShapeDtypeStruct((b, nq, MLA_HEADS * MLA_V), BF16),
        compiler_params=_params("arbitrary", "arbitrary", "arbitrary"),
        name="mla_attn",
    )(q, k, v)


def mixer_mla(xs, mods, g, n_ctx, ctx_out, w_in, q_norm, w_uq, kv_norm, w_ukv, w_o):
    b, n, d = xs.shape
    hp = MLA_HEAD_PAD
    dq = MLA_NOPE + MLA_ROPE
    wq = w_in[:, :MLA_Q_LORA].astype(BF16)
    wkv = w_in[:, MLA_Q_LORA:MLA_Q_LORA + MLA_KV_LORA].astype(BF16)
    w_pe = w_in[:, MLA_Q_LORA + MLA_KV_LORA:]
    swap = _rope_swap_index()
    wkp = jnp.zeros((d, 2 * hp), F32)
    wkp = wkp.at[:, MLA_NOPE:dq].set(w_pe).at[:, hp + MLA_NOPE:hp + dq].set(w_pe[:, swap]).astype(BF16)
    uq = w_uq.reshape(MLA_Q_LORA, MLA_HEADS, dq)
    z = jnp.zeros((MLA_Q_LORA, MLA_HEADS, hp - dq), F32)
    wuq1 = jnp.concatenate([uq, z], axis=-1).reshape(MLA_Q_LORA, MLA_HEADS * hp).astype(BF16)
    zn = jnp.zeros((MLA_Q_LORA, MLA_HEADS, MLA_NOPE), F32)
    wuq2 = jnp.concatenate([zn, uq[:, :, MLA_NOPE:][:, :, swap], z], axis=-1)
    wuq2 = wuq2.reshape(MLA_Q_LORA, MLA_HEADS * hp).astype(BF16)
    ukv = w_ukv.reshape(MLA_KV_LORA, MLA_HEADS, MLA_NOPE + MLA_V)
    zk = jnp.zeros((MLA_KV_LORA, MLA_HEADS, hp - MLA_NOPE), F32)
    wuk = jnp.concatenate([ukv[:, :, :MLA_NOPE], zk], axis=-1).reshape(MLA_KV_LORA, MLA_HEADS * hp).astype(BF16)
    wuv = ukv[:, :, MLA_NOPE:].reshape(MLA_KV_LORA, MLA_HEADS * MLA_V).astype(BF16)
    aq, bq = _rope_tables(n_ctx, n - n_ctx, float(dq) ** -0.5)
    ak, bk = _rope_tables(n_ctx, n - n_ctx, 1.0)

    full = lambda shape: pl.BlockSpec(shape, lambda i, t: tuple(0 for _ in shape))
    tab = pl.BlockSpec((TILE, hp), lambda i, t: (t, 0))
    nct = n_ctx // TILE
    q_ctx, q_lat, k, v = pl.pallas_call(
        functools.partial(_mla_in_kernel, n_ctx),
        grid=(b, n // TILE),
        in_specs=[
            pl.BlockSpec((None, TILE, d), lambda i, t: (i, t, 0)),
            full((1, d)),
            pl.BlockSpec((2, 8, d), lambda i, t: (i, 0, 0)),
            full(wq.shape), full(wkv.shape), full(wkp.shape),
            full((1, MLA_Q_LORA)), full((1, MLA_KV_LORA)),
            full(wuq1.shape), full(wuq2.shape), full(wuk.shape), full(wuv.shape),
            tab, tab, tab, tab,
        ],
        out_specs=[
            pl.BlockSpec((None, TILE, MLA_HEADS * hp), lambda i, t: (i, jnp.minimum(t, nct - 1), 0)),
            pl.BlockSpec((None, TILE, MLA_HEADS * hp), lambda i, t: (i, jnp.maximum(t - nct, 0), 0)),
            pl.BlockSpec((None, TILE, MLA_HEADS * hp), lambda i, t: (i, t, 0)),
            pl.BlockSpec((None, TILE, MLA_HEADS * MLA_V), lambda i, t: (i, t, 0)),
        ],
        out_shape=[
            jax.ShapeDtypeStruct((b, n_ctx, MLA_HEADS * hp), BF16),
            jax.ShapeDtypeStruct((b, n - n_ctx, MLA_HEADS * hp), BF16),
            jax.ShapeDtypeStruct((b, n, MLA_HEADS * hp), BF16),
            jax.ShapeDtypeStruct((b, n, MLA_HEADS * MLA_V), BF16),
        ],
        compiler_params=_params("arbitrary", "arbitrary"),
        name="mla_in",
    )(xs, g.reshape(1, d), mods, wq, wkv, wkp, q_norm.reshape(1, -1), kv_norm.reshape(1, -1),
      wuq1, wuq2, wuk, wuv, aq, bq, ak, bk)

    wo = w_o.astype(BF16)
    o_lat = _mla_attention(q_lat, k, v, n, min(MLA_Q_ROWS, n - n_ctx), MLA_HEADS_PER_STEP)
    xs = proj_residual(xs, o_lat, wo, mods, n_ctx, 2, tile_off=nct)
    if ctx_out:
        o_ctx = _mla_attention(q_ctx, k, v, n_ctx, n_ctx, MLA_HEADS_PER_STEP)
        xs = proj_residual(xs, o_ctx, wo, mods, n_ctx, 2, tile_off=0)
    return xs


def _s5_disc_kernel(lr_ref, li_ref, ls_ref, bre_ref, bim_ref, ar_ref, ai_ref, br_ref, bi_ref):
    lr = lr_ref[...]
    li = li_ref[...]
    dt = jnp.exp(ls_ref[...])
    mag = jnp.exp(lr * dt)
    ar = mag * jnp.cos(li * dt)
    ai = mag * jnp.sin(li * dt)
    den = lr * lr + li * li
    fr = ((ar - 1.0) * lr + ai * li) / den
    fi = (ai * lr - (ar - 1.0) * li) / den
    ar_ref[...] = ar
    ai_ref[...] = ai
    for c in range(S5_GROUP):
        br_ref[c] = fr * bre_ref[c] - fi * bim_ref[c]
        bi_ref[c] = fr * bim_ref[c] + fi * bre_ref[c]


def _s5_in_kernel(n_ctx, x_ref, g_ref, m_ref, u_ref):
    is_ctx = _is_ctx(pl.program_id(1), x_ref.shape[0], n_ctx)
    u_ref[...] = _norm_mod(x_ref[...], g_ref[...], m_ref, 0, is_ctx).astype(BF16)


def _s5_scan_kernel(nb, u_ref, bcat_ref, ar_ref, ai_ref, ccat_ref, y_ref, bu_ref, xs_ref, st_ref):
    d = pl.program_id(0)
    half = ar_ref.shape[-1]

    @pl.when(pl.program_id(2) == 0)
    def _():
        st_ref[...] = jnp.zeros_like(st_ref)

    bu_ref[...] = _dot(u_ref[...], bcat_ref[...])
    ar = jnp.broadcast_to(ar_ref[...], (nb, half))
    ai = jnp.broadcast_to(ai_ref[...], (nb, half))
    steps = u_ref.shape[0] // nb

    def step(i, carry):
        xr, xi = carry
        tt = jnp.where(d == 0, i, steps - 1 - i)
        r0 = pl.multiple_of(tt * nb, nb)
        nxr = ar * xr - ai * xi + bu_ref[pl.ds(r0, nb), 0:half]
        nxi = ar * xi + ai * xr + bu_ref[pl.ds(r0, nb), half:2 * half]
        xs_ref[pl.ds(r0, nb), 0:half] = nxr.astype(BF16)
        xs_ref[pl.ds(r0, nb), half:2 * half] = nxi.astype(BF16)
        return nxr, nxi

    xr, xi = lax.fori_loop(0, steps, step, (st_ref[:, 0:half], st_ref[:, half:2 * half]), unroll=4)
    st_ref[:, 0:half] = xr
    st_ref[:, half:2 * half] = xi
    y_ref[...] = _dot(xs_ref[...], ccat_ref[...])


def _s5_glu_kernel(n_ctx, x_ref, g_ref, m_ref, y_ref, dsk_ref, w_ref, o_ref):
    is_ctx = _is_ctx(pl.program_id(1), x_ref.shape[0], n_ctx)
    x = x_ref[...]
    u = _norm_mod(x, g_ref[...], m_ref, 0, is_ctx)
    y = y_ref[0] + y_ref[1]
    z = _dot(_gelu_tanh(y + dsk_ref[...] * u).astype(BF16), w_ref[...])
    dm = z.shape[-1] // 2
    out = z[:, :dm] * jax.nn.sigmoid(z[:, dm:])
    o_ref[...] = x + _mod_row(m_ref, 2, is_ctx) * out


def mixer_s5(xs, mods, g, n_ctx, lam_re, lam_im, b_re, b_im, c_re, c_im, log_step, d_skip, w_glu):
    b, n, d = xs.shape
    groups = d // S5_GROUP
    p = S5_STATE
    gb = S5_GROUPS_PER_BLOCK
    nblk = groups // gb
    cin = gb * S5_GROUP
    half = gb * p

    full = lambda shape: pl.BlockSpec(shape, lambda dd: tuple(0 for _ in shape))
    ar, ai, br, bi = pl.pallas_call(
        _s5_disc_kernel,
        grid=(2,),
        in_specs=[
            pl.BlockSpec((None, groups, p), lambda dd: (dd, 0, 0)),
            pl.BlockSpec((None, groups, p), lambda dd: (dd, 0, 0)),
            pl.BlockSpec((None, groups, 1), lambda dd: (dd, 0, 0)),
            pl.BlockSpec((None, S5_GROUP, groups, p), lambda dd: (dd, 0, 0, 0)),
            pl.BlockSpec((None, S5_GROUP, groups, p), lambda dd: (dd, 0, 0, 0)),
        ],
        out_specs=[
            pl.BlockSpec((None, groups, p), lambda dd: (dd, 0, 0)),
            pl.BlockSpec((None, groups, p), lambda dd: (dd, 0, 0)),
            pl.BlockSpec((None, S5_GROUP, groups, p), lambda dd: (dd, 0, 0, 0)),
            pl.BlockSpec((None, S5_GROUP, groups, p), lambda dd: (dd, 0, 0, 0)),
        ],
        out_shape=[
            jax.ShapeDtypeStruct((2, groups, p), F32),
            jax.ShapeDtypeStruct((2, groups, p), F32),
            jax.ShapeDtypeStruct((2, S5_GROUP, groups, p), F32),
            jax.ShapeDtypeStruct((2, S5_GROUP, groups, p), F32),
        ],
        compiler_params=_params("arbitrary"),
        name="s5_disc",
    )(lam_re, lam_im, log_step.reshape(2, groups, 1),
      jnp.transpose(b_re, (0, 3, 1, 2)), jnp.transpose(b_im, (0, 3, 1, 2)))

    eye = jnp.eye(gb, dtype=F32)

    def in_blocks(t):
        t = t.reshape(2, S5_GROUP, nblk, gb, p)
        return jnp.einsum('ab,dcjap->djacbp', eye, t).reshape(2, nblk, cin, half)

    bcat = jnp.concatenate([in_blocks(br), in_blocks(bi)], axis=-1).astype(BF16)

    def out_blocks(t):
        t = t.reshape(2, nblk, gb, S5_GROUP, p)
        return jnp.einsum('ab,djacp->djapbc', eye, t).reshape(2, nblk, half, cin)

    ccat = jnp.concatenate([out_blocks(c_re), -out_blocks(c_im)], axis=2).astype(BF16)
    ar_b = ar.reshape(2, nblk, 1, half)
    ai_b = ai.reshape(2, nblk, 1, half)

    u_tm = pl.pallas_call(
        functools.partial(_s5_in_kernel, n_ctx),
        grid=(b, n // TILE),
        in_specs=[
            pl.BlockSpec((None, TILE, d), lambda i, t: (i, t, 0)),
            pl.BlockSpec((1, d), lambda i, t: (0, 0)),
            pl.BlockSpec((2, 8, d), lambda i, t: (i, 0, 0)),
        ],
        out_specs=pl.BlockSpec((TILE, d), lambda i, t: (t, i)),
        out_shape=jax.ShapeDtypeStruct((n, b * d), BF16),
        compiler_params=_params("arbitrary", "arbitrary"),
        name="s5_in",
    )(xs, g.reshape(1, d), mods)

    tc = S5_CHUNK
    nchunks = n // tc
    ncc = n_ctx // tc
    rows = tc * b

    def chunk_of(dd, s):
        rev = jnp.where(s < ncc, ncc - 1 - s, nchunks - 1 - (s - ncc))
        return jnp.where(dd == 0, s, rev)

    y2 = pl.pallas_call(
        functools.partial(_s5_scan_kernel, b),
        grid=(2, nblk, nchunks),
        in_specs=[
            pl.BlockSpec((rows, cin), lambda dd, j, s: (chunk_of(dd, s), j)),
            pl.BlockSpec((None, None, cin, 2 * half), lambda dd, j, s: (dd, j, 0, 0)),
            pl.BlockSpec((None, None, 1, half), lambda dd, j, s: (dd, j, 0, 0)),
            pl.BlockSpec((None, None, 1, half), lambda dd, j, s: (dd, j, 0, 0)),
            pl.BlockSpec((None, None, 2 * half, cin), lambda dd, j, s: (dd, j, 0, 0)),
        ],
        out_specs=pl.BlockSpec((None, rows, cin), lambda dd, j, s: (dd, chunk_of(dd, s), j)),
        out_shape=jax.ShapeDtypeStruct((2, n * b, d), F32),
        scratch_shapes=[
            pltpu.VMEM((rows, 2 * half), F32),
            pltpu.VMEM((rows, 2 * half), BF16),
            pltpu.VMEM((b, 2 * half), F32),
        ],
        compiler_params=_params("arbitrary", "arbitrary", "arbitrary"),
        name="s5_scan",
    )(u_tm.reshape(n * b, d), bcat, ar_b, ai_b, ccat)

    return pl.pallas_call(
        functools.partial(_s5_glu_kernel, n_ctx),
        grid=(b, n // TILE),
        in_specs=[
            pl.BlockSpec((None, TILE, d), lambda i, t: (i, t, 0)),
            pl.BlockSpec((1, d), lambda i, t: (0, 0)),
            pl.BlockSpec((2, 8, d), lambda i, t: (i, 0, 0)),
            pl.BlockSpec((2, TILE, d), lambda i, t: (0, t, i)),
            pl.BlockSpec((1, d), lambda i, t: (0, 0)),
            pl.BlockSpec((d, 2 * d), lambda i, t: (0, 0)),
        ],
        out_specs=pl.BlockSpec((None, TILE, d), lambda i, t: (i, t, 0)),
        out_shape=jax.ShapeDtypeStruct(xs.shape, F32),
        input_output_aliases={0: 0},
        compiler_params=_params("arbitrary", "arbitrary"),
        name="s5_glu",
    )(xs, g.reshape(1, d), mods, y2.reshape(2, n, b * d), d_skip.reshape(1, d), w_glu.astype(BF16))


def _gla_in_kernel(n_ctx, x_ref, g_ref, m_ref, w_ref, w1_ref, o_ref, r_ref):
    is_ctx = _is_ctx(pl.program_id(1), x_ref.shape[0], n_ctx)
    hb = _norm_mod(x_ref[...], g_ref[...], m_ref, 0, is_ctx).astype(BF16)
    o_ref[...] = _dot(hb, w_ref[...]).astype(BF16)
    r_ref[...] = _dot(hb, w1_ref[...])


def _gla_scan_kernel(dk, dv, qf_ref, rf_ref, qr_ref, rr_ref, w2_ref, gb_ref, of_ref, or_ref, s_ref):
    c = GLA_CHUNK
    nck = qf_ref.shape[0] // c
    kd = GLA_HEADS * dk

    @pl.when(pl.program_id(1) == 0)
    def _():
        s_ref[...] = jnp.zeros_like(s_ref)

    row = lax.broadcasted_iota(jnp.int32, (c, c), 0)
    col = lax.broadcasted_iota(jnp.int32, (c, c), 1)
    masks = (row >= col, row <= col)

    refs = ((qf_ref, rf_ref, of_ref), (qr_ref, rr_ref, or_ref))
    chains = [(dd, h) for dd in range(2) for h in range(GLA_HEADS)]
    for i in range(nck):
        rows = [slice(i * c, (i + 1) * c), slice((nck - 1 - i) * c, (nck - i) * c)]
        decay = []
        for dd in range(2):
            rb = refs[dd][1][rows[dd], :].astype(BF16)
            lg = jax.nn.log_sigmoid(_dot(rb, w2_ref[dd]) + gb_ref[dd]) * (1.0 / GLA_GATE_NORM)
            bcum = jnp.dot(masks[dd].astype(F32), lg, preferred_element_type=F32,
                           precision=lax.Precision.HIGHEST)
            blast = jnp.sum(lg, axis=0, keepdims=True)
            decay.append((jnp.exp(bcum), jnp.exp(-bcum), jnp.exp(blast - bcum), jnp.exp(blast)))
        qg, att, kdec, vv, st = {}, {}, {}, {}, {}
        for ch in chains:
            dd, h = ch
            x_ref = refs[dd][0]
            hs = slice(h * dk, (h + 1) * dk)
            q = x_ref[rows[dd], hs].astype(F32) * (float(dk) ** -0.5)
            k = x_ref[rows[dd], kd + h * dk:kd + (h + 1) * dk].astype(F32)
            vv[ch] = x_ref[rows[dd], 2 * kd + h * dv:2 * kd + (h + 1) * dv]
            qg[ch] = (q * decay[dd][0][:, hs]).astype(BF16)
            kg = (k * decay[dd][1][:, hs]).astype(BF16)
            kdec[ch] = k * decay[dd][2][:, hs]
            att[ch] = jnp.where(masks[dd], _dot_nt(qg[ch], kg), 0.0).astype(BF16)
        for ch in chains:
            dd, h = ch
            st[ch] = s_ref[dd, h]
            refs[dd][2][rows[dd], h * dv:(h + 1) * dv] = (
                _dot(att[ch], vv[ch]) + _dot(qg[ch], st[ch].astype(BF16)))
        for ch in chains:
            dd, h = ch
            eb = jnp.broadcast_to(decay[dd][3][:, h * dk:(h + 1) * dk], (dk - c, dk))
            mt = jnp.concatenate([kdec[ch], eb], axis=0).T
            s_ref[dd, h] = mt[:, c:c + 1] * st[ch] + _dot(mt[:, 0:c].astype(BF16), vv[ch])


def _gla_out_kernel(n_ctx, x_ref, of_ref, or_ref, gt_ref, on_ref, w_ref, m_ref, o_ref):
    is_ctx = _is_ctx(pl.program_id(1), x_ref.shape[0], n_ctx)
    o = of_ref[...] + or_ref[...]
    gt = gt_ref[...].astype(F32)
    dv = on_ref.shape[-1]
    parts = []
    for h in range(GLA_HEADS):
        sl = slice(h * dv, (h + 1) * dv)
        parts.append(_rms(o[:, sl], on_ref[...]) * (gt[:, sl] * jax.nn.sigmoid(gt[:, sl])))
    y = jnp.concatenate(parts, axis=-1).astype(BF16)
    o_ref[...] = x_ref[...] + _mod_row(m_ref, 2, is_ctx) * _dot(y, w_ref[...])


def mixer_gla(xs, mods, g, n_ctx, w_in, gk_w1, gk_w2, gk_b, o_norm, w_o):
    b, n, d = xs.shape
    kd = d // 2
    vd = d
    dk = kd // GLA_HEADS
    dv = vd // GLA_HEADS
    rk = GLA_GATE_RANK
    w1 = jnp.zeros((d, 128), F32).at[:, :rk].set(gk_w1[0]).at[:, rk:2 * rk].set(gk_w1[1]).astype(BF16)
    w2 = jnp.zeros((2, 128, kd), F32)
    for dd in range(2):
        w2 = w2.at[dd, dd * rk:(dd + 1) * rk, :].set(gk_w2[dd])
    w2 = w2.astype(BF16)
    gbias = gk_b.reshape(2, 1, kd)
    nw = w_in.shape[-1]

    qkvg, r = pl.pallas_call(
        functools.partial(_gla_in_kernel, n_ctx),
        grid=(b, n // TILE),
        in_specs=[
            pl.BlockSpec((None, TILE, d), lambda i, t: (i, t, 0)),
            pl.BlockSpec((1, d), lambda i, t: (0, 0)),
            pl.BlockSpec((2, 8, d), lambda i, t: (i, 0, 0)),
            pl.BlockSpec((d, nw), lambda i, t: (0, 0)),
            pl.BlockSpec((d, 128), lambda i, t: (0, 0)),
        ],
        out_specs=[
            pl.BlockSpec((None, TILE, nw), lambda i, t: (i, t, 0)),
            pl.BlockSpec((None, TILE, 128), lambda i, t: (i, t, 0)),
        ],
        out_shape=[
            jax.ShapeDtypeStruct((b, n, nw), BF16),
            jax.ShapeDtypeStruct((b, n, 128), F32),
        ],
        compiler_params=_params("arbitrary", "arbitrary"),
        name="gla_in",
    )(xs, g.reshape(1, d), mods, w_in.astype(BF16), w1)

    nt = n // TILE
    nct = n_ctx // TILE

    def rev_tile(s):
        return jnp.where(s < nct, nct - 1 - s, nt - 1 - (s - nct))

    qkv_w = 2 * kd + vd
    o_fwd, o_rev = pl.pallas_call(
        functools.partial(_gla_scan_kernel, dk, dv),
        grid=(b, nt),
        in_specs=[
            pl.BlockSpec((None, TILE, qkv_w), lambda i, s: (i, s, 0)),
            pl.BlockSpec((None, TILE, 128), lambda i, s: (i, s, 0)),
            pl.BlockSpec((None, TILE, qkv_w), lambda i, s: (i, rev_tile(s), 0)),
            pl.BlockSpec((None, TILE, 128), lambda i, s: (i, rev_tile(s), 0)),
            pl.BlockSpec(w2.shape, lambda i, s: (0, 0, 0)),
            pl.BlockSpec(gbias.shape, lambda i, s: (0, 0, 0)),
        ],
        out_specs=[
            pl.BlockSpec((None, TILE, vd), lambda i, s: (i, s, 0)),
            pl.BlockSpec((None, TILE, vd), lambda i, s: (i, rev_tile(s), 0)),
        ],
        out_shape=[jax.ShapeDtypeStruct((b, n, vd), F32), jax.ShapeDtypeStruct((b, n, vd), F32)],
        scratch_shapes=[pltpu.VMEM((2, GLA_HEADS, dk, dv), F32)],
        compiler_params=_params("arbitrary", "arbitrary"),
        name="gla_scan",
    )(qkvg, r, qkvg, r, w2, gbias)

    return pl.pallas_call(
        functools.partial(_gla_out_kernel, n_ctx),
        grid=(b, nt),
        in_specs=[
            pl.BlockSpec((None, TILE, d), lambda i, t: (i, t, 0)),
            pl.BlockSpec((None, TILE, vd), lambda i, t: (i, t, 0)),
            pl.BlockSpec((None, TILE, vd), lambda i, t: (i, t, 0)),
            pl.BlockSpec((None, TILE, vd), lambda i, t: (i, t, (2 * kd + vd) // vd)),
            pl.BlockSpec((1, dv), lambda i, t: (0, 0)),
            pl.BlockSpec((vd, d), lambda i, t: (0, 0)),
            pl.BlockSpec((2, 8, d), lambda i, t: (i, 0, 0)),
        ],
        out_specs=pl.BlockSpec((None, TILE, d), lambda i, t: (i, t, 0)),
        out_shape=jax.ShapeDtypeStruct(xs.shape, F32),
        input_output_aliases={0: 0},
        compiler_params=_params("arbitrary", "arbitrary"),
        name="gla_out",
    )(xs, o_fwd, o_rev, qkvg, o_norm.reshape(1, dv), w_o.astype(BF16), mods)


PEER_TT = 768
PEER_EB = 2048
PEER_PIECES = (1024, 1024)
PEER_LAYER_PIECES = ((1024, 1024), (128, 896, 1024), (256, 768, 1024), (512, 1536))
PEER_RANKS = PEER_TOPK + 1


SUBLANES = 8


def _sorting_network(n):
    pairs = []
    p = 1
    while p < n:
        k = p
        while k >= 1:
            for j in range(k % p, n - k, 2 * k):
                for i in range(min(k, n - j - k)):
                    if (i + j) // (2 * p) == (i + j + k) // (2 * p):
                        pairs.append((i + j, i + j + k))
            k //= 2
        p *= 2
    return pairs


def _sublane_max_all(x):
    for shift in (4, 2, 1):
        x = jnp.maximum(x, pltpu.roll(x, shift, 0))
    return x


def _pop_top(lists, k):
    neg = jnp.full(lists[0].shape, -jnp.inf, F32)
    out = []
    for r in range(k):
        m = _sublane_max_all(lists[0])
        out.append(m)
        if r == k - 1:
            break
        hit = lists[0] == m
        keep = min(len(lists), k - 1 - r)
        lists = [jnp.where(hit, lists[p + 1] if p + 1 < len(lists) else neg, lists[p]) for p in range(keep)]
    return out


def _top_rows(s, k):
    tiles = [s[SUBLANES * v:SUBLANES * (v + 1), :] for v in range(s.shape[0] // SUBLANES)]
    for i, j in _sorting_network(len(tiles)):
        tiles[i], tiles[j] = jnp.maximum(tiles[i], tiles[j]), jnp.minimum(tiles[i], tiles[j])
    return _pop_top(tiles, k)


def _pair_threshold(a, b):
    k = PEER_RANKS
    lens = (k, k // 2, k // 3, k // 4, k - 4, k // 2 - 4, k // 3 - 4, 0)
    sub = lax.broadcasted_iota(jnp.int32, a[0].shape, 0)
    length = jnp.zeros(a[0].shape, jnp.int32)
    for c, ln in enumerate(lens):
        length = jnp.where(sub == c, ln, length)
    a_fix = jnp.where(sub == 0, a[0], jnp.where(sub == 1, a[1], jnp.where(sub == 2, a[2], a[3])))
    b_fix = jnp.where(sub == 4, b[0], jnp.where(sub == 5, b[1], b[2]))
    lists = []
    for p in range(k):
        a_p = jnp.where(sub < 4, a_fix, jnp.where(sub < 6, a[min(4 + p, k - 1)], a[4]))
        b_p = jnp.where(sub < 4, b[p], b_fix)
        lists.append(jnp.where(length > p, a_p + b_p, -jnp.inf))
    return _pop_top(lists, k)


def _peer_score_kernel(n_ctx, x_ref, g_ref, m_ref, wqt_ref, keys_ref, h_ref, e0_ref, e1_ref, th_ref, sc_ref):
    tt = x_ref.shape[0]
    is_ctx = _is_ctx(pl.program_id(1), tt, n_ctx)
    ht = _norm_mod(x_ref[...], g_ref[...], m_ref, 3, is_ctx).T.astype(BF16)
    h_ref[...] = ht
    qt = _dot(wqt_ref[...], ht).astype(BF16)
    dkey = keys_ref.shape[-1]
    for hp in range(2 * PEER_HEADS):
        sc_ref[hp] = _dot(keys_ref[hp], qt[hp * dkey:(hp + 1) * dkey, :])
    head_row = lax.broadcasted_iota(jnp.int32, th_ref.shape, 0)

    def head(hd, th_acc):
        th_parts = []
        for lt in range(tt // LANES):
            ls = slice(lt * LANES, (lt + 1) * LANES)
            s0 = sc_ref[2 * hd, :, ls]
            s1 = sc_ref[2 * hd + 1, :, ls]
            a = _top_rows(s0, PEER_RANKS)
            b = _top_rows(s1, PEER_RANKS)
            v = _pair_threshold(a, b)
            tau = 0.5 * (v[PEER_TOPK - 1] + v[PEER_TOPK])
            z = jnp.ones_like(v[0])
            for kk in range(1, PEER_TOPK):
                z = z + jnp.exp(v[kk] - v[0])
            rz = (1.0 / z)[0:1, :]
            e0_ref[hd, :, ls] = jnp.exp(s0 - a[0][0:1, :]) * rz
            e1_ref[hd, :, ls] = jnp.exp(s1 - b[0][0:1, :])
            th_parts.append(jnp.exp(tau - v[0])[0:1, :] * rz)
        return jnp.where(head_row == hd, jnp.concatenate(th_parts, axis=1), th_acc)

    th_ref[...] = lax.fori_loop(0, PEER_HEADS, head, jnp.zeros(th_ref.shape, F32))


def _gelu_tanh_sigmoid_form(x):
    c0 = -2.0 * 0.7978845608028654 * 1.4426950408889634
    c1 = c0 * 0.044715
    return x / (1.0 + jnp.exp2(x * (c0 + c1 * (x * x))))


def _peer_dense_kernel(n_ctx, pieces, ht_ref, e0_ref, e1_ref, th_ref, u_ref, vt_ref, x_ref, m_ref, o_ref,
                       w_ref, acc_ref):
    e = pl.program_id(2)
    eb, tt = w_ref.shape
    nk = e1_ref.shape[1]

    @pl.when(e == 0)
    def _():
        acc_ref[...] = jnp.zeros_like(acc_ref)

    ht = ht_ref[...]
    r0 = 0
    for pi, rows in enumerate(pieces):
        a_sb = _dot(u_ref[r0:r0 + rows, :], ht)
        nrow, nlane = rows // nk, tt // LANES
        order = ([(ii, lt) for lt in range(nlane) for ii in range(nrow)] if pi == 0
                 else [(ii, lt) for ii in range(nrow) for lt in range(nlane)])
        for ii, lt in order:
            i = r0 // nk + ii
            ls = slice(lt * LANES, (lt + 1) * LANES)
            gsum = jnp.zeros((nk, LANES), F32)
            for hd in range(PEER_HEADS):
                p = e0_ref[hd, i:i + 1, ls] * e1_ref[hd, :, ls]
                gsum = gsum + jnp.where(p > th_ref[hd:hd + 1, ls], p, 0.0)
            act = _gelu_tanh_sigmoid_form(a_sb[ii * nk:(ii + 1) * nk, ls])
            w_ref[i * nk:(i + 1) * nk, ls] = (act * gsum).astype(BF16)
        r0 += rows
    acc_ref[...] += _dot(vt_ref[...], w_ref[...])

    @pl.when(e == pl.num_programs(2) - 1)
    def _():
        is_ctx = _is_ctx(pl.program_id(1), tt, n_ctx)
        o_ref[...] = x_ref[...] + _mod_row(m_ref, 5, is_ctx) * acc_ref[...].T


def peer_layer(xs, mods, g, n_ctx, w_q, keys, u_tab, v_tab, eb=PEER_EB, pieces=PEER_PIECES):
    b, n, d = xs.shape
    tt = PEER_TT
    ntt = n // tt
    ne = u_tab.shape[0]
    nk = PEER_NKEYS
    hq = w_q.shape[-1]
    wqt = jnp.transpose(w_q).astype(BF16)
    keys_b = keys.reshape(PEER_HEADS * 2, nk, keys.shape[-1]).astype(BF16)
    u_b = u_tab.astype(BF16)

    h2t, e0, e1, th = pl.pallas_call(
        functools.partial(_peer_score_kernel, n_ctx),
        grid=(b, ntt),
        in_specs=[
            pl.BlockSpec((None, tt, d), lambda i, t: (i, t, 0)),
            pl.BlockSpec((1, d), lambda i, t: (0, 0)),
            pl.BlockSpec((2, 8, d), lambda i, t: (i, 0, 0)),
            pl.BlockSpec((hq, d), lambda i, t: (0, 0)),
            pl.BlockSpec(keys_b.shape, lambda i, t: (0, 0, 0)),
        ],
        out_specs=[
            pl.BlockSpec((d, tt), lambda i, t: (0, i * ntt + t)),
            pl.BlockSpec((PEER_HEADS, nk, tt), lambda i, t: (0, 0, i * ntt + t)),
            pl.BlockSpec((PEER_HEADS, nk, tt), lambda i, t: (0, 0, i * ntt + t)),
            pl.BlockSpec((PEER_HEADS, tt), lambda i, t: (0, i * ntt + t)),
        ],
        out_shape=[
            jax.ShapeDtypeStruct((d, b * n), BF16),
            jax.ShapeDtypeStruct((PEER_HEADS, nk, b * n), F32),
            jax.ShapeDtypeStruct((PEER_HEADS, nk, b * n), F32),
            jax.ShapeDtypeStruct((PEER_HEADS, b * n), F32),
        ],
        scratch_shapes=[pltpu.VMEM((2 * PEER_HEADS, nk, tt), F32)],
        compiler_params=_params("arbitrary", "arbitrary"),
        name="peer_score",
    )(xs, g.reshape(1, d), mods, wqt, keys_b)

    nblk = ne // eb
    vt_b = jnp.transpose(v_tab.reshape(nblk, eb, d), (0, 2, 1)).astype(BF16)
    return pl.pallas_call(
        functools.partial(_peer_dense_kernel, n_ctx, pieces),
        grid=(b, ntt, nblk),
        in_specs=[
            pl.BlockSpec((d, tt), lambda i, t, e: (0, i * ntt + t)),
            pl.BlockSpec((PEER_HEADS, eb // nk, tt), lambda i, t, e: (0, e, i * ntt + t)),
            pl.BlockSpec((PEER_HEADS, nk, tt), lambda i, t, e: (0, 0, i * ntt + t)),
            pl.BlockSpec((PEER_HEADS, tt), lambda i, t, e: (0, i * ntt + t)),
            pl.BlockSpec((eb, d), lambda i, t, e: (e, 0)),
            pl.BlockSpec((None, d, eb), lambda i, t, e: (e, 0, 0)),
            pl.BlockSpec((None, tt, d), lambda i, t, e: (i, t, 0)),
            pl.BlockSpec((2, 8, d), lambda i, t, e: (i, 0, 0)),
        ],
        out_specs=pl.BlockSpec((None, tt, d), lambda i, t, e: (i, t, 0)),
        out_shape=jax.ShapeDtypeStruct(xs.shape, F32),
        scratch_shapes=[
            pltpu.VMEM((eb, tt), BF16),
            pltpu.VMEM((d, tt), F32),
        ],
        input_output_aliases={6: 0},
        compiler_params=_params("arbitrary", "arbitrary", "arbitrary"),
        name="peer_dense",
    )(h2t, e0, e1, th, u_b, vt_b, xs, mods)


def _final_kernel(x_ref, g_ref, o_ref):
    o_ref[...] = _rms(x_ref[...], g_ref[...])


def final_norm(xs, g, n_ctx):
    b, n, d = xs.shape
    off = n_ctx // TILE
    return pl.pallas_call(
        _final_kernel,
        grid=(b, (n - n_ctx) // TILE),
        in_specs=[
            pl.BlockSpec((None, TILE, d), lambda i, t: (i, t + off, 0)),
            pl.BlockSpec((1, d), lambda i, t: (0, 0)),
        ],
        out_specs=pl.BlockSpec((None, TILE, d), lambda i, t: (i, t, 0)),
        out_shape=jax.ShapeDtypeStruct((b, n - n_ctx, d), F32),
        compiler_params=_params("arbitrary", "arbitrary"),
        name="final_norm",
    )(xs, g.reshape(1, d))


def kernel(x, c, ctx, c_ctx, norm_g, ada_w, ada_b, mla_w_in, mla_q_norm, mla_w_uq, mla_kv_norm, mla_w_ukv, mla_w_o, s5_lam_re, s5_lam_im, s5_b_re, s5_b_im, s5_c_re, s5_c_im, s5_log_step, s5_d, s5_w_glu, gla_w_in, gla_gk_w1, gla_gk_w2, gla_gk_b, gla_o_norm, gla_w_o, peer_w_q, peer_keys, peer_u, peer_v, final_g):
    b, n_lat, d = x.shape
    n_ctx = ctx.shape[1]
    depth = ada_w.shape[0]
    xs = jnp.concatenate([ctx, x], axis=1)

    r = 8 * ((b + 1 + 7) // 8)
    cond = jnp.zeros((r, d), F32).at[:b].set(c).at[b].set(c_ctx)
    ada = ada_all(cond, ada_w, ada_b).reshape(depth, r, 6, d)
    lat = ada[:, :b]
    ctxp = jnp.broadcast_to(ada[:, b:b + 1], lat.shape)
    mods_all = jnp.stack([ctxp, lat], axis=2)
    mods_all = jnp.pad(mods_all, ((0, 0), (0, 0), (0, 0), (0, 2), (0, 0))).reshape(depth, b * 2, 8, d)

    for i in range(depth):
        kind, j = i % N_MIXERS, i // N_MIXERS
        mods = mods_all[i]
        if kind == 0:
            xs = mixer_mla(xs, mods, norm_g[i, 0], n_ctx, i < depth - 1, mla_w_in[j], mla_q_norm[j], mla_w_uq[j],
                           mla_kv_norm[j], mla_w_ukv[j], mla_w_o[j])
        elif kind == 1:
            xs = mixer_s5(xs, mods, norm_g[i, 0], n_ctx, s5_lam_re[j], s5_lam_im[j], s5_b_re[j], s5_b_im[j],
                          s5_c_re[j], s5_c_im[j], s5_log_step[j], s5_d[j], s5_w_glu[j])
        else:
            xs = mixer_gla(xs, mods, norm_g[i, 0], n_ctx, gla_w_in[j], gla_gk_w1[j], gla_gk_w2[j],
                           gla_gk_b[j], gla_o_norm[j], gla_w_o[j])
        xs = peer_layer(xs, mods, norm_g[i, 1], n_ctx, peer_w_q[i], peer_keys[i], peer_u[i], peer_v[i],
                        pieces=PEER_LAYER_PIECES[i % len(PEER_LAYER_PIECES)])
    return final_norm(xs, final_g, n_ctx)
```

```python
import functools
import math

import numpy as np
import jax
import jax.numpy as jnp
from jax import lax
from jax.experimental import pallas as pl
from jax.experimental.pallas import tpu as pltpu

F32 = jnp.float32
BF16 = jnp.bfloat16

EPS = 1e-6
GRID_W = 64
N_MIXERS = 3

MLA_HEADS = 16
MLA_NOPE = 64
MLA_ROPE = 32
MLA_V = 64
MLA_Q_LORA = 384
MLA_KV_LORA = 256
ROPE_BASE = 10000.0
MLA_HEAD_PAD = 128
MLA_HEADS_PER_STEP = 2
MLA_Q_ROWS = 1024

S5_GROUP = 16
S5_STATE = 64
S5_GROUPS_PER_BLOCK = 8
S5_CHUNK = 128

GLA_HEADS = 4
GLA_GATE_RANK = 16
GLA_GATE_NORM = 16.0
GLA_CHUNK = 64

PEER_HEADS = 8
PEER_NKEYS = 128
PEER_TOPK = 16

LANES = 128
TILE = 256
VMEM_LIMIT = 56 * 1024 * 1024


def _params(*sem):
    return pltpu.CompilerParams(dimension_semantics=sem, vmem_limit_bytes=VMEM_LIMIT)


def _gelu_tanh(x):
    return 0.5 * x * (1.0 + jnp.tanh(0.7978845608028654 * (x + 0.044715 * x * x * x)))


def _is_ctx(tile_idx, rows, n_ctx):
    pos = tile_idx * rows + lax.broadcasted_iota(jnp.int32, (rows, 1), 0)
    return pos < n_ctx


def _mod_row(m_ref, row, is_ctx):
    return jnp.where(is_ctx, m_ref[0, row:row + 1, :], m_ref[1, row:row + 1, :])


def _rms(x, g):
    return x * lax.rsqrt(jnp.mean(x * x, axis=-1, keepdims=True) + EPS) * g


def _norm_mod(x, g, m_ref, row0, is_ctx):
    return _rms(x, g) * (1.0 + _mod_row(m_ref, row0 + 1, is_ctx)) + _mod_row(m_ref, row0, is_ctx)


def _dot(a, b):
    return jnp.dot(a, b, preferred_element_type=F32)


def _dot_nt(a, b):
    return lax.dot_general(a, b, (((1,), (1,)), ((), ())), preferred_element_type=F32)


def _ada_kernel(c_ref, w_ref, b_ref, o_ref):
    c = c_ref[...]
    s = (c * jax.nn.sigmoid(c)).astype(BF16)
    o_ref[...] = _dot(s, w_ref[...].astype(BF16)) + b_ref[...]


def ada_all(cond, ada_w, ada_b):
    depth, d, n6 = ada_w.shape
    r = cond.shape[0]
    tn = 1024
    return pl.pallas_call(
        _ada_kernel,
        grid=(depth, n6 // tn),
        in_specs=[
            pl.BlockSpec((r, d), lambda l, j: (0, 0)),
            pl.BlockSpec((None, d, tn), lambda l, j: (l, 0, j)),
            pl.BlockSpec((None, 1, tn), lambda l, j: (l, 0, j)),
        ],
        out_specs=pl.BlockSpec((None, r, tn), lambda l, j: (l, 0, j)),
        out_shape=jax.ShapeDtypeStruct((depth, r, n6), F32),
        compiler_params=_params("arbitrary", "arbitrary"),
        name="ada",
    )(cond, ada_w, ada_b.reshape(depth, 1, n6))


def _proj_residual_kernel(n_ctx, gate_row, tile_off, x_ref, y_ref, w_ref, m_ref, o_ref):
    is_ctx = _is_ctx(pl.program_id(1) + tile_off, x_ref.shape[0], n_ctx)
    f = _dot(y_ref[...], w_ref[...])
    o_ref[...] = x_ref[...] + _mod_row(m_ref, gate_row, is_ctx) * f


def proj_residual(xs, y, w, mods, n_ctx, gate_row, tile_off=0):
    b, n, d = xs.shape
    k = y.shape[-1]
    return pl.pallas_call(
        functools.partial(_proj_residual_kernel, n_ctx, gate_row, tile_off),
        grid=(b, y.shape[1] // TILE),
        in_specs=[
            pl.BlockSpec((None, TILE, d), lambda i, t: (i, t + tile_off, 0)),
            pl.BlockSpec((None, TILE, k), lambda i, t: (i, t, 0)),
            pl.BlockSpec((k, d), lambda i, t: (0, 0)),
            pl.BlockSpec((2, 8, d), lambda i, t: (i, 0, 0)),
        ],
        out_specs=pl.BlockSpec((None, TILE, d), lambda i, t: (i, t + tile_off, 0)),
        out_shape=jax.ShapeDtypeStruct(xs.shape, F32),
        input_output_aliases={0: 0},
        compiler_params=_params("arbitrary", "arbitrary"),
        name="proj_residual",
    )(xs, y, w, mods)


def _rope_tables(n_ctx, n_lat, scale):
    half = MLA_ROPE // 2
    rows_n = n_lat // GRID_W
    rows = np.repeat(np.arange(rows_n, dtype=np.float32), GRID_W)
    cols = np.tile(np.arange(GRID_W, dtype=np.float32), rows_n)
    inv = (ROPE_BASE ** (-np.arange(0, half, 2, dtype=np.float32) / half)).astype(np.float32)
    ang_r = rows[:, None] * inv
    ang_c = cols[:, None] * inv
    cos = np.concatenate([np.cos(ang_r), np.cos(ang_r), np.cos(ang_c), np.cos(ang_c)], axis=1)
    sin = np.concatenate([-np.sin(ang_r), np.sin(ang_r), -np.sin(ang_c), np.sin(ang_c)], axis=1)
    n = n_ctx + n_lat
    a = np.zeros((n, MLA_HEAD_PAD), np.float32)
    b = np.zeros((n, MLA_HEAD_PAD), np.float32)
    a[:, :MLA_NOPE] = 1.0
    a[:n_ctx, MLA_NOPE:MLA_NOPE + MLA_ROPE] = 1.0
    a[n_ctx:, MLA_NOPE:MLA_NOPE + MLA_ROPE] = cos
    b[n_ctx:, MLA_NOPE:MLA_NOPE + MLA_ROPE] = sin
    return jnp.asarray(a * scale), jnp.asarray(b * scale)


def _rope_swap_index():
    q = MLA_ROPE // 4
    base = np.arange(MLA_ROPE)
    return np.where((base % (2 * q)) < q, base + q, base - q)


def _mla_in_kernel(n_ctx, x_ref, g_ref, m_ref, wq_ref, wkv_ref, wkp_ref, qn_ref, kvn_ref,
                   wuq1_ref, wuq2_ref, wuk_ref, wuv_ref, aq_ref, bq_ref, ak_ref, bk_ref,
                   qc_ref, ql_ref, k_ref, v_ref):
    is_ctx = _is_ctx(pl.program_id(1), x_ref.shape[0], n_ctx)
    hb = _norm_mod(x_ref[...], g_ref[...], m_ref, 0, is_ctx).astype(BF16)
    cq = _rms(_dot(hb, wq_ref[...]), qn_ref[...]).astype(BF16)
    ckv = _rms(_dot(hb, wkv_ref[...]), kvn_ref[...]).astype(BF16)
    kp = _dot(hb, wkp_ref[...])
    kpe = kp[:, :MLA_HEAD_PAD] * ak_ref[...] + kp[:, MLA_HEAD_PAD:] * bk_ref[...]
    y1 = _dot(cq, wuq1_ref[...])
    y2 = _dot(cq, wuq2_ref[...])
    kk = _dot(ckv, wuk_ref[...])
    aq = aq_ref[...]
    bq = bq_ref[...]
    qs = []
    for h in range(MLA_HEADS):
        sl = slice(h * MLA_HEAD_PAD, (h + 1) * MLA_HEAD_PAD)
        qs.append((y1[:, sl] * aq + y2[:, sl] * bq).astype(BF16))
        k_ref[:, sl] = (kk[:, sl] + kpe).astype(BF16)
    v_ref[...] = _dot(ckv, wuv_ref[...]).astype(BF16)
    in_ctx = pl.program_id(1) * x_ref.shape[0] < n_ctx

    @pl.when(in_ctx)
    def _():
        for h in range(MLA_HEADS):
            qc_ref[:, h * MLA_HEAD_PAD:(h + 1) * MLA_HEAD_PAD] = qs[h]

    @pl.when(jnp.logical_not(in_ctx))
    def _():
        for h in range(MLA_HEADS):
            ql_ref[:, h * MLA_HEAD_PAD:(h + 1) * MLA_HEAD_PAD] = qs[h]


def _mla_attn_kernel(q_ref, k_ref, v_ref, o_ref):
    heads = range(q_ref.shape[-1] // MLA_HEAD_PAD)
    ss = [_dot_nt(q_ref[:, h * MLA_HEAD_PAD:(h + 1) * MLA_HEAD_PAD],
                  k_ref[:, h * MLA_HEAD_PAD:(h + 1) * MLA_HEAD_PAD]) for h in heads]
    ps = [jnp.exp(s - jnp.max(s, axis=-1, keepdims=True)) for s in ss]
    ls = [jnp.sum(p, axis=-1, keepdims=True) for p in ps]
    outs = [_dot(ps[h].astype(BF16), v_ref[:, (h // 2) * 2 * MLA_V:(h // 2 + 1) * 2 * MLA_V]) / ls[h]
            for h in heads]
    lane = lax.broadcasted_iota(jnp.int32, outs[0].shape, 1)
    for pr in range(len(heads) // 2):
        o_ref[:, pr * 2 * MLA_V:(pr + 1) * 2 * MLA_V] = jnp.where(
            lane < MLA_V, outs[2 * pr], outs[2 * pr + 1]).astype(BF16)


def _mla_attention(q, k, v, n_keys, q_rows, hs):
    b, nq, _ = q.shape
    hp = MLA_HEAD_PAD
    return pl.pallas_call(
        _mla_attn_kernel,
        grid=(b, MLA_HEADS // hs, nq // q_rows),
        in_specs=[
            pl.BlockSpec((None, q_rows, hs * hp), lambda i, h, t: (i, t, h)),
            pl.BlockSpec((None, n_keys, hs * hp), lambda i, h, t: (i, 0, h)),
            pl.BlockSpec((None, n_keys, hs * MLA_V), lambda i, h, t: (i, 0, h)),
        ],
        out_specs=pl.BlockSpec((None, q_rows, hs * MLA_V), lambda i, h, t: (i, t, h)),
        out_shape=jax.ShapeDtypeStruct((b, nq, MLA_HEADS * MLA_V), BF16),
        compiler_params=_params("arbitrary", "arbitrary", "arbitrary"),
        name="mla_attn",
    )(q, k, v)


def mixer_mla(xs, mods, g, n_ctx, ctx_out, w_in, q_norm, w_uq, kv_norm, w_ukv, w_o):
    b, n, d = xs.shape
    hp = MLA_HEAD_PAD
    dq = MLA_NOPE + MLA_ROPE
    wq = w_in[:, :MLA_Q_LORA].astype(BF16)
    wkv = w_in[:, MLA_Q_LORA:MLA_Q_LORA + MLA_KV_LORA].astype(BF16)
    w_pe = w_in[:, MLA_Q_LORA + MLA_KV_LORA:]
    swap = _rope_swap_index()
    wkp = jnp.zeros((d, 2 * hp), F32)
    wkp = wkp.at[:, MLA_NOPE:dq].set(w_pe).at[:, hp + MLA_NOPE:hp + dq].set(w_pe[:, swap]).astype(BF16)
    uq = w_uq.reshape(MLA_Q_LORA, MLA_HEADS, dq)
    z = jnp.zeros((MLA_Q_LORA, MLA_HEADS, hp - dq), F32)
    wuq1 = jnp.concatenate([uq, z], axis=-1).reshape(MLA_Q_LORA, MLA_HEADS * hp).astype(BF16)
    zn = jnp.zeros((MLA_Q_LORA, MLA_HEADS, MLA_NOPE), F32)
    wuq2 = jnp.concatenate([zn, uq[:, :, MLA_NOPE:][:, :, swap], z], axis=-1)
    wuq2 = wuq2.reshape(MLA_Q_LORA, MLA_HEADS * hp).astype(BF16)
    ukv = w_ukv.reshape(MLA_KV_LORA, MLA_HEADS, MLA_NOPE + MLA_V)
    zk = jnp.zeros((MLA_KV_LORA, MLA_HEADS, hp - MLA_NOPE), F32)
    wuk = jnp.concatenate([ukv[:, :, :MLA_NOPE], zk], axis=-1).reshape(MLA_KV_LORA, MLA_HEADS * hp).astype(BF16)
    wuv = ukv[:, :, MLA_NOPE:].reshape(MLA_KV_LORA, MLA_HEADS * MLA_V).astype(BF16)
    aq, bq = _rope_tables(n_ctx, n - n_ctx, float(dq) ** -0.5)
    ak, bk = _rope_tables(n_ctx, n - n_ctx, 1.0)

    full = lambda shape: pl.BlockSpec(shape, lambda i, t: tuple(0 for _ in shape))
    tab = pl.BlockSpec((TILE, hp), lambda i, t: (t, 0))
    nct = n_ctx // TILE
    q_ctx, q_lat, k, v = pl.pallas_call(
        functools.partial(_mla_in_kernel, n_ctx),
        grid=(b, n // TILE),
        in_specs=[
            pl.BlockSpec((None, TILE, d), lambda i, t: (i, t, 0)),
            full((1, d)),
            pl.BlockSpec((2, 8, d), lambda i, t: (i, 0, 0)),
            full(wq.shape), full(wkv.shape), full(wkp.shape),
            full((1, MLA_Q_LORA)), full((1, MLA_KV_LORA)),
            full(wuq1.shape), full(wuq2.shape), full(wuk.shape), full(wuv.shape),
            tab, tab, tab, tab,
        ],
        out_specs=[
            pl.BlockSpec((None, TILE, MLA_HEADS * hp), lambda i, t: (i, jnp.minimum(t, nct - 1), 0)),
            pl.BlockSpec((None, TILE, MLA_HEADS * hp), lambda i, t: (i, jnp.maximum(t - nct, 0), 0)),
            pl.BlockSpec((None, TILE, MLA_HEADS * hp), lambda i, t: (i, t, 0)),
            pl.BlockSpec((None, TILE, MLA_HEADS * MLA_V), lambda i, t: (i, t, 0)),
        ],
        out_shape=[
            jax.ShapeDtypeStruct((b, n_ctx, MLA_HEADS * hp), BF16),
            jax.ShapeDtypeStruct((b, n - n_ctx, MLA_HEADS * hp), BF16),
            jax.ShapeDtypeStruct((b, n, MLA_HEADS * hp), BF16),
            jax.ShapeDtypeStruct((b, n, MLA_HEADS * MLA_V), BF16),
        ],
        compiler_params=_params("arbitrary", "arbitrary"),
        name="mla_in",
    )(xs, g.reshape(1, d), mods, wq, wkv, wkp, q_norm.reshape(1, -1), kv_norm.reshape(1, -1),
      wuq1, wuq2, wuk, wuv, aq, bq, ak, bk)

    wo = w_o.astype(BF16)
    o_lat = _mla_attention(q_lat, k, v, n, min(MLA_Q_ROWS, n - n_ctx), MLA_HEADS_PER_STEP)
    xs = proj_residual(xs, o_lat, wo, mods, n_ctx, 2, tile_off=nct)
    if ctx_out:
        o_ctx = _mla_attention(q_ctx, k, v, n_ctx, n_ctx, MLA_HEADS_PER_STEP)
        xs = proj_residual(xs, o_ctx, wo, mods, n_ctx, 2, tile_off=0)
    return xs


def _s5_disc_kernel(lr_ref, li_ref, ls_ref, bre_ref, bim_ref, ar_ref, ai_ref, br_ref, bi_ref):
    lr = lr_ref[...]
    li = li_ref[...]
    dt = jnp.exp(ls_ref[...])
    mag = jnp.exp(lr * dt)
    ar = mag * jnp.cos(li * dt)
    ai = mag * jnp.sin(li * dt)
    den = lr * lr + li * li
    fr = ((ar - 1.0) * lr + ai * li) / den
    fi = (ai * lr - (ar - 1.0) * li) / den
    ar_ref[...] = ar
    ai_ref[...] = ai
    for c in range(S5_GROUP):
        br_ref[c] = fr * bre_ref[c] - fi * bim_ref[c]
        bi_ref[c] = fr * bim_ref[c] + fi * bre_ref[c]


def _s5_in_kernel(n_ctx, x_ref, g_ref, m_ref, u_ref):
    is_ctx = _is_ctx(pl.program_id(1), x_ref.shape[0], n_ctx)
    u_ref[...] = _norm_mod(x_ref[...], g_ref[...], m_ref, 0, is_ctx).astype(BF16)


def _s5_scan_kernel(nb, u_ref, bcat_ref, ar_ref, ai_ref, ccat_ref, y_ref, bu_ref, xs_ref, st_ref):
    d = pl.program_id(0)
    half = ar_ref.shape[-1]

    @pl.when(pl.program_id(2) == 0)
    def _():
        st_ref[...] = jnp.zeros_like(st_ref)

    bu_ref[...] = _dot(u_ref[...], bcat_ref[...])
    ar = jnp.broadcast_to(ar_ref[...], (nb, half))
    ai = jnp.broadcast_to(ai_ref[...], (nb, half))
    steps = u_ref.shape[0] // nb

    def step(i, carry):
        xr, xi = carry
        tt = jnp.where(d == 0, i, steps - 1 - i)
        r0 = pl.multiple_of(tt * nb, nb)
        nxr = ar * xr - ai * xi + bu_ref[pl.ds(r0, nb), 0:half]
        nxi = ar * xi + ai * xr + bu_ref[pl.ds(r0, nb), half:2 * half]
        xs_ref[pl.ds(r0, nb), 0:half] = nxr.astype(BF16)
        xs_ref[pl.ds(r0, nb), half:2 * half] = nxi.astype(BF16)
        return nxr, nxi

    xr, xi = lax.fori_loop(0, steps, step, (st_ref[:, 0:half], st_ref[:, half:2 * half]), unroll=4)
    st_ref[:, 0:half] = xr
    st_ref[:, half:2 * half] = xi
    y_ref[...] = _dot(xs_ref[...], ccat_ref[...])


def _s5_glu_kernel(n_ctx, x_ref, g_ref, m_ref, y_ref, dsk_ref, w_ref, o_ref):
    is_ctx = _is_ctx(pl.program_id(1), x_ref.shape[0], n_ctx)
    x = x_ref[...]
    u = _norm_mod(x, g_ref[...], m_ref, 0, is_ctx)
    y = y_ref[0] + y_ref[1]
    z = _dot(_gelu_tanh(y + dsk_ref[...] * u).astype(BF16), w_ref[...])
    dm = z.shape[-1] // 2
    out = z[:, :dm] * jax.nn.sigmoid(z[:, dm:])
    o_ref[...] = x + _mod_row(m_ref, 2, is_ctx) * out


def mixer_s5(xs, mods, g, n_ctx, lam_re, lam_im, b_re, b_im, c_re, c_im, log_step, d_skip, w_glu):
    b, n, d = xs.shape
    groups = d // S5_GROUP
    p = S5_STATE
    gb = S5_GROUPS_PER_BLOCK
    nblk = groups // gb
    cin = gb * S5_GROUP
    half = gb * p

    full = lambda shape: pl.BlockSpec(shape, lambda dd: tuple(0 for _ in shape))
    ar, ai, br, bi = pl.pallas_call(
        _s5_disc_kernel,
        grid=(2,),
        in_specs=[
            pl.BlockSpec((None, groups, p), lambda dd: (dd, 0, 0)),
            pl.BlockSpec((None, groups, p), lambda dd: (dd, 0, 0)),
            pl.BlockSpec((None, groups, 1), lambda dd: (dd, 0, 0)),
            pl.BlockSpec((None, S5_GROUP, groups, p), lambda dd: (dd, 0, 0, 0)),
            pl.BlockSpec((None, S5_GROUP, groups, p), lambda dd: (dd, 0, 0, 0)),
        ],
        out_specs=[
            pl.BlockSpec((None, groups, p), lambda dd: (dd, 0, 0)),
            pl.BlockSpec((None, groups, p), lambda dd: (dd, 0, 0)),
            pl.BlockSpec((None, S5_GROUP, groups, p), lambda dd: (dd, 0, 0, 0)),
            pl.BlockSpec((None, S5_GROUP, groups, p), lambda dd: (dd, 0, 0, 0)),
        ],
        out_shape=[
            jax.ShapeDtypeStruct((2, groups, p), F32),
            jax.ShapeDtypeStruct((2, groups, p), F32),
            jax.ShapeDtypeStruct((2, S5_GROUP, groups, p), F32),
            jax.ShapeDtypeStruct((2, S5_GROUP, groups, p), F32),
        ],
        compiler_params=_params("arbitrary"),
        name="s5_disc",
    )(lam_re, lam_im, log_step.reshape(2, groups, 1),
      jnp.transpose(b_re, (0, 3, 1, 2)), jnp.transpose(b_im, (0, 3, 1, 2)))

    eye = jnp.eye(gb, dtype=F32)

    def in_blocks(t):
        t = t.reshape(2, S5_GROUP, nblk, gb, p)
        return jnp.einsum('ab,dcjap->djacbp', eye, t).reshape(2, nblk, cin, half)

    bcat = jnp.concatenate([in_blocks(br), in_blocks(bi)], axis=-1).astype(BF16)

    def out_blocks(t):
        t = t.reshape(2, nblk, gb, S5_GROUP, p)
        return jnp.einsum('ab,djacp->djapbc', eye, t).reshape(2, nblk, half, cin)

    ccat = jnp.concatenate([out_blocks(c_re), -out_blocks(c_im)], axis=2).astype(BF16)
    ar_b = ar.reshape(2, nblk, 1, half)
    ai_b = ai.reshape(2, nblk, 1, half)

    u_tm = pl.pallas_call(
        functools.partial(_s5_in_kernel, n_ctx),
        grid=(b, n // TILE),
        in_specs=[
            pl.BlockSpec((None, TILE, d), lambda i, t: (i, t, 0)),
            pl.BlockSpec((1, d), lambda i, t: (0, 0)),
            pl.BlockSpec((2, 8, d), lambda i, t: (i, 0, 0)),
        ],
        out_specs=pl.BlockSpec((TILE, d), lambda i, t: (t, i)),
        out_shape=jax.ShapeDtypeStruct((n, b * d), BF16),
        compiler_params=_params("arbitrary", "arbitrary"),
        name="s5_in",
    )(xs, g.reshape(1, d), mods)

    tc = S5_CHUNK
    nchunks = n // tc
    ncc = n_ctx // tc
    rows = tc * b

    def chunk_of(dd, s):
        rev = jnp.where(s < ncc, ncc - 1 - s, nchunks - 1 - (s - ncc))
        return jnp.where(dd == 0, s, rev)

    y2 = pl.pallas_call(
        functools.partial(_s5_scan_kernel, b),
        grid=(2, nblk, nchunks),
        in_specs=[
            pl.BlockSpec((rows, cin), lambda dd, j, s: (chunk_of(dd, s), j)),
            pl.BlockSpec((None, None, cin, 2 * half), lambda dd, j, s: (dd, j, 0, 0)),
            pl.BlockSpec((None, None, 1, half), lambda dd, j, s: (dd, j, 0, 0)),
            pl.BlockSpec((None, None, 1, half), lambda dd, j, s: (dd, j, 0, 0)),
            pl.BlockSpec((None, None, 2 * half, cin), lambda dd, j, s: (dd, j, 0, 0)),
        ],
        out_specs=pl.BlockSpec((None, rows, cin), lambda dd, j, s: (dd, chunk_of(dd, s), j)),
        out_shape=jax.ShapeDtypeStruct((2, n * b, d), F32),
        scratch_shapes=[
            pltpu.VMEM((rows, 2 * half), F32),
            pltpu.VMEM((rows, 2 * half), BF16),
            pltpu.VMEM((b, 2 * half), F32),
        ],
        compiler_params=_params("arbitrary", "arbitrary", "arbitrary"),
        name="s5_scan",
    )(u_tm.reshape(n * b, d), bcat, ar_b, ai_b, ccat)

    return pl.pallas_call(
        functools.partial(_s5_glu_kernel, n_ctx),
        grid=(b, n // TILE),
        in_specs=[
            pl.BlockSpec((None, TILE, d), lambda i, t: (i, t, 0)),
            pl.BlockSpec((1, d), lambda i, t: (0, 0)),
            pl.BlockSpec((2, 8, d), lambda i, t: (i, 0, 0)),
            pl.BlockSpec((2, TILE, d), lambda i, t: (0, t, i)),
            pl.BlockSpec((1, d), lambda i, t: (0, 0)),
            pl.BlockSpec((d, 2 * d), lambda i, t: (0, 0)),
        ],
        out_specs=pl.BlockSpec((None, TILE, d), lambda i, t: (i, t, 0)),
        out_shape=jax.ShapeDtypeStruct(xs.shape, F32),
        input_output_aliases={0: 0},
        compiler_params=_params("arbitrary", "arbitrary"),
        name="s5_glu",
    )(xs, g.reshape(1, d), mods, y2.reshape(2, n, b * d), d_skip.reshape(1, d), w_glu.astype(BF16))


def _gla_in_kernel(n_ctx, x_ref, g_ref, m_ref, w_ref, w1_ref, o_ref, r_ref):
    is_ctx = _is_ctx(pl.program_id(1), x_ref.shape[0], n_ctx)
    hb = _norm_mod(x_ref[...], g_ref[...], m_ref, 0, is_ctx).astype(BF16)
    o_ref[...] = _dot(hb, w_ref[...]).astype(BF16)
    r_ref[...] = _dot(hb, w1_ref[...])


def _gla_scan_kernel(dk, dv, qf_ref, rf_ref, qr_ref, rr_ref, w2_ref, gb_ref, of_ref, or_ref, s_ref):
    c = GLA_CHUNK
    nck = qf_ref.shape[0] // c
    kd = GLA_HEADS * dk

    @pl.when(pl.program_id(1) == 0)
    def _():
        s_ref[...] = jnp.zeros_like(s_ref)

    row = lax.broadcasted_iota(jnp.int32, (c, c), 0)
    col = lax.broadcasted_iota(jnp.int32, (c, c), 1)
    masks = (row >= col, row <= col)

    refs = ((qf_ref, rf_ref, of_ref), (qr_ref, rr_ref, or_ref))
    chains = [(dd, h) for dd in range(2) for h in range(GLA_HEADS)]
    for i in range(nck):
        rows = [slice(i * c, (i + 1) * c), slice((nck - 1 - i) * c, (nck - i) * c)]
        decay = []
        for dd in range(2):
            rb = refs[dd][1][rows[dd], :].astype(BF16)
            lg = jax.nn.log_sigmoid(_dot(rb, w2_ref[dd]) + gb_ref[dd]) * (1.0 / GLA_GATE_NORM)
            bcum = jnp.dot(masks[dd].astype(F32), lg, preferred_element_type=F32,
                           precision=lax.Precision.HIGHEST)
            blast = jnp.sum(lg, axis=0, keepdims=True)
            decay.append((jnp.exp(bcum), jnp.exp(-bcum), jnp.exp(blast - bcum), jnp.exp(blast)))
        qg, att, kdec, vv, st = {}, {}, {}, {}, {}
        for ch in chains:
            dd, h = ch
            x_ref = refs[dd][0]
            hs = slice(h * dk, (h + 1) * dk)
            q = x_ref[rows[dd], hs].astype(F32) * (float(dk) ** -0.5)
            k = x_ref[rows[dd], kd + h * dk:kd + (h + 1) * dk].astype(F32)
            vv[ch] = x_ref[rows[dd], 2 * kd + h * dv:2 * kd + (h + 1) * dv]
            qg[ch] = (q * decay[dd][0][:, hs]).astype(BF16)
            kg = (k * decay[dd][1][:, hs]).astype(BF16)
            kdec[ch] = k * decay[dd][2][:, hs]
            att[ch] = jnp.where(masks[dd], _dot_nt(qg[ch], kg), 0.0).astype(BF16)
        for ch in chains:
            dd, h = ch
            st[ch] = s_ref[dd, h]
            refs[dd][2][rows[dd], h * dv:(h + 1) * dv] = (
                _dot(att[ch], vv[ch]) + _dot(qg[ch], st[ch].astype(BF16)))
        for ch in chains:
            dd, h = ch
            eb = jnp.broadcast_to(decay[dd][3][:, h * dk:(h + 1) * dk], (dk - c, dk))
            mt = jnp.concatenate([kdec[ch], eb], axis=0).T
            s_ref[dd, h] = mt[:, c:c + 1] * st[ch] + _dot(mt[:, 0:c].astype(BF16), vv[ch])


def _gla_out_kernel(n_ctx, x_ref, of_ref, or_ref, gt_ref, on_ref, w_ref, m_ref, o_ref):
    is_ctx = _is_ctx(pl.program_id(1), x_ref.shape[0], n_ctx)
    o = of_ref[...] + or_ref[...]
    gt = gt_ref[...].astype(F32)
    dv = on_ref.shape[-1]
    parts = []
    for h in range(GLA_HEADS):
        sl = slice(h * dv, (h + 1) * dv)
        parts.append(_rms(o[:, sl], on_ref[...]) * (gt[:, sl] * jax.nn.sigmoid(gt[:, sl])))
    y = jnp.concatenate(parts, axis=-1).astype(BF16)
    o_ref[...] = x_ref[...] + _mod_row(m_ref, 2, is_ctx) * _dot(y, w_ref[...])


def mixer_gla(xs, mods, g, n_ctx, w_in, gk_w1, gk_w2, gk_b, o_norm, w_o):
    b, n, d = xs.shape
    kd = d // 2
    vd = d
    dk = kd // GLA_HEADS
    dv = vd // GLA_HEADS
    rk = GLA_GATE_RANK
    w1 = jnp.zeros((d, 128), F32).at[:, :rk].set(gk_w1[0]).at[:, rk:2 * rk].set(gk_w1[1]).astype(BF16)
    w2 = jnp.zeros((2, 128, kd), F32)
    for dd in range(2):
        w2 = w2.at[dd, dd * rk:(dd + 1) * rk, :].set(gk_w2[dd])
    w2 = w2.astype(BF16)
    gbias = gk_b.reshape(2, 1, kd)
    nw = w_in.shape[-1]

    qkvg, r = pl.pallas_call(
        functools.partial(_gla_in_kernel, n_ctx),
        grid=(b, n // TILE),
        in_specs=[
            pl.BlockSpec((None, TILE, d), lambda i, t: (i, t, 0)),
            pl.BlockSpec((1, d), lambda i, t: (0, 0)),
            pl.BlockSpec((2, 8, d), lambda i, t: (i, 0, 0)),
            pl.BlockSpec((d, nw), lambda i, t: (0, 0)),
            pl.BlockSpec((d, 128), lambda i, t: (0, 0)),
        ],
        out_specs=[
            pl.BlockSpec((None, TILE, nw), lambda i, t: (i, t, 0)),
            pl.BlockSpec((None, TILE, 128), lambda i, t: (i, t, 0)),
        ],
        out_shape=[
            jax.ShapeDtypeStruct((b, n, nw), BF16),
            jax.ShapeDtypeStruct((b, n, 128), F32),
        ],
        compiler_params=_params("arbitrary", "arbitrary"),
        name="gla_in",
    )(xs, g.reshape(1, d), mods, w_in.astype(BF16), w1)

    nt = n // TILE
    nct = n_ctx // TILE

    def rev_tile(s):
        return jnp.where(s < nct, nct - 1 - s, nt - 1 - (s - nct))

    qkv_w = 2 * kd + vd
    o_fwd, o_rev = pl.pallas_call(
        functools.partial(_gla_scan_kernel, dk, dv),
        grid=(b, nt),
        in_specs=[
            pl.BlockSpec((None, TILE, qkv_w), lambda i, s: (i, s, 0)),
            pl.BlockSpec((None, TILE, 128), lambda i, s: (i, s, 0)),
            pl.BlockSpec((None, TILE, qkv_w), lambda i, s: (i, rev_tile(s), 0)),
            pl.BlockSpec((None, TILE, 128), lambda i, s: (i, rev_tile(s), 0)),
            pl.BlockSpec(w2.shape, lambda i, s: (0, 0, 0)),
            pl.BlockSpec(gbias.shape, lambda i, s: (0, 0, 0)),
        ],
        out_specs=[
            pl.BlockSpec((None, TILE, vd), lambda i, s: (i, s, 0)),
            pl.BlockSpec((None, TILE, vd), lambda i, s: (i, rev_tile(s), 0)),
        ],
        out_shape=[jax.ShapeDtypeStruct((b, n, vd), F32), jax.ShapeDtypeStruct((b, n, vd), F32)],
        scratch_shapes=[pltpu.VMEM((2, GLA_HEADS, dk, dv), F32)],
        compiler_params=_params("arbitrary", "arbitrary"),
        name="gla_scan",
    )(qkvg, r, qkvg, r, w2, gbias)

    return pl.pallas_call(
        functools.partial(_gla_out_kernel, n_ctx),
        grid=(b, nt),
        in_specs=[
            pl.BlockSpec((None, TILE, d), lambda i, t: (i, t, 0)),
            pl.BlockSpec((None, TILE, vd), lambda i, t: (i, t, 0)),
            pl.BlockSpec((None, TILE, vd), lambda i, t: (i, t, 0)),
            pl.BlockSpec((None, TILE, vd), lambda i, t: (i, t, (2 * kd + vd) // vd)),
            pl.BlockSpec((1, dv), lambda i, t: (0, 0)),
            pl.BlockSpec((vd, d), lambda i, t: (0, 0)),
            pl.BlockSpec((2, 8, d), lambda i, t: (i, 0, 0)),
        ],
        out_specs=pl.BlockSpec((None, TILE, d), lambda i, t: (i, t, 0)),
        out_shape=jax.ShapeDtypeStruct(xs.shape, F32),
        input_output_aliases={0: 0},
        compiler_params=_params("arbitrary", "arbitrary"),
        name="gla_out",
    )(xs, o_fwd, o_rev, qkvg, o_norm.reshape(1, dv), w_o.astype(BF16), mods)


PEER_TT = 768
PEER_EB = 2048
PEER_SUB = 1024
PEER_RANKS = PEER_TOPK + 1


SUBLANES = 8


def _sorting_network(n):
    pairs = []
    p = 1
    while p < n:
        k = p
        while k >= 1:
            for j in range(k % p, n - k, 2 * k):
                for i in range(min(k, n - j - k)):
                    if (i + j) // (2 * p) == (i + j + k) // (2 * p):
                        pairs.append((i + j, i + j + k))
            k //= 2
        p *= 2
    return pairs


def _sublane_max_all(x):
    for shift in (4, 2, 1):
        x = jnp.maximum(x, pltpu.roll(x, shift, 0))
    return x


def _pop_top(lists, k):
    neg = jnp.full(lists[0].shape, -jnp.inf, F32)
    out = []
    for r in range(k):
        m = _sublane_max_all(lists[0])
        out.append(m)
        if r == k - 1:
            break
        hit = lists[0] == m
        keep = min(len(lists), k - 1 - r)
        lists = [jnp.where(hit, lists[p + 1] if p + 1 < len(lists) else neg, lists[p]) for p in range(keep)]
    return out


def _top_rows(s, k):
    tiles = [s[SUBLANES * v:SUBLANES * (v + 1), :] for v in range(s.shape[0] // SUBLANES)]
    for i, j in _sorting_network(len(tiles)):
        tiles[i], tiles[j] = jnp.maximum(tiles[i], tiles[j]), jnp.minimum(tiles[i], tiles[j])
    return _pop_top(tiles, k)


def _pair_threshold(a, b):
    k = PEER_RANKS
    lens = (k, k // 2, k // 3, k // 4, k - 4, k // 2 - 4, k // 3 - 4, 0)
    sub = lax.broadcasted_iota(jnp.int32, a[0].shape, 0)
    length = jnp.zeros(a[0].shape, jnp.int32)
    for c, ln in enumerate(lens):
        length = jnp.where(sub == c, ln, length)
    a_fix = jnp.where(sub == 0, a[0], jnp.where(sub == 1, a[1], jnp.where(sub == 2, a[2], a[3])))
    b_fix = jnp.where(sub == 4, b[0], jnp.where(sub == 5, b[1], b[2]))
    lists = []
    for p in range(k):
        a_p = jnp.where(sub < 4, a_fix, jnp.where(sub < 6, a[min(4 + p, k - 1)], a[4]))
        b_p = jnp.where(sub < 4, b[p], b_fix)
        lists.append(jnp.where(length > p, a_p + b_p, -jnp.inf))
    return _pop_top(lists, k)


def _peer_score_kernel(n_ctx, x_ref, g_ref, m_ref, wqt_ref, keys_ref, h_ref, e0_ref, e1_ref, th_ref, sc_ref):
    tt = x_ref.shape[0]
    is_ctx = _is_ctx(pl.program_id(1), tt, n_ctx)
    ht = _norm_mod(x_ref[...], g_ref[...], m_ref, 3, is_ctx).T.astype(BF16)
    h_ref[...] = ht
    qt = _dot(wqt_ref[...], ht).astype(BF16)
    dkey = keys_ref.shape[-1]
    for hp in range(2 * PEER_HEADS):
        sc_ref[hp] = _dot(keys_ref[hp], qt[hp * dkey:(hp + 1) * dkey, :])
    head_row = lax.broadcasted_iota(jnp.int32, th_ref.shape, 0)

    def head(hd, th_acc):
        th_parts = []
        for lt in range(tt // LANES):
            ls = slice(lt * LANES, (lt + 1) * LANES)
            s0 = sc_ref[2 * hd, :, ls]
            s1 = sc_ref[2 * hd + 1, :, ls]
            a = _top_rows(s0, PEER_RANKS)
            b = _top_rows(s1, PEER_RANKS)
            v = _pair_threshold(a, b)
            tau = 0.5 * (v[PEER_TOPK - 1] + v[PEER_TOPK])
            z = jnp.ones_like(v[0])
            for kk in range(1, PEER_TOPK):
                z = z + jnp.exp(v[kk] - v[0])
            rz = (1.0 / z)[0:1, :]
            e0_ref[hd, :, ls] = jnp.exp(s0 - a[0][0:1, :]) * rz
            e1_ref[hd, :, ls] = jnp.exp(s1 - b[0][0:1, :])
            th_parts.append(jnp.exp(tau - v[0])[0:1, :] * rz)
        return jnp.where(head_row == hd, jnp.concatenate(th_parts, axis=1), th_acc)

    th_ref[...] = lax.fori_loop(0, PEER_HEADS, head, jnp.zeros(th_ref.shape, F32))


def _gelu_tanh_sigmoid_form(x):
    c0 = -2.0 * 0.7978845608028654 * 1.4426950408889634
    c1 = c0 * 0.044715
    return x / (1.0 + jnp.exp2(x * (c0 + c1 * (x * x))))


def _peer_dense_kernel(n_ctx, sub, ht_ref, e0_ref, e1_ref, th_ref, u_ref, vt_ref, x_ref, m_ref, o_ref,
                       w_ref, acc_ref):
    e = pl.program_id(2)
    eb, tt = w_ref.shape
    nk = e1_ref.shape[1]

    @pl.when(e == 0)
    def _():
        acc_ref[...] = jnp.zeros_like(acc_ref)

    ht = ht_ref[...]
    for sb in range(eb // sub):
        a_sb = _dot(u_ref[sb * sub:(sb + 1) * sub, :], ht)
        for ii in range(sub // nk):
            i = sb * (sub // nk) + ii
            for lt in range(tt // LANES):
                ls = slice(lt * LANES, (lt + 1) * LANES)
                gsum = jnp.zeros((nk, LANES), F32)
                for hd in range(PEER_HEADS):
                    p = e0_ref[hd, i:i + 1, ls] * e1_ref[hd, :, ls]
                    gsum = gsum + jnp.where(p > th_ref[hd:hd + 1, ls], p, 0.0)
                act = _gelu_tanh_sigmoid_form(a_sb[ii * nk:(ii + 1) * nk, ls])
                w_ref[i * nk:(i + 1) * nk, ls] = (act * gsum).astype(BF16)
    acc_ref[...] += _dot(vt_ref[...], w_ref[...])

    @pl.when(e == pl.num_programs(2) - 1)
    def _():
        is_ctx = _is_ctx(pl.program_id(1), tt, n_ctx)
        o_ref[...] = x_ref[...] + _mod_row(m_ref, 5, is_ctx) * acc_ref[...].T


def peer_layer(xs, mods, g, n_ctx, w_q, keys, u_tab, v_tab, eb=PEER_EB, sub=PEER_SUB):
    b, n, d = xs.shape
    tt = PEER_TT
    ntt = n // tt
    ne = u_tab.shape[0]
    nk = PEER_NKEYS
    hq = w_q.shape[-1]
    wqt = jnp.transpose(w_q).astype(BF16)
    keys_b = keys.reshape(PEER_HEADS * 2, nk, keys.shape[-1]).astype(BF16)
    u_b = u_tab.astype(BF16)

    h2t, e0, e1, th = pl.pallas_call(
        functools.partial(_peer_score_kernel, n_ctx),
        grid=(b, ntt),
        in_specs=[
            pl.BlockSpec((None, tt, d), lambda i, t: (i, t, 0)),
            pl.BlockSpec((1, d), lambda i, t: (0, 0)),
            pl.BlockSpec((2, 8, d), lambda i, t: (i, 0, 0)),
            pl.BlockSpec((hq, d), lambda i, t: (0, 0)),
            pl.BlockSpec(keys_b.shape, lambda i, t: (0, 0, 0)),
        ],
        out_specs=[
            pl.BlockSpec((d, tt), lambda i, t: (0, i * ntt + t)),
            pl.BlockSpec((PEER_HEADS, nk, tt), lambda i, t: (0, 0, i * ntt + t)),
            pl.BlockSpec((PEER_HEADS, nk, tt), lambda i, t: (0, 0, i * ntt + t)),
            pl.BlockSpec((PEER_HEADS, tt), lambda i, t: (0, i * ntt + t)),
        ],
        out_shape=[
            jax.ShapeDtypeStruct((d, b * n), BF16),
            jax.ShapeDtypeStruct((PEER_HEADS, nk, b * n), F32),
            jax.ShapeDtypeStruct((PEER_HEADS, nk, b * n), F32),
            jax.ShapeDtypeStruct((PEER_HEADS, b * n), F32),
        ],
        scratch_shapes=[pltpu.VMEM((2 * PEER_HEADS, nk, tt), F32)],
        compiler_params=_params("arbitrary", "arbitrary"),
        name="peer_score",
    )(xs, g.reshape(1, d), mods, wqt, keys_b)

    nblk = ne // eb
    vt_b = jnp.transpose(v_tab.reshape(nblk, eb, d), (0, 2, 1)).astype(BF16)
    return pl.pallas_call(
        functools.partial(_peer_dense_kernel, n_ctx, sub),
        grid=(b, ntt, nblk),
        in_specs=[
            pl.BlockSpec((d, tt), lambda i, t, e: (0, i * ntt + t)),
            pl.BlockSpec((PEER_HEADS, eb // nk, tt), lambda i, t, e: (0, e, i * ntt + t)),
            pl.BlockSpec((PEER_HEADS, nk, tt), lambda i, t, e: (0, 0, i * ntt + t)),
            pl.BlockSpec((PEER_HEADS, tt), lambda i, t, e: (0, i * ntt + t)),
            pl.BlockSpec((eb, d), lambda i, t, e: (e, 0)),
            pl.BlockSpec((None, d, eb), lambda i, t, e: (e, 0, 0)),
            pl.BlockSpec((None, tt, d), lambda i, t, e: (i, t, 0)),
            pl.BlockSpec((2, 8, d), lambda i, t, e: (i, 0, 0)),
        ],
        out_specs=pl.BlockSpec((None, tt, d), lambda i, t, e: (i, t, 0)),
        out_shape=jax.ShapeDtypeStruct(xs.shape, F32),
        scratch_shapes=[
            pltpu.VMEM((eb, tt), BF16),
            pltpu.VMEM((d, tt), F32),
        ],
        input_output_aliases={6: 0},
        compiler_params=_params("arbitrary", "arbitrary", "arbitrary"),
        name="peer_dense",
    )(h2t, e0, e1, th, u_b, vt_b, xs, mods)


def _final_kernel(x_ref, g_ref, o_ref):
    o_ref[...] = _rms(x_ref[...], g_ref[...])


def final_norm(xs, g, n_ctx):
    b, n, d = xs.shape
    off = n_ctx // TILE
    return pl.pallas_call(
        _final_kernel,
        grid=(b, (n - n_ctx) // TILE),
        in_specs=[
            pl.BlockSpec((None, TILE, d), lambda i, t: (i, t + off, 0)),
            pl.BlockSpec((1, d), lambda i, t: (0, 0)),
        ],
        out_specs=pl.BlockSpec((None, TILE, d), lambda i, t: (i, t, 0)),
        out_shape=jax.ShapeDtypeStruct((b, n - n_ctx, d), F32),
        compiler_params=_params("arbitrary", "arbitrary"),
        name="final_norm",
    )(xs, g.reshape(1, d))


def kernel(x, c, ctx, c_ctx, norm_g, ada_w, ada_b, mla_w_in, mla_q_norm, mla_w_uq, mla_kv_norm, mla_w_ukv, mla_w_o, s5_lam_re, s5_lam_im, s5_b_re, s5_b_im, s5_c_re, s5_c_im, s5_log_step, s5_d, s5_w_glu, gla_w_in, gla_gk_w1, gla_gk_w2, gla_gk_b, gla_o_norm, gla_w_o, peer_w_q, peer_keys, peer_u, peer_v, final_g):
    b, n_lat, d = x.shape
    n_ctx = ctx.shape[1]
    depth = ada_w.shape[0]
    xs = jnp.concatenate([ctx, x], axis=1)

    r = 8 * ((b + 1 + 7) // 8)
    cond = jnp.zeros((r, d), F32).at[:b].set(c).at[b].set(c_ctx)
    ada = ada_all(cond, ada_w, ada_b).reshape(depth, r, 6, d)
    lat = ada[:, :b]
    ctxp = jnp.broadcast_to(ada[:, b:b + 1], lat.shape)
    mods_all = jnp.stack([ctxp, lat], axis=2)
    mods_all = jnp.pad(mods_all, ((0, 0), (0, 0), (0, 0), (0, 2), (0, 0))).reshape(depth, b * 2, 8, d)

    for i in range(depth):
        kind, j = i % N_MIXERS, i // N_MIXERS
        mods = mods_all[i]
        if kind == 0:
            xs = mixer_mla(xs, mods, norm_g[i, 0], n_ctx, i < depth - 1, mla_w_in[j], mla_q_norm[j], mla_w_uq[j],
                           mla_kv_norm[j], mla_w_ukv[j], mla_w_o[j])
        elif kind == 1:
            xs = mixer_s5(xs, mods, norm_g[i, 0], n_ctx, s5_lam_re[j], s5_lam_im[j], s5_b_re[j], s5_b_im[j],
                          s5_c_re[j], s5_c_im[j], s5_log_step[j], s5_d[j], s5_w_glu[j])
        else:
            xs = mixer_gla(xs, mods, norm_g[i, 0], n_ctx, gla_w_in[j], gla_gk_w1[j], gla_gk_w2[j],
                           gla_gk_b[j], gla_o_norm[j], gla_w_o[j])
        xs = peer_layer(xs, mods, norm_g[i, 1], n_ctx, peer_w_q[i], peer_keys[i], peer_u[i], peer_v[i])
    return final_norm(xs, final_g, n_ctx)
```

```python
import functools
import math

import numpy as np
import jax
import jax.numpy as jnp
from jax import lax
from jax.experimental import pallas as pl
from jax.experimental.pallas import tpu as pltpu

F32 = jnp.float32
BF16 = jnp.bfloat16

EPS = 1e-6
GRID_W = 64
N_MIXERS = 3

MLA_HEADS = 16
MLA_NOPE = 64
MLA_ROPE = 32
MLA_V = 64
MLA_Q_LORA = 384
MLA_KV_LORA = 256
ROPE_BASE = 10000.0
MLA_HEAD_PAD = 128
MLA_HEADS_PER_STEP = 2
MLA_Q_ROWS = 1024

S5_GROUP = 16
S5_STATE = 64
S5_GROUPS_PER_BLOCK = 8
S5_CHUNK = 256

GLA_HEADS = 4
GLA_GATE_RANK = 16
GLA_GATE_NORM = 16.0
GLA_CHUNK = 64

PEER_HEADS = 8
PEER_NKEYS = 128
PEER_TOPK = 16

LANES = 128
TILE = 256
VMEM_LIMIT = 56 * 1024 * 1024


def _params(*sem):
    return pltpu.CompilerParams(dimension_semantics=sem, vmem_limit_bytes=VMEM_LIMIT)


def _gelu_tanh(x):
    return 0.5 * x * (1.0 + jnp.tanh(0.7978845608028654 * (x + 0.044715 * x * x * x)))


def _is_ctx(tile_idx, rows, n_ctx):
    pos = tile_idx * rows + lax.broadcasted_iota(jnp.int32, (rows, 1), 0)
    return pos < n_ctx


def _mod_row(m_ref, row, is_ctx):
    return jnp.where(is_ctx, m_ref[0, row:row + 1, :], m_ref[1, row:row + 1, :])


def _rms(x, g):
    return x * lax.rsqrt(jnp.mean(x * x, axis=-1, keepdims=True) + EPS) * g


def _norm_mod(x, g, m_ref, row0, is_ctx):
    return _rms(x, g) * (1.0 + _mod_row(m_ref, row0 + 1, is_ctx)) + _mod_row(m_ref, row0, is_ctx)


def _dot(a, b):
    return jnp.dot(a, b, preferred_element_type=F32)


def _dot_nt(a, b):
    return lax.dot_general(a, b, (((1,), (1,)), ((), ())), preferred_element_type=F32)


def _ada_kernel(c_ref, w_ref, b_ref, o_ref):
    c = c_ref[...]
    s = (c * jax.nn.sigmoid(c)).astype(BF16)
    o_ref[...] = _dot(s, w_ref[...].astype(BF16)) + b_ref[...]


def ada_all(cond, ada_w, ada_b):
    depth, d, n6 = ada_w.shape
    r = cond.shape[0]
    tn = 1024
    return pl.pallas_call(
        _ada_kernel,
        grid=(depth, n6 // tn),
        in_specs=[
            pl.BlockSpec((r, d), lambda l, j: (0, 0)),
            pl.BlockSpec((None, d, tn), lambda l, j: (l, 0, j)),
            pl.BlockSpec((None, 1, tn), lambda l, j: (l, 0, j)),
        ],
        out_specs=pl.BlockSpec((None, r, tn), lambda l, j: (l, 0, j)),
        out_shape=jax.ShapeDtypeStruct((depth, r, n6), F32),
        compiler_params=_params("arbitrary", "arbitrary"),
        name="ada",
    )(cond, ada_w, ada_b.reshape(depth, 1, n6))


def _proj_residual_kernel(n_ctx, gate_row, tile_off, x_ref, y_ref, w_ref, m_ref, o_ref):
    is_ctx = _is_ctx(pl.program_id(1) + tile_off, x_ref.shape[0], n_ctx)
    f = _dot(y_ref[...], w_ref[...])
    o_ref[...] = x_ref[...] + _mod_row(m_ref, gate_row, is_ctx) * f


def proj_residual(xs, y, w, mods, n_ctx, gate_row, tile_off=0):
    b, n, d = xs.shape
    k = y.shape[-1]
    return pl.pallas_call(
        functools.partial(_proj_residual_kernel, n_ctx, gate_row, tile_off),
        grid=(b, y.shape[1] // TILE),
        in_specs=[
            pl.BlockSpec((None, TILE, d), lambda i, t: (i, t + tile_off, 0)),
            pl.BlockSpec((None, TILE, k), lambda i, t: (i, t, 0)),
            pl.BlockSpec((k, d), lambda i, t: (0, 0)),
            pl.BlockSpec((2, 8, d), lambda i, t: (i, 0, 0)),
        ],
        out_specs=pl.BlockSpec((None, TILE, d), lambda i, t: (i, t + tile_off, 0)),
        out_shape=jax.ShapeDtypeStruct(xs.shape, F32),
        input_output_aliases={0: 0},
        compiler_params=_params("arbitrary", "arbitrary"),
        name="proj_residual",
    )(xs, y, w, mods)


def _rope_tables(n_ctx, n_lat, scale):
    half = MLA_ROPE // 2
    rows_n = n_lat // GRID_W
    rows = np.repeat(np.arange(rows_n, dtype=np.float32), GRID_W)
    cols = np.tile(np.arange(GRID_W, dtype=np.float32), rows_n)
    inv = (ROPE_BASE ** (-np.arange(0, half, 2, dtype=np.float32) / half)).astype(np.float32)
    ang_r = rows[:, None] * inv
    ang_c = cols[:, None] * inv
    cos = np.concatenate([np.cos(ang_r), np.cos(ang_r), np.cos(ang_c), np.cos(ang_c)], axis=1)
    sin = np.concatenate([-np.sin(ang_r), np.sin(ang_r), -np.sin(ang_c), np.sin(ang_c)], axis=1)
    n = n_ctx + n_lat
    a = np.zeros((n, MLA_HEAD_PAD), np.float32)
    b = np.zeros((n, MLA_HEAD_PAD), np.float32)
    a[:, :MLA_NOPE] = 1.0
    a[:n_ctx, MLA_NOPE:MLA_NOPE + MLA_ROPE] = 1.0
    a[n_ctx:, MLA_NOPE:MLA_NOPE + MLA_ROPE] = cos
    b[n_ctx:, MLA_NOPE:MLA_NOPE + MLA_ROPE] = sin
    return jnp.asarray(a * scale), jnp.asarray(b * scale)


def _rope_swap_index():
    q = MLA_ROPE // 4
    base = np.arange(MLA_ROPE)
    return np.where((base % (2 * q)) < q, base + q, base - q)


def _mla_in_kernel(n_ctx, x_ref, g_ref, m_ref, wq_ref, wkv_ref, wkp_ref, qn_ref, kvn_ref,
                   wuq1_ref, wuq2_ref, wuk_ref, wuv_ref, aq_ref, bq_ref, ak_ref, bk_ref,
                   qc_ref, ql_ref, k_ref, v_ref):
    is_ctx = _is_ctx(pl.program_id(1), x_ref.shape[0], n_ctx)
    hb = _norm_mod(x_ref[...], g_ref[...], m_ref, 0, is_ctx).astype(BF16)
    cq = _rms(_dot(hb, wq_ref[...]), qn_ref[...]).astype(BF16)
    ckv = _rms(_dot(hb, wkv_ref[...]), kvn_ref[...]).astype(BF16)
    kp = _dot(hb, wkp_ref[...])
    kpe = kp[:, :MLA_HEAD_PAD] * ak_ref[...] + kp[:, MLA_HEAD_PAD:] * bk_ref[...]
    y1 = _dot(cq, wuq1_ref[...])
    y2 = _dot(cq, wuq2_ref[...])
    kk = _dot(ckv, wuk_ref[...])
    aq = aq_ref[...]
    bq = bq_ref[...]
    qs = []
    for h in range(MLA_HEADS):
        sl = slice(h * MLA_HEAD_PAD, (h + 1) * MLA_HEAD_PAD)
        qs.append((y1[:, sl] * aq + y2[:, sl] * bq).astype(BF16))
        k_ref[:, sl] = (kk[:, sl] + kpe).astype(BF16)
    v_ref[...] = _dot(ckv, wuv_ref[...]).astype(BF16)
    in_ctx = pl.program_id(1) * x_ref.shape[0] < n_ctx

    @pl.when(in_ctx)
    def _():
        for h in range(MLA_HEADS):
            qc_ref[:, h * MLA_HEAD_PAD:(h + 1) * MLA_HEAD_PAD] = qs[h]

    @pl.when(jnp.logical_not(in_ctx))
    def _():
        for h in range(MLA_HEADS):
            ql_ref[:, h * MLA_HEAD_PAD:(h + 1) * MLA_HEAD_PAD] = qs[h]


def _mla_attn_kernel(q_ref, k_ref, v_ref, o_ref):
    heads = range(q_ref.shape[-1] // MLA_HEAD_PAD)
    ss = [_dot_nt(q_ref[:, h * MLA_HEAD_PAD:(h + 1) * MLA_HEAD_PAD],
                  k_ref[:, h * MLA_HEAD_PAD:(h + 1) * MLA_HEAD_PAD]) for h in heads]
    ps = [jnp.exp(s - jnp.max(s, axis=-1, keepdims=True)) for s in ss]
    ls = [jnp.sum(p, axis=-1, keepdims=True) for p in ps]
    outs = [_dot(ps[h].astype(BF16), v_ref[:, (h // 2) * 2 * MLA_V:(h // 2 + 1) * 2 * MLA_V]) / ls[h]
            for h in heads]
    lane = lax.broadcasted_iota(jnp.int32, outs[0].shape, 1)
    for pr in range(len(heads) // 2):
        o_ref[:, pr * 2 * MLA_V:(pr + 1) * 2 * MLA_V] = jnp.where(
            lane < MLA_V, outs[2 * pr], outs[2 * pr + 1]).astype(BF16)


def _mla_attention(q, k, v, n_keys, q_rows, hs):
    b, nq, _ = q.shape
    hp = MLA_HEAD_PAD
    return pl.pallas_call(
        _mla_attn_kernel,
        grid=(b, MLA_HEADS // hs, nq // q_rows),
        in_specs=[
            pl.BlockSpec((None, q_rows, hs * hp), lambda i, h, t: (i, t, h)),
            pl.BlockSpec((None, n_keys, hs * hp), lambda i, h, t: (i, 0, h)),
            pl.BlockSpec((None, n_keys, hs * MLA_V), lambda i, h, t: (i, 0, h)),
        ],
        out_specs=pl.BlockSpec((None, q_rows, hs * MLA_V), lambda i, h, t: (i, t, h)),
        out_shape=jax.ShapeDtypeStruct((b, nq, MLA_HEADS * MLA_V), BF16),
        compiler_params=_params("arbitrary", "arbitrary", "arbitrary"),
        name="mla_attn",
    )(q, k, v)


def mixer_mla(xs, mods, g, n_ctx, ctx_out, w_in, q_norm, w_uq, kv_norm, w_ukv, w_o):
    b, n, d = xs.shape
    hp = MLA_HEAD_PAD
    dq = MLA_NOPE + MLA_ROPE
    wq = w_in[:, :MLA_Q_LORA].astype(BF16)
    wkv = w_in[:, MLA_Q_LORA:MLA_Q_LORA + MLA_KV_LORA].astype(BF16)
    w_pe = w_in[:, MLA_Q_LORA + MLA_KV_LORA:]
    swap = _rope_swap_index()
    wkp = jnp.zeros((d, 2 * hp), F32)
    wkp = wkp.at[:, MLA_NOPE:dq].set(w_pe).at[:, hp + MLA_NOPE:hp + dq].set(w_pe[:, swap]).astype(BF16)
    uq = w_uq.reshape(MLA_Q_LORA, MLA_HEADS, dq)
    z = jnp.zeros((MLA_Q_LORA, MLA_HEADS, hp - dq), F32)
    wuq1 = jnp.concatenate([uq, z], axis=-1).reshape(MLA_Q_LORA, MLA_HEADS * hp).astype(BF16)
    zn = jnp.zeros((MLA_Q_LORA, MLA_HEADS, MLA_NOPE), F32)
    wuq2 = jnp.concatenate([zn, uq[:, :, MLA_NOPE:][:, :, swap], z], axis=-1)
    wuq2 = wuq2.reshape(MLA_Q_LORA, MLA_HEADS * hp).astype(BF16)
    ukv = w_ukv.reshape(MLA_KV_LORA, MLA_HEADS, MLA_NOPE + MLA_V)
    zk = jnp.zeros((MLA_KV_LORA, MLA_HEADS, hp - MLA_NOPE), F32)
    wuk = jnp.concatenate([ukv[:, :, :MLA_NOPE], zk], axis=-1).reshape(MLA_KV_LORA, MLA_HEADS * hp).astype(BF16)
    wuv = ukv[:, :, MLA_NOPE:].reshape(MLA_KV_LORA, MLA_HEADS * MLA_V).astype(BF16)
    aq, bq = _rope_tables(n_ctx, n - n_ctx, float(dq) ** -0.5)
    ak, bk = _rope_tables(n_ctx, n - n_ctx, 1.0)

    full = lambda shape: pl.BlockSpec(shape, lambda i, t: tuple(0 for _ in shape))
    tab = pl.BlockSpec((TILE, hp), lambda i, t: (t, 0))
    nct = n_ctx // TILE
    q_ctx, q_lat, k, v = pl.pallas_call(
        functools.partial(_mla_in_kernel, n_ctx),
        grid=(b, n // TILE),
        in_specs=[
            pl.BlockSpec((None, TILE, d), lambda i, t: (i, t, 0)),
            full((1, d)),
            pl.BlockSpec((2, 8, d), lambda i, t: (i, 0, 0)),
            full(wq.shape), full(wkv.shape), full(wkp.shape),
            full((1, MLA_Q_LORA)), full((1, MLA_KV_LORA)),
            full(wuq1.shape), full(wuq2.shape), full(wuk.shape), full(wuv.shape),
            tab, tab, tab, tab,
        ],
        out_specs=[
            pl.BlockSpec((None, TILE, MLA_HEADS * hp), lambda i, t: (i, jnp.minimum(t, nct - 1), 0)),
            pl.BlockSpec((None, TILE, MLA_HEADS * hp), lambda i, t: (i, jnp.maximum(t - nct, 0), 0)),
            pl.BlockSpec((None, TILE, MLA_HEADS * hp), lambda i, t: (i, t, 0)),
            pl.BlockSpec((None, TILE, MLA_HEADS * MLA_V), lambda i, t: (i, t, 0)),
        ],
        out_shape=[
            jax.ShapeDtypeStruct((b, n_ctx, MLA_HEADS * hp), BF16),
            jax.ShapeDtypeStruct((b, n - n_ctx, MLA_HEADS * hp), BF16),
            jax.ShapeDtypeStruct((b, n, MLA_HEADS * hp), BF16),
            jax.ShapeDtypeStruct((b, n, MLA_HEADS * MLA_V), BF16),
        ],
        compiler_params=_params("arbitrary", "arbitrary"),
        name="mla_in",
    )(xs, g.reshape(1, d), mods, wq, wkv, wkp, q_norm.reshape(1, -1), kv_norm.reshape(1, -1),
      wuq1, wuq2, wuk, wuv, aq, bq, ak, bk)

    wo = w_o.astype(BF16)
    o_lat = _mla_attention(q_lat, k, v, n, min(MLA_Q_ROWS, n - n_ctx), MLA_HEADS_PER_STEP)
    xs = proj_residual(xs, o_lat, wo, mods, n_ctx, 2, tile_off=nct)
    if ctx_out:
        o_ctx = _mla_attention(q_ctx, k, v, n_ctx, n_ctx, MLA_HEADS_PER_STEP)
        xs = proj_residual(xs, o_ctx, wo, mods, n_ctx, 2, tile_off=0)
    return xs


def _s5_disc_kernel(lr_ref, li_ref, ls_ref, bre_ref, bim_ref, ar_ref, ai_ref, br_ref, bi_ref):
    lr = lr_ref[...]
    li = li_ref[...]
    dt = jnp.exp(ls_ref[...])
    mag = jnp.exp(lr * dt)
    ar = mag * jnp.cos(li * dt)
    ai = mag * jnp.sin(li * dt)
    den = lr * lr + li * li
    fr = ((ar - 1.0) * lr + ai * li) / den
    fi = (ai * lr - (ar - 1.0) * li) / den
    ar_ref[...] = ar
    ai_ref[...] = ai
    for c in range(S5_GROUP):
        br_ref[c] = fr * bre_ref[c] - fi * bim_ref[c]
        bi_ref[c] = fr * bim_ref[c] + fi * bre_ref[c]


def _s5_in_kernel(n_ctx, x_ref, g_ref, m_ref, u_ref):
    is_ctx = _is_ctx(pl.program_id(1), x_ref.shape[0], n_ctx)
    u_ref[...] = _norm_mod(x_ref[...], g_ref[...], m_ref, 0, is_ctx).astype(BF16)


def _s5_scan_kernel(nb, u_ref, bcat_ref, ar_ref, ai_ref, ccat_ref, y_ref, bu_ref, xs_ref, st_ref):
    d = pl.program_id(0)
    half = ar_ref.shape[-1]

    @pl.when(pl.program_id(2) == 0)
    def _():
        st_ref[...] = jnp.zeros_like(st_ref)

    bu_ref[...] = _dot(u_ref[...], bcat_ref[...])
    ar = jnp.broadcast_to(ar_ref[...], (nb, half))
    ai = jnp.broadcast_to(ai_ref[...], (nb, half))
    steps = u_ref.shape[0] // nb

    def step(i, carry):
        xr, xi = carry
        tt = jnp.where(d == 0, i, steps - 1 - i)
        r0 = pl.multiple_of(tt * nb, nb)
        nxr = ar * xr - ai * xi + bu_ref[pl.ds(r0, nb), 0:half]
        nxi = ar * xi + ai * xr + bu_ref[pl.ds(r0, nb), half:2 * half]
        xs_ref[pl.ds(r0, nb), 0:half] = nxr.astype(BF16)
        xs_ref[pl.ds(r0, nb), half:2 * half] = nxi.astype(BF16)
        return nxr, nxi

    xr, xi = lax.fori_loop(0, steps, step, (st_ref[:, 0:half], st_ref[:, half:2 * half]), unroll=8)
    st_ref[:, 0:half] = xr
    st_ref[:, half:2 * half] = xi
    y_ref[...] = _dot(xs_ref[...], ccat_ref[...])


def _s5_glu_kernel(n_ctx, x_ref, g_ref, m_ref, y_ref, dsk_ref, w_ref, o_ref):
    is_ctx = _is_ctx(pl.program_id(1), x_ref.shape[0], n_ctx)
    x = x_ref[...]
    u = _norm_mod(x, g_ref[...], m_ref, 0, is_ctx)
    y = y_ref[0] + y_ref[1]
    z = _dot(_gelu_tanh(y + dsk_ref[...] * u).astype(BF16), w_ref[...])
    dm = z.shape[-1] // 2
    out = z[:, :dm] * jax.nn.sigmoid(z[:, dm:])
    o_ref[...] = x + _mod_row(m_ref, 2, is_ctx) * out


def mixer_s5(xs, mods, g, n_ctx, lam_re, lam_im, b_re, b_im, c_re, c_im, log_step, d_skip, w_glu):
    b, n, d = xs.shape
    groups = d // S5_GROUP
    p = S5_STATE
    gb = S5_GROUPS_PER_BLOCK
    nblk = groups // gb
    cin = gb * S5_GROUP
    half = gb * p

    full = lambda shape: pl.BlockSpec(shape, lambda dd: tuple(0 for _ in shape))
    ar, ai, br, bi = pl.pallas_call(
        _s5_disc_kernel,
        grid=(2,),
        in_specs=[
            pl.BlockSpec((None, groups, p), lambda dd: (dd, 0, 0)),
            pl.BlockSpec((None, groups, p), lambda dd: (dd, 0, 0)),
            pl.BlockSpec((None, groups, 1), lambda dd: (dd, 0, 0)),
            pl.BlockSpec((None, S5_GROUP, groups, p), lambda dd: (dd, 0, 0, 0)),
            pl.BlockSpec((None, S5_GROUP, groups, p), lambda dd: (dd, 0, 0, 0)),
        ],
        out_specs=[
            pl.BlockSpec((None, groups, p), lambda dd: (dd, 0, 0)),
            pl.BlockSpec((None, groups, p), lambda dd: (dd, 0, 0)),
            pl.BlockSpec((None, S5_GROUP, groups, p), lambda dd: (dd, 0, 0, 0)),
            pl.BlockSpec((None, S5_GROUP, groups, p), lambda dd: (dd, 0, 0, 0)),
        ],
        out_shape=[
            jax.ShapeDtypeStruct((2, groups, p), F32),
            jax.ShapeDtypeStruct((2, groups, p), F32),
            jax.ShapeDtypeStruct((2, S5_GROUP, groups, p), F32),
            jax.ShapeDtypeStruct((2, S5_GROUP, groups, p), F32),
        ],
        compiler_params=_params("arbitrary"),
        name="s5_disc",
    )(lam_re, lam_im, log_step.reshape(2, groups, 1),
      jnp.transpose(b_re, (0, 3, 1, 2)), jnp.transpose(b_im, (0, 3, 1, 2)))

    eye = jnp.eye(gb, dtype=F32)

    def in_blocks(t):
        t = t.reshape(2, S5_GROUP, nblk, gb, p)
        return jnp.einsum('ab,dcjap->djacbp', eye, t).reshape(2, nblk, cin, half)

    bcat = jnp.concatenate([in_blocks(br), in_blocks(bi)], axis=-1).astype(BF16)

    def out_blocks(t):
        t = t.reshape(2, nblk, gb, S5_GROUP, p)
        return jnp.einsum('ab,djacp->djapbc', eye, t).reshape(2, nblk, half, cin)

    ccat = jnp.concatenate([out_blocks(c_re), -out_blocks(c_im)], axis=2).astype(BF16)
    ar_b = ar.reshape(2, nblk, 1, half)
    ai_b = ai.reshape(2, nblk, 1, half)

    u_tm = pl.pallas_call(
        functools.partial(_s5_in_kernel, n_ctx),
        grid=(b, n // TILE),
        in_specs=[
            pl.BlockSpec((None, TILE, d), lambda i, t: (i, t, 0)),
            pl.BlockSpec((1, d), lambda i, t: (0, 0)),
            pl.BlockSpec((2, 8, d), lambda i, t: (i, 0, 0)),
        ],
        out_specs=pl.BlockSpec((TILE, d), lambda i, t: (t, i)),
        out_shape=jax.ShapeDtypeStruct((n, b * d), BF16),
        compiler_params=_params("arbitrary", "arbitrary"),
        name="s5_in",
    )(xs, g.reshape(1, d), mods)

    tc = S5_CHUNK
    nchunks = n // tc
    ncc = n_ctx // tc
    rows = tc * b

    def chunk_of(dd, s):
        rev = jnp.where(s < ncc, ncc - 1 - s, nchunks - 1 - (s - ncc))
        return jnp.where(dd == 0, s, rev)

    y2 = pl.pallas_call(
        functools.partial(_s5_scan_kernel, b),
        grid=(2, nblk, nchunks),
        in_specs=[
            pl.BlockSpec((rows, cin), lambda dd, j, s: (chunk_of(dd, s), j)),
            pl.BlockSpec((None, None, cin, 2 * half), lambda dd, j, s: (dd, j, 0, 0)),
            pl.BlockSpec((None, None, 1, half), lambda dd, j, s: (dd, j, 0, 0)),
            pl.BlockSpec((None, None, 1, half), lambda dd, j, s: (dd, j, 0, 0)),
            pl.BlockSpec((None, None, 2 * half, cin), lambda dd, j, s: (dd, j, 0, 0)),
        ],
        out_specs=pl.BlockSpec((None, rows, cin), lambda dd, j, s: (dd, chunk_of(dd, s), j)),
        out_shape=jax.ShapeDtypeStruct((2, n * b, d), F32),
        scratch_shapes=[
            pltpu.VMEM((rows, 2 * half), F32),
            pltpu.VMEM((rows, 2 * half), BF16),
            pltpu.VMEM((b, 2 * half), F32),
        ],
        compiler_params=_params("arbitrary", "arbitrary", "arbitrary"),
        name="s5_scan",
    )(u_tm.reshape(n * b, d), bcat, ar_b, ai_b, ccat)

    return pl.pallas_call(
        functools.partial(_s5_glu_kernel, n_ctx),
        grid=(b, n // TILE),
        in_specs=[
            pl.BlockSpec((None, TILE, d), lambda i, t: (i, t, 0)),
            pl.BlockSpec((1, d), lambda i, t: (0, 0)),
            pl.BlockSpec((2, 8, d), lambda i, t: (i, 0, 0)),
            pl.BlockSpec((2, TILE, d), lambda i, t: (0, t, i)),
            pl.BlockSpec((1, d), lambda i, t: (0, 0)),
            pl.BlockSpec((d, 2 * d), lambda i, t: (0, 0)),
        ],
        out_specs=pl.BlockSpec((None, TILE, d), lambda i, t: (i, t, 0)),
        out_shape=jax.ShapeDtypeStruct(xs.shape, F32),
        input_output_aliases={0: 0},
        compiler_params=_params("arbitrary", "arbitrary"),
        name="s5_glu",
    )(xs, g.reshape(1, d), mods, y2.reshape(2, n, b * d), d_skip.reshape(1, d), w_glu.astype(BF16))


def _gla_in_kernel(n_ctx, x_ref, g_ref, m_ref, w_ref, w1_ref, o_ref, r_ref):
    is_ctx = _is_ctx(pl.program_id(1), x_ref.shape[0], n_ctx)
    hb = _norm_mod(x_ref[...], g_ref[...], m_ref, 0, is_ctx).astype(BF16)
    o_ref[...] = _dot(hb, w_ref[...]).astype(BF16)
    r_ref[...] = _dot(hb, w1_ref[...])


def _gla_scan_kernel(dk, dv, qf_ref, rf_ref, qr_ref, rr_ref, w2_ref, gb_ref, of_ref, or_ref, s_ref):
    c = GLA_CHUNK
    nck = qf_ref.shape[0] // c
    kd = GLA_HEADS * dk

    @pl.when(pl.program_id(1) == 0)
    def _():
        s_ref[...] = jnp.zeros_like(s_ref)

    row = lax.broadcasted_iota(jnp.int32, (c, c), 0)
    col = lax.broadcasted_iota(jnp.int32, (c, c), 1)
    masks = (row >= col, row <= col)

    refs = ((qf_ref, rf_ref, of_ref), (qr_ref, rr_ref, or_ref))
    chains = [(dd, h) for dd in range(2) for h in range(GLA_HEADS)]
    for i in range(nck):
        rows = [slice(i * c, (i + 1) * c), slice((nck - 1 - i) * c, (nck - i) * c)]
        decay = []
        for dd in range(2):
            rb = refs[dd][1][rows[dd], :].astype(BF16)
            lg = jax.nn.log_sigmoid(_dot(rb, w2_ref[dd]) + gb_ref[dd]) * (1.0 / GLA_GATE_NORM)
            bcum = jnp.dot(masks[dd].astype(F32), lg, preferred_element_type=F32,
                           precision=lax.Precision.HIGHEST)
            blast = jnp.sum(lg, axis=0, keepdims=True)
            decay.append((jnp.exp(bcum), jnp.exp(-bcum), jnp.exp(blast - bcum), jnp.exp(blast)))
        qg, att, kdec, vv, st = {}, {}, {}, {}, {}
        for ch in chains:
            dd, h = ch
            x_ref = refs[dd][0]
            hs = slice(h * dk, (h + 1) * dk)
            q = x_ref[rows[dd], hs].astype(F32) * (float(dk) ** -0.5)
            k = x_ref[rows[dd], kd + h * dk:kd + (h + 1) * dk].astype(F32)
            vv[ch] = x_ref[rows[dd], 2 * kd + h * dv:2 * kd + (h + 1) * dv]
            qg[ch] = (q * decay[dd][0][:, hs]).astype(BF16)
            kg = (k * decay[dd][1][:, hs]).astype(BF16)
            kdec[ch] = k * decay[dd][2][:, hs]
            att[ch] = jnp.where(masks[dd], _dot_nt(qg[ch], kg), 0.0).astype(BF16)
        for ch in chains:
            dd, h = ch
            st[ch] = s_ref[dd, h]
            refs[dd][2][rows[dd], h * dv:(h + 1) * dv] = (
                _dot(att[ch], vv[ch]) + _dot(qg[ch], st[ch].astype(BF16)))
        for ch in chains:
            dd, h = ch
            eb = jnp.broadcast_to(decay[dd][3][:, h * dk:(h + 1) * dk], (dk - c, dk))
            mt = jnp.concatenate([kdec[ch], eb], axis=0).T
            s_ref[dd, h] = mt[:, c:c + 1] * st[ch] + _dot(mt[:, 0:c].astype(BF16), vv[ch])


def _gla_out_kernel(n_ctx, x_ref, of_ref, or_ref, gt_ref, on_ref, w_ref, m_ref, o_ref):
    is_ctx = _is_ctx(pl.program_id(1), x_ref.shape[0], n_ctx)
    o = of_ref[...] + or_ref[...]
    gt = gt_ref[...].astype(F32)
    dv = on_ref.shape[-1]
    parts = []
    for h in range(GLA_HEADS):
        sl = slice(h * dv, (h + 1) * dv)
        parts.append(_rms(o[:, sl], on_ref[...]) * (gt[:, sl] * jax.nn.sigmoid(gt[:, sl])))
    y = jnp.concatenate(parts, axis=-1).astype(BF16)
    o_ref[...] = x_ref[...] + _mod_row(m_ref, 2, is_ctx) * _dot(y, w_ref[...])


def mixer_gla(xs, mods, g, n_ctx, w_in, gk_w1, gk_w2, gk_b, o_norm, w_o):
    b, n, d = xs.shape
    kd = d // 2
    vd = d
    dk = kd // GLA_HEADS
    dv = vd // GLA_HEADS
    rk = GLA_GATE_RANK
    w1 = jnp.zeros((d, 128), F32).at[:, :rk].set(gk_w1[0]).at[:, rk:2 * rk].set(gk_w1[1]).astype(BF16)
    w2 = jnp.zeros((2, 128, kd), F32)
    for dd in range(2):
        w2 = w2.at[dd, dd * rk:(dd + 1) * rk, :].set(gk_w2[dd])
    w2 = w2.astype(BF16)
    gbias = gk_b.reshape(2, 1, kd)
    nw = w_in.shape[-1]

    qkvg, r = pl.pallas_call(
        functools.partial(_gla_in_kernel, n_ctx),
        grid=(b, n // TILE),
        in_specs=[
            pl.BlockSpec((None, TILE, d), lambda i, t: (i, t, 0)),
            pl.BlockSpec((1, d), lambda i, t: (0, 0)),
            pl.BlockSpec((2, 8, d), lambda i, t: (i, 0, 0)),
            pl.BlockSpec((d, nw), lambda i, t: (0, 0)),
            pl.BlockSpec((d, 128), lambda i, t: (0, 0)),
        ],
        out_specs=[
            pl.BlockSpec((None, TILE, nw), lambda i, t: (i, t, 0)),
            pl.BlockSpec((None, TILE, 128), lambda i, t: (i, t, 0)),
        ],
        out_shape=[
            jax.ShapeDtypeStruct((b, n, nw), BF16),
            jax.ShapeDtypeStruct((b, n, 128), F32),
        ],
        compiler_params=_params("arbitrary", "arbitrary"),
        name="gla_in",
    )(xs, g.reshape(1, d), mods, w_in.astype(BF16), w1)

    nt = n // TILE
    nct = n_ctx // TILE

    def rev_tile(s):
        return jnp.where(s < nct, nct - 1 - s, nt - 1 - (s - nct))

    qkv_w = 2 * kd + vd
    o_fwd, o_rev = pl.pallas_call(
        functools.partial(_gla_scan_kernel, dk, dv),
        grid=(b, nt),
        in_specs=[
            pl.BlockSpec((None, TILE, qkv_w), lambda i, s: (i, s, 0)),
            pl.BlockSpec((None, TILE, 128), lambda i, s: (i, s, 0)),
            pl.BlockSpec((None, TILE, qkv_w), lambda i, s: (i, rev_tile(s), 0)),
            pl.BlockSpec((None, TILE, 128), lambda i, s: (i, rev_tile(s), 0)),
            pl.BlockSpec(w2.shape, lambda i, s: (0, 0, 0)),
            pl.BlockSpec(gbias.shape, lambda i, s: (0, 0, 0)),
        ],
        out_specs=[
            pl.BlockSpec((None, TILE, vd), lambda i, s: (i, s, 0)),
            pl.BlockSpec((None, TILE, vd), lambda i, s: (i, rev_tile(s), 0)),
        ],
        out_shape=[jax.ShapeDtypeStruct((b, n, vd), F32), jax.ShapeDtypeStruct((b, n, vd), F32)],
        scratch_shapes=[pltpu.VMEM((2, GLA_HEADS, dk, dv), F32)],
        compiler_params=_params("arbitrary", "arbitrary"),
        name="gla_scan",
    )(qkvg, r, qkvg, r, w2, gbias)

    return pl.pallas_call(
        functools.partial(_gla_out_kernel, n_ctx),
        grid=(b, nt),
        in_specs=[
            pl.BlockSpec((None, TILE, d), lambda i, t: (i, t, 0)),
            pl.BlockSpec((None, TILE, vd), lambda i, t: (i, t, 0)),
            pl.BlockSpec((None, TILE, vd), lambda i, t: (i, t, 0)),
            pl.BlockSpec((None, TILE, vd), lambda i, t: (i, t, (2 * kd + vd) // vd)),
            pl.BlockSpec((1, dv), lambda i, t: (0, 0)),
            pl.BlockSpec((vd, d), lambda i, t: (0, 0)),
            pl.BlockSpec((2, 8, d), lambda i, t: (i, 0, 0)),
        ],
        out_specs=pl.BlockSpec((None, TILE, d), lambda i, t: (i, t, 0)),
        out_shape=jax.ShapeDtypeStruct(xs.shape, F32),
        input_output_aliases={0: 0},
        compiler_params=_params("arbitrary", "arbitrary"),
        name="gla_out",
    )(xs, o_fwd, o_rev, qkvg, o_norm.reshape(1, dv), w_o.astype(BF16), mods)


PEER_TT = 768
PEER_EB = 2048
PEER_SUB = 1024
PEER_RANKS = PEER_TOPK + 1


SUBLANES = 8


def _sorting_network(n):
    pairs = []
    p = 1
    while p < n:
        k = p
        while k >= 1:
            for j in range(k % p, n - k, 2 * k):
                for i in range(min(k, n - j - k)):
                    if (i + j) // (2 * p) == (i + j + k) // (2 * p):
                        pairs.append((i + j, i + j + k))
            k //= 2
        p *= 2
    return pairs


def _sublane_max_all(x):
    for shift in (4, 2, 1):
        x = jnp.maximum(x, pltpu.roll(x, shift, 0))
    return x


def _pop_top(lists, k):
    neg = jnp.full(lists[0].shape, -jnp.inf, F32)
    out = []
    for r in range(k):
        m = _sublane_max_all(lists[0])
        out.append(m)
        if r == k - 1:
            break
        hit = lists[0] == m
        keep = min(len(lists), k - 1 - r)
        lists = [jnp.where(hit, lists[p + 1] if p + 1 < len(lists) else neg, lists[p]) for p in range(keep)]
    return out


def _top_rows(s, k):
    tiles = [s[SUBLANES * v:SUBLANES * (v + 1), :] for v in range(s.shape[0] // SUBLANES)]
    for i, j in _sorting_network(len(tiles)):
        tiles[i], tiles[j] = jnp.maximum(tiles[i], tiles[j]), jnp.minimum(tiles[i], tiles[j])
    return _pop_top(tiles, k)


def _pair_threshold(a, b):
    k = PEER_RANKS
    lens = (k, k // 2, k // 3, k // 4, k - 4, k // 2 - 4, k // 3 - 4, 0)
    sub = lax.broadcasted_iota(jnp.int32, a[0].shape, 0)
    length = jnp.zeros(a[0].shape, jnp.int32)
    for c, ln in enumerate(lens):
        length = jnp.where(sub == c, ln, length)
    a_fix = jnp.where(sub == 0, a[0], jnp.where(sub == 1, a[1], jnp.where(sub == 2, a[2], a[3])))
    b_fix = jnp.where(sub == 4, b[0], jnp.where(sub == 5, b[1], b[2]))
    lists = []
    for p in range(k):
        a_p = jnp.where(sub < 4, a_fix, jnp.where(sub < 6, a[min(4 + p, k - 1)], a[4]))
        b_p = jnp.where(sub < 4, b[p], b_fix)
        lists.append(jnp.where(length > p, a_p + b_p, -jnp.inf))
    return _pop_top(lists, k)


def _peer_score_kernel(n_ctx, x_ref, g_ref, m_ref, wqt_ref, keys_ref, h_ref, e0_ref, e1_ref, th_ref, sc_ref):
    tt = x_ref.shape[0]
    is_ctx = _is_ctx(pl.program_id(1), tt, n_ctx)
    ht = _norm_mod(x_ref[...], g_ref[...], m_ref, 3, is_ctx).T.astype(BF16)
    h_ref[...] = ht
    qt = _dot(wqt_ref[...], ht).astype(BF16)
    dkey = keys_ref.shape[-1]
    for hp in range(2 * PEER_HEADS):
        sc_ref[hp] = _dot(keys_ref[hp], qt[hp * dkey:(hp + 1) * dkey, :])
    head_row = lax.broadcasted_iota(jnp.int32, th_ref.shape, 0)

    def head(hd, th_acc):
        th_parts = []
        for lt in range(tt // LANES):
            ls = slice(lt * LANES, (lt + 1) * LANES)
            s0 = sc_ref[2 * hd, :, ls]
            s1 = sc_ref[2 * hd + 1, :, ls]
            a = _top_rows(s0, PEER_RANKS)
            b = _top_rows(s1, PEER_RANKS)
            v = _pair_threshold(a, b)
            tau = 0.5 * (v[PEER_TOPK - 1] + v[PEER_TOPK])
            z = jnp.ones_like(v[0])
            for kk in range(1, PEER_TOPK):
                z = z + jnp.exp(v[kk] - v[0])
            rz = (1.0 / z)[0:1, :]
            e0_ref[hd, :, ls] = jnp.exp(s0 - a[0][0:1, :]) * rz
            e1_ref[hd, :, ls] = jnp.exp(s1 - b[0][0:1, :])
            th_parts.append(jnp.exp(tau - v[0])[0:1, :] * rz)
        return jnp.where(head_row == hd, jnp.concatenate(th_parts, axis=1), th_acc)

    th_ref[...] = lax.fori_loop(0, PEER_HEADS, head, jnp.zeros(th_ref.shape, F32))


def _gelu_tanh_sigmoid_form(x):
    c0 = -2.0 * 0.7978845608028654 * 1.4426950408889634
    c1 = c0 * 0.044715
    return x / (1.0 + jnp.exp2(x * (c0 + c1 * (x * x))))


def _peer_dense_kernel(n_ctx, sub, ht_ref, e0_ref, e1_ref, th_ref, u_ref, vt_ref, x_ref, m_ref, o_ref,
                       w_ref, acc_ref):
    e = pl.program_id(2)
    eb, tt = w_ref.shape
    nk = e1_ref.shape[1]

    @pl.when(e == 0)
    def _():
        acc_ref[...] = jnp.zeros_like(acc_ref)

    ht = ht_ref[...]
    for sb in range(eb // sub):
        a_sb = _dot(u_ref[sb * sub:(sb + 1) * sub, :], ht)
        for ii in range(sub // nk):
            i = sb * (sub // nk) + ii
            for lt in range(tt // LANES):
                ls = slice(lt * LANES, (lt + 1) * LANES)
                gsum = jnp.zeros((nk, LANES), F32)
                for hd in range(PEER_HEADS):
                    p = e0_ref[hd, i:i + 1, ls] * e1_ref[hd, :, ls]
                    gsum = gsum + jnp.where(p > th_ref[hd:hd + 1, ls], p, 0.0)
                act = _gelu_tanh_sigmoid_form(a_sb[ii * nk:(ii + 1) * nk, ls])
                w_ref[i * nk:(i + 1) * nk, ls] = (act * gsum).astype(BF16)
    acc_ref[...] += _dot(vt_ref[...], w_ref[...])

    @pl.when(e == pl.num_programs(2) - 1)
    def _():
        is_ctx = _is_ctx(pl.program_id(1), tt, n_ctx)
        o_ref[...] = x_ref[...] + _mod_row(m_ref, 5, is_ctx) * acc_ref[...].T


def peer_layer(xs, mods, g, n_ctx, w_q, keys, u_tab, v_tab, eb=PEER_EB, sub=PEER_SUB):
    b, n, d = xs.shape
    tt = PEER_TT
    ntt = n // tt
    ne = u_tab.shape[0]
    nk = PEER_NKEYS
    hq = w_q.shape[-1]
    wqt = jnp.transpose(w_q).astype(BF16)
    keys_b = keys.reshape(PEER_HEADS * 2, nk, keys.shape[-1]).astype(BF16)
    u_b = u_tab.astype(BF16)

    h2t, e0, e1, th = pl.pallas_call(
        functools.partial(_peer_score_kernel, n_ctx),
        grid=(b, ntt),
        in_specs=[
            pl.BlockSpec((None, tt, d), lambda i, t: (i, t, 0)),
            pl.BlockSpec((1, d), lambda i, t: (0, 0)),
            pl.BlockSpec((2, 8, d), lambda i, t: (i, 0, 0)),
            pl.BlockSpec((hq, d), lambda i, t: (0, 0)),
            pl.BlockSpec(keys_b.shape, lambda i, t: (0, 0, 0)),
        ],
        out_specs=[
            pl.BlockSpec((d, tt), lambda i, t: (0, i * ntt + t)),
            pl.BlockSpec((PEER_HEADS, nk, tt), lambda i, t: (0, 0, i * ntt + t)),
            pl.BlockSpec((PEER_HEADS, nk, tt), lambda i, t: (0, 0, i * ntt + t)),
            pl.BlockSpec((PEER_HEADS, tt), lambda i, t: (0, i * ntt + t)),
        ],
        out_shape=[
            jax.ShapeDtypeStruct((d, b * n), BF16),
            jax.ShapeDtypeStruct((PEER_HEADS, nk, b * n), F32),
            jax.ShapeDtypeStruct((PEER_HEADS, nk, b * n), F32),
            jax.ShapeDtypeStruct((PEER_HEADS, b * n), F32),
        ],
        scratch_shapes=[pltpu.VMEM((2 * PEER_HEADS, nk, tt), F32)],
        compiler_params=_params("arbitrary", "arbitrary"),
        name="peer_score",
    )(xs, g.reshape(1, d), mods, wqt, keys_b)

    nblk = ne // eb
    vt_b = jnp.transpose(v_tab.reshape(nblk, eb, d), (0, 2, 1)).astype(BF16)
    return pl.pallas_call(
        functools.partial(_peer_dense_kernel, n_ctx, sub),
        grid=(b, ntt, nblk),
        in_specs=[
            pl.BlockSpec((d, tt), lambda i, t, e: (0, i * ntt + t)),
            pl.BlockSpec((PEER_HEADS, eb // nk, tt), lambda i, t, e: (0, e, i * ntt + t)),
            pl.BlockSpec((PEER_HEADS, nk, tt), lambda i, t, e: (0, 0, i * ntt + t)),
            pl.BlockSpec((PEER_HEADS, tt), lambda i, t, e: (0, i * ntt + t)),
            pl.BlockSpec((eb, d), lambda i, t, e: (e, 0)),
            pl.BlockSpec((None, d, eb), lambda i, t, e: (e, 0, 0)),
            pl.BlockSpec((None, tt, d), lambda i, t, e: (i, t, 0)),
            pl.BlockSpec((2, 8, d), lambda i, t, e: (i, 0, 0)),
        ],
        out_specs=pl.BlockSpec((None, tt, d), lambda i, t, e: (i, t, 0)),
        out_shape=jax.ShapeDtypeStruct(xs.shape, F32),
        scratch_shapes=[
            pltpu.VMEM((eb, tt), BF16),
            pltpu.VMEM((d, tt), F32),
        ],
        input_output_aliases={6: 0},
        compiler_params=_params("arbitrary", "arbitrary", "arbitrary"),
        name="peer_dense",
    )(h2t, e0, e1, th, u_b, vt_b, xs, mods)


def _final_kernel(x_ref, g_ref, o_ref):
    o_ref[...] = _rms(x_ref[...], g_ref[...])


def final_norm(xs, g, n_ctx):
    b, n, d = xs.shape
    off = n_ctx // TILE
    return pl.pallas_call(
        _final_kernel,
        grid=(b, (n - n_ctx) // TILE),
        in_specs=[
            pl.BlockSpec((None, TILE, d), lambda i, t: (i, t + off, 0)),
            pl.BlockSpec((1, d), lambda i, t: (0, 0)),
        ],
        out_specs=pl.BlockSpec((None, TILE, d), lambda i, t: (i, t, 0)),
        out_shape=jax.ShapeDtypeStruct((b, n - n_ctx, d), F32),
        compiler_params=_params("arbitrary", "arbitrary"),
        name="final_norm",
    )(xs, g.reshape(1, d))


def kernel(x, c, ctx, c_ctx, norm_g, ada_w, ada_b, mla_w_in, mla_q_norm, mla_w_uq, mla_kv_norm, mla_w_ukv, mla_w_o, s5_lam_re, s5_lam_im, s5_b_re, s5_b_im, s5_c_re, s5_c_im, s5_log_step, s5_d, s5_w_glu, gla_w_in, gla_gk_w1, gla_gk_w2, gla_gk_b, gla_o_norm, gla_w_o, peer_w_q, peer_keys, peer_u, peer_v, final_g):
    b, n_lat, d = x.shape
    n_ctx = ctx.shape[1]
    depth = ada_w.shape[0]
    xs = jnp.concatenate([ctx, x], axis=1)

    r = 8 * ((b + 1 + 7) // 8)
    cond = jnp.zeros((r, d), F32).at[:b].set(c).at[b].set(c_ctx)
    ada = ada_all(cond, ada_w, ada_b).reshape(depth, r, 6, d)
    lat = ada[:, :b]
    ctxp = jnp.broadcast_to(ada[:, b:b + 1], lat.shape)
    mods_all = jnp.stack([ctxp, lat], axis=2)
    mods_all = jnp.pad(mods_all, ((0, 0), (0, 0), (0, 0), (0, 2), (0, 0))).reshape(depth, b * 2, 8, d)

    for i in range(depth):
        kind, j = i % N_MIXERS, i // N_MIXERS
        mods = mods_all[i]
        if kind == 0:
            xs = mixer_mla(xs, mods, norm_g[i, 0], n_ctx, i < depth - 1, mla_w_in[j], mla_q_norm[j], mla_w_uq[j],
                           mla_kv_norm[j], mla_w_ukv[j], mla_w_o[j])
        elif kind == 1:
            xs = mixer_s5(xs, mods, norm_g[i, 0], n_ctx, s5_lam_re[j], s5_lam_im[j], s5_b_re[j], s5_b_im[j],
                          s5_c_re[j], s5_c_im[j], s5_log_step[j], s5_d[j], s5_w_glu[j])
        else:
            xs = mixer_gla(xs, mods, norm_g[i, 0], n_ctx, gla_w_in[j], gla_gk_w1[j], gla_gk_w2[j],
                           gla_gk_b[j], gla_o_norm[j], gla_w_o[j])
        xs = peer_layer(xs, mods, norm_g[i, 1], n_ctx, peer_w_q[i], peer_keys[i], peer_u[i], peer_v[i])
    return final_norm(xs, final_g, n_ctx)
```

```python
import functools
import math

import numpy as np
import jax
import jax.numpy as jnp
from jax import lax
from jax.experimental import pallas as pl
from jax.experimental.pallas import tpu as pltpu

F32 = jnp.float32
BF16 = jnp.bfloat16

EPS = 1e-6
GRID_W = 64
N_MIXERS = 3

MLA_HEADS = 16
MLA_NOPE = 64
MLA_ROPE = 32
MLA_V = 64
MLA_Q_LORA = 384
MLA_KV_LORA = 256
ROPE_BASE = 10000.0
MLA_HEAD_PAD = 128
MLA_HEADS_PER_STEP = 2
MLA_Q_ROWS = 1024

S5_GROUP = 16
S5_STATE = 64
S5_GROUPS_PER_BLOCK = 8
S5_CHUNK = 256

GLA_HEADS = 4
GLA_GATE_RANK = 16
GLA_GATE_NORM = 16.0
GLA_CHUNK = 64

PEER_HEADS = 8
PEER_NKEYS = 128
PEER_TOPK = 16

LANES = 128
TILE = 256
VMEM_LIMIT = 56 * 1024 * 1024


def _params(*sem):
    return pltpu.CompilerParams(dimension_semantics=sem, vmem_limit_bytes=VMEM_LIMIT)


def _gelu_tanh(x):
    return 0.5 * x * (1.0 + jnp.tanh(0.7978845608028654 * (x + 0.044715 * x * x * x)))


def _is_ctx(tile_idx, rows, n_ctx):
    pos = tile_idx * rows + lax.broadcasted_iota(jnp.int32, (rows, 1), 0)
    return pos < n_ctx


def _mod_row(m_ref, row, is_ctx):
    return jnp.where(is_ctx, m_ref[0, row:row + 1, :], m_ref[1, row:row + 1, :])


def _rms(x, g):
    return x * lax.rsqrt(jnp.mean(x * x, axis=-1, keepdims=True) + EPS) * g


def _norm_mod(x, g, m_ref, row0, is_ctx):
    return _rms(x, g) * (1.0 + _mod_row(m_ref, row0 + 1, is_ctx)) + _mod_row(m_ref, row0, is_ctx)


def _dot(a, b):
    return jnp.dot(a, b, preferred_element_type=F32)


def _dot_nt(a, b):
    return lax.dot_general(a, b, (((1,), (1,)), ((), ())), preferred_element_type=F32)


def _ada_kernel(c_ref, w_ref, b_ref, o_ref):
    c = c_ref[...]
    s = (c * jax.nn.sigmoid(c)).astype(BF16)
    o_ref[...] = _dot(s, w_ref[...].astype(BF16)) + b_ref[...]


def ada_all(cond, ada_w, ada_b):
    depth, d, n6 = ada_w.shape
    r = cond.shape[0]
    tn = 1024
    return pl.pallas_call(
        _ada_kernel,
        grid=(depth, n6 // tn),
        in_specs=[
            pl.BlockSpec((r, d), lambda l, j: (0, 0)),
            pl.BlockSpec((None, d, tn), lambda l, j: (l, 0, j)),
            pl.BlockSpec((None, 1, tn), lambda l, j: (l, 0, j)),
        ],
        out_specs=pl.BlockSpec((None, r, tn), lambda l, j: (l, 0, j)),
        out_shape=jax.ShapeDtypeStruct((depth, r, n6), F32),
        compiler_params=_params("arbitrary", "arbitrary"),
        name="ada",
    )(cond, ada_w, ada_b.reshape(depth, 1, n6))


def _proj_residual_kernel(n_ctx, gate_row, tile_off, x_ref, y_ref, w_ref, m_ref, o_ref):
    is_ctx = _is_ctx(pl.program_id(1) + tile_off, x_ref.shape[0], n_ctx)
    f = _dot(y_ref[...], w_ref[...])
    o_ref[...] = x_ref[...] + _mod_row(m_ref, gate_row, is_ctx) * f


def proj_residual(xs, y, w, mods, n_ctx, gate_row, tile_off=0):
    b, n, d = xs.shape
    k = y.shape[-1]
    return pl.pallas_call(
        functools.partial(_proj_residual_kernel, n_ctx, gate_row, tile_off),
        grid=(b, y.shape[1] // TILE),
        in_specs=[
            pl.BlockSpec((None, TILE, d), lambda i, t: (i, t + tile_off, 0)),
            pl.BlockSpec((None, TILE, k), lambda i, t: (i, t, 0)),
            pl.BlockSpec((k, d), lambda i, t: (0, 0)),
            pl.BlockSpec((2, 8, d), lambda i, t: (i, 0, 0)),
        ],
        out_specs=pl.BlockSpec((None, TILE, d), lambda i, t: (i, t + tile_off, 0)),
        out_shape=jax.ShapeDtypeStruct(xs.shape, F32),
        input_output_aliases={0: 0},
        compiler_params=_params("arbitrary", "arbitrary"),
        name="proj_residual",
    )(xs, y, w, mods)


def _rope_tables(n_ctx, n_lat, scale):
    half = MLA_ROPE // 2
    rows_n = n_lat // GRID_W
    rows = np.repeat(np.arange(rows_n, dtype=np.float32), GRID_W)
    cols = np.tile(np.arange(GRID_W, dtype=np.float32), rows_n)
    inv = (ROPE_BASE ** (-np.arange(0, half, 2, dtype=np.float32) / half)).astype(np.float32)
    ang_r = rows[:, None] * inv
    ang_c = cols[:, None] * inv
    cos = np.concatenate([np.cos(ang_r), np.cos(ang_r), np.cos(ang_c), np.cos(ang_c)], axis=1)
    sin = np.concatenate([-np.sin(ang_r), np.sin(ang_r), -np.sin(ang_c), np.sin(ang_c)], axis=1)
    n = n_ctx + n_lat
    a = np.zeros((n, MLA_HEAD_PAD), np.float32)
    b = np.zeros((n, MLA_HEAD_PAD), np.float32)
    a[:, :MLA_NOPE] = 1.0
    a[:n_ctx, MLA_NOPE:MLA_NOPE + MLA_ROPE] = 1.0
    a[n_ctx:, MLA_NOPE:MLA_NOPE + MLA_ROPE] = cos
    b[n_ctx:, MLA_NOPE:MLA_NOPE + MLA_ROPE] = sin
    return jnp.asarray(a * scale), jnp.asarray(b * scale)


def _rope_swap_index():
    q = MLA_ROPE // 4
    base = np.arange(MLA_ROPE)
    return np.where((base % (2 * q)) < q, base + q, base - q)


def _mla_in_kernel(n_ctx, x_ref, g_ref, m_ref, wq_ref, wkv_ref, wkp_ref, qn_ref, kvn_ref,
                   wuq1_ref, wuq2_ref, wuk_ref, wuv_ref, aq_ref, bq_ref, ak_ref, bk_ref,
                   q_ref, k_ref, v_ref):
    is_ctx = _is_ctx(pl.program_id(1), x_ref.shape[0], n_ctx)
    hb = _norm_mod(x_ref[...], g_ref[...], m_ref, 0, is_ctx).astype(BF16)
    cq = _rms(_dot(hb, wq_ref[...]), qn_ref[...]).astype(BF16)
    ckv = _rms(_dot(hb, wkv_ref[...]), kvn_ref[...]).astype(BF16)
    kp = _dot(hb, wkp_ref[...])
    kpe = kp[:, :MLA_HEAD_PAD] * ak_ref[...] + kp[:, MLA_HEAD_PAD:] * bk_ref[...]
    y1 = _dot(cq, wuq1_ref[...])
    y2 = _dot(cq, wuq2_ref[...])
    kk = _dot(ckv, wuk_ref[...])
    aq = aq_ref[...]
    bq = bq_ref[...]
    for h in range(MLA_HEADS):
        sl = slice(h * MLA_HEAD_PAD, (h + 1) * MLA_HEAD_PAD)
        q_ref[:, sl] = (y1[:, sl] * aq + y2[:, sl] * bq).astype(BF16)
        k_ref[:, sl] = (kk[:, sl] + kpe).astype(BF16)
    v_ref[...] = _dot(ckv, wuv_ref[...]).astype(BF16)


def _mla_attn_kernel(q_ref, k_ref, v_ref, o_ref):
    heads = range(q_ref.shape[-1] // MLA_HEAD_PAD)
    ss = [_dot_nt(q_ref[:, h * MLA_HEAD_PAD:(h + 1) * MLA_HEAD_PAD],
                  k_ref[:, h * MLA_HEAD_PAD:(h + 1) * MLA_HEAD_PAD]) for h in heads]
    ps = [jnp.exp(s - jnp.max(s, axis=-1, keepdims=True)) for s in ss]
    ls = [jnp.sum(p, axis=-1, keepdims=True) for p in ps]
    outs = [_dot(ps[h].astype(BF16), v_ref[:, (h // 2) * 2 * MLA_V:(h // 2 + 1) * 2 * MLA_V]) / ls[h]
            for h in heads]
    lane = lax.broadcasted_iota(jnp.int32, outs[0].shape, 1)
    for pr in range(len(heads) // 2):
        o_ref[:, pr * 2 * MLA_V:(pr + 1) * 2 * MLA_V] = jnp.where(
            lane < MLA_V, outs[2 * pr], outs[2 * pr + 1]).astype(BF16)


def _mla_attention(q, row0, nq, k, v, n_keys, q_rows, hs):
    b = q.shape[0]
    hp = MLA_HEAD_PAD
    off = row0 // q_rows
    return pl.pallas_call(
        _mla_attn_kernel,
        grid=(b, MLA_HEADS // hs, nq // q_rows),
        in_specs=[
            pl.BlockSpec((None, q_rows, hs * hp), lambda i, h, t: (i, t + off, h)),
            pl.BlockSpec((None, n_keys, hs * hp), lambda i, h, t: (i, 0, h)),
            pl.BlockSpec((None, n_keys, hs * MLA_V), lambda i, h, t: (i, 0, h)),
        ],
        out_specs=pl.BlockSpec((None, q_rows, hs * MLA_V), lambda i, h, t: (i, t, h)),
        out_shape=jax.ShapeDtypeStruct((b, nq, MLA_HEADS * MLA_V), BF16),
        compiler_params=_params("arbitrary", "arbitrary", "arbitrary"),
        name="mla_attn",
    )(q, k, v)


def mixer_mla(xs, mods, g, n_ctx, ctx_out, w_in, q_norm, w_uq, kv_norm, w_ukv, w_o):
    b, n, d = xs.shape
    hp = MLA_HEAD_PAD
    dq = MLA_NOPE + MLA_ROPE
    wq = w_in[:, :MLA_Q_LORA].astype(BF16)
    wkv = w_in[:, MLA_Q_LORA:MLA_Q_LORA + MLA_KV_LORA].astype(BF16)
    w_pe = w_in[:, MLA_Q_LORA + MLA_KV_LORA:]
    swap = _rope_swap_index()
    wkp = jnp.zeros((d, 2 * hp), F32)
    wkp = wkp.at[:, MLA_NOPE:dq].set(w_pe).at[:, hp + MLA_NOPE:hp + dq].set(w_pe[:, swap]).astype(BF16)
    uq = w_uq.reshape(MLA_Q_LORA, MLA_HEADS, dq)
    z = jnp.zeros((MLA_Q_LORA, MLA_HEADS, hp - dq), F32)
    wuq1 = jnp.concatenate([uq, z], axis=-1).reshape(MLA_Q_LORA, MLA_HEADS * hp).astype(BF16)
    zn = jnp.zeros((MLA_Q_LORA, MLA_HEADS, MLA_NOPE), F32)
    wuq2 = jnp.concatenate([zn, uq[:, :, MLA_NOPE:][:, :, swap], z], axis=-1)
    wuq2 = wuq2.reshape(MLA_Q_LORA, MLA_HEADS * hp).astype(BF16)
    ukv = w_ukv.reshape(MLA_KV_LORA, MLA_HEADS, MLA_NOPE + MLA_V)
    zk = jnp.zeros((MLA_KV_LORA, MLA_HEADS, hp - MLA_NOPE), F32)
    wuk = jnp.concatenate([ukv[:, :, :MLA_NOPE], zk], axis=-1).reshape(MLA_KV_LORA, MLA_HEADS * hp).astype(BF16)
    wuv = ukv[:, :, MLA_NOPE:].reshape(MLA_KV_LORA, MLA_HEADS * MLA_V).astype(BF16)
    aq, bq = _rope_tables(n_ctx, n - n_ctx, float(dq) ** -0.5)
    ak, bk = _rope_tables(n_ctx, n - n_ctx, 1.0)

    full = lambda shape: pl.BlockSpec(shape, lambda i, t: tuple(0 for _ in shape))
    tab = pl.BlockSpec((TILE, hp), lambda i, t: (t, 0))
    nt, nct = n // TILE, n_ctx // TILE
    n_lat = n - n_ctx
    q, k, v = pl.pallas_call(
        functools.partial(_mla_in_kernel, n_ctx),
        grid=(b, n // TILE),
        in_specs=[
            pl.BlockSpec((None, TILE, d), lambda i, t: (i, t, 0)),
            full((1, d)),
            pl.BlockSpec((2, 8, d), lambda i, t: (i, 0, 0)),
            full(wq.shape), full(wkv.shape), full(wkp.shape),
            full((1, MLA_Q_LORA)), full((1, MLA_KV_LORA)),
            full(wuq1.shape), full(wuq2.shape), full(wuk.shape), full(wuv.shape),
            tab, tab, tab, tab,
        ],
        out_specs=[
            pl.BlockSpec((None, TILE, MLA_HEADS * hp), lambda i, t: (i, lax.rem(t + nt - nct, nt), 0)),
            pl.BlockSpec((None, TILE, MLA_HEADS * hp), lambda i, t: (i, t, 0)),
            pl.BlockSpec((None, TILE, MLA_HEADS * MLA_V), lambda i, t: (i, t, 0)),
        ],
        out_shape=[
            jax.ShapeDtypeStruct((b, n, MLA_HEADS * hp), BF16),
            jax.ShapeDtypeStruct((b, n, MLA_HEADS * hp), BF16),
            jax.ShapeDtypeStruct((b, n, MLA_HEADS * MLA_V), BF16),
        ],
        compiler_params=_params("arbitrary", "arbitrary"),
        name="mla_in",
    )(xs, g.reshape(1, d), mods, wq, wkv, wkp, q_norm.reshape(1, -1), kv_norm.reshape(1, -1),
      wuq1, wuq2, wuk, wuv, aq, bq, ak, bk)

    wo = w_o.astype(BF16)
    o_lat = _mla_attention(q, 0, n_lat, k, v, n, min(MLA_Q_ROWS, n_lat), MLA_HEADS_PER_STEP)
    xs = proj_residual(xs, o_lat, wo, mods, n_ctx, 2, tile_off=nct)
    if ctx_out:
        o_ctx = _mla_attention(q, n_lat, n_ctx, k, v, n_ctx, n_ctx, MLA_HEADS_PER_STEP)
        xs = proj_residual(xs, o_ctx, wo, mods, n_ctx, 2, tile_off=0)
    return xs


def _s5_disc_kernel(lr_ref, li_ref, ls_ref, bre_ref, bim_ref, ar_ref, ai_ref, br_ref, bi_ref):
    lr = lr_ref[...]
    li = li_ref[...]
    dt = jnp.exp(ls_ref[...])
    mag = jnp.exp(lr * dt)
    ar = mag * jnp.cos(li * dt)
    ai = mag * jnp.sin(li * dt)
    den = lr * lr + li * li
    fr = ((ar - 1.0) * lr + ai * li) / den
    fi = (ai * lr - (ar - 1.0) * li) / den
    ar_ref[...] = ar
    ai_ref[...] = ai
    for c in range(S5_GROUP):
        br_ref[c] = fr * bre_ref[c] - fi * bim_ref[c]
        bi_ref[c] = fr * bim_ref[c] + fi * bre_ref[c]


def _s5_in_kernel(n_ctx, x_ref, g_ref, m_ref, u_ref):
    is_ctx = _is_ctx(pl.program_id(1), x_ref.shape[0], n_ctx)
    u_ref[...] = _norm_mod(x_ref[...], g_ref[...], m_ref, 0, is_ctx).astype(BF16)


def _s5_scan_kernel(nb, u_ref, bcat_ref, ar_ref, ai_ref, ccat_ref, y_ref, bu_ref, xs_ref, st_ref):
    d = pl.program_id(0)
    half = ar_ref.shape[-1]

    @pl.when(pl.program_id(2) == 0)
    def _():
        st_ref[...] = jnp.zeros_like(st_ref)

    bu_ref[...] = _dot(u_ref[...], bcat_ref[...])
    ar = jnp.broadcast_to(ar_ref[...], (nb, half))
    ai = jnp.broadcast_to(ai_ref[...], (nb, half))
    steps = u_ref.shape[0] // nb

    def step(i, carry):
        xr, xi = carry
        tt = jnp.where(d == 0, i, steps - 1 - i)
        r0 = pl.multiple_of(tt * nb, nb)
        nxr = ar * xr - ai * xi + bu_ref[pl.ds(r0, nb), 0:half]
        nxi = ar * xi + ai * xr + bu_ref[pl.ds(r0, nb), half:2 * half]
        xs_ref[pl.ds(r0, nb), 0:half] = nxr.astype(BF16)
        xs_ref[pl.ds(r0, nb), half:2 * half] = nxi.astype(BF16)
        return nxr, nxi

    xr, xi = lax.fori_loop(0, steps, step, (st_ref[:, 0:half], st_ref[:, half:2 * half]), unroll=8)
    st_ref[:, 0:half] = xr
    st_ref[:, half:2 * half] = xi
    y_ref[...] = _dot(xs_ref[...], ccat_ref[...])


def _s5_glu_kernel(n_ctx, x_ref, g_ref, m_ref, y_ref, dsk_ref, w_ref, o_ref):
    is_ctx = _is_ctx(pl.program_id(1), x_ref.shape[0], n_ctx)
    x = x_ref[...]
    u = _norm_mod(x, g_ref[...], m_ref, 0, is_ctx)
    y = y_ref[0] + y_ref[1]
    z = _dot(_gelu_tanh(y + dsk_ref[...] * u).astype(BF16), w_ref[...])
    dm = z.shape[-1] // 2
    out = z[:, :dm] * jax.nn.sigmoid(z[:, dm:])
    o_ref[...] = x + _mod_row(m_ref, 2, is_ctx) * out


def mixer_s5(xs, mods, g, n_ctx, lam_re, lam_im, b_re, b_im, c_re, c_im, log_step, d_skip, w_glu):
    b, n, d = xs.shape
    groups = d // S5_GROUP
    p = S5_STATE
    gb = S5_GROUPS_PER_BLOCK
    nblk = groups // gb
    cin = gb * S5_GROUP
    half = gb * p

    full = lambda shape: pl.BlockSpec(shape, lambda dd: tuple(0 for _ in shape))
    ar, ai, br, bi = pl.pallas_call(
        _s5_disc_kernel,
        grid=(2,),
        in_specs=[
            pl.BlockSpec((None, groups, p), lambda dd: (dd, 0, 0)),
            pl.BlockSpec((None, groups, p), lambda dd: (dd, 0, 0)),
            pl.BlockSpec((None, groups, 1), lambda dd: (dd, 0, 0)),
            pl.BlockSpec((None, S5_GROUP, groups, p), lambda dd: (dd, 0, 0, 0)),
            pl.BlockSpec((None, S5_GROUP, groups, p), lambda dd: (dd, 0, 0, 0)),
        ],
        out_specs=[
            pl.BlockSpec((None, groups, p), lambda dd: (dd, 0, 0)),
            pl.BlockSpec((None, groups, p), lambda dd: (dd, 0, 0)),
            pl.BlockSpec((None, S5_GROUP, groups, p), lambda dd: (dd, 0, 0, 0)),
            pl.BlockSpec((None, S5_GROUP, groups, p), lambda dd: (dd, 0, 0, 0)),
        ],
        out_shape=[
            jax.ShapeDtypeStruct((2, groups, p), F32),
            jax.ShapeDtypeStruct((2, groups, p), F32),
            jax.ShapeDtypeStruct((2, S5_GROUP, groups, p), F32),
            jax.ShapeDtypeStruct((2, S5_GROUP, groups, p), F32),
        ],
        compiler_params=_params("arbitrary"),
        name="s5_disc",
    )(lam_re, lam_im, log_step.reshape(2, groups, 1),
      jnp.transpose(b_re, (0, 3, 1, 2)), jnp.transpose(b_im, (0, 3, 1, 2)))

    eye = jnp.eye(gb, dtype=F32)

    def in_blocks(t):
        t = t.reshape(2, S5_GROUP, nblk, gb, p)
        return jnp.einsum('ab,dcjap->djacbp', eye, t).reshape(2, nblk, cin, half)

    bcat = jnp.concatenate([in_blocks(br), in_blocks(bi)], axis=-1).astype(BF16)

    def out_blocks(t):
        t = t.reshape(2, nblk, gb, S5_GROUP, p)
        return jnp.einsum('ab,djacp->djapbc', eye, t).reshape(2, nblk, half, cin)

    ccat = jnp.concatenate([out_blocks(c_re), -out_blocks(c_im)], axis=2).astype(BF16)
    ar_b = ar.reshape(2, nblk, 1, half)
    ai_b = ai.reshape(2, nblk, 1, half)

    u_tm = pl.pallas_call(
        functools.partial(_s5_in_kernel, n_ctx),
        grid=(b, n // TILE),
        in_specs=[
            pl.BlockSpec((None, TILE, d), lambda i, t: (i, t, 0)),
            pl.BlockSpec((1, d), lambda i, t: (0, 0)),
            pl.BlockSpec((2, 8, d), lambda i, t: (i, 0, 0)),
        ],
        out_specs=pl.BlockSpec((TILE, d), lambda i, t: (t, i)),
        out_shape=jax.ShapeDtypeStruct((n, b * d), BF16),
        compiler_params=_params("arbitrary", "arbitrary"),
        name="s5_in",
    )(xs, g.reshape(1, d), mods)

    tc = S5_CHUNK
    nchunks = n // tc
    ncc = n_ctx // tc
    rows = tc * b

    def chunk_of(dd, s):
        rev = jnp.where(s < ncc, ncc - 1 - s, nchunks - 1 - (s - ncc))
        return jnp.where(dd == 0, s, rev)

    y2 = pl.pallas_call(
        functools.partial(_s5_scan_kernel, b),
        grid=(2, nblk, nchunks),
        in_specs=[
            pl.BlockSpec((rows, cin), lambda dd, j, s: (chunk_of(dd, s), j)),
            pl.BlockSpec((None, None, cin, 2 * half), lambda dd, j, s: (dd, j, 0, 0)),
            pl.BlockSpec((None, None, 1, half), lambda dd, j, s: (dd, j, 0, 0)),
            pl.BlockSpec((None, None, 1, half), lambda dd, j, s: (dd, j, 0, 0)),
            pl.BlockSpec((None, None, 2 * half, cin), lambda dd, j, s: (dd, j, 0, 0)),
        ],
        out_specs=pl.BlockSpec((None, rows, cin), lambda dd, j, s: (dd, chunk_of(dd, s), j)),
        out_shape=jax.ShapeDtypeStruct((2, n * b, d), F32),
        scratch_shapes=[
            pltpu.VMEM((rows, 2 * half), F32),
            pltpu.VMEM((rows, 2 * half), BF16),
            pltpu.VMEM((b, 2 * half), F32),
        ],
        compiler_params=_params("arbitrary", "arbitrary", "arbitrary"),
        name="s5_scan",
    )(u_tm.reshape(n * b, d), bcat, ar_b, ai_b, ccat)

    return pl.pallas_call(
        functools.partial(_s5_glu_kernel, n_ctx),
        grid=(b, n // TILE),
        in_specs=[
            pl.BlockSpec((None, TILE, d), lambda i, t: (i, t, 0)),
            pl.BlockSpec((1, d), lambda i, t: (0, 0)),
            pl.BlockSpec((2, 8, d), lambda i, t: (i, 0, 0)),
            pl.BlockSpec((2, TILE, d), lambda i, t: (0, t, i)),
            pl.BlockSpec((1, d), lambda i, t: (0, 0)),
            pl.BlockSpec((d, 2 * d), lambda i, t: (0, 0)),
        ],
        out_specs=pl.BlockSpec((None, TILE, d), lambda i, t: (i, t, 0)),
        out_shape=jax.ShapeDtypeStruct(xs.shape, F32),
        input_output_aliases={0: 0},
        compiler_params=_params("arbitrary", "arbitrary"),
        name="s5_glu",
    )(xs, g.reshape(1, d), mods, y2.reshape(2, n, b * d), d_skip.reshape(1, d), w_glu.astype(BF16))


def _gla_in_kernel(n_ctx, x_ref, g_ref, m_ref, w_ref, w1_ref, o_ref, r_ref):
    is_ctx = _is_ctx(pl.program_id(1), x_ref.shape[0], n_ctx)
    hb = _norm_mod(x_ref[...], g_ref[...], m_ref, 0, is_ctx).astype(BF16)
    o_ref[...] = _dot(hb, w_ref[...]).astype(BF16)
    r_ref[...] = _dot(hb, w1_ref[...])


def _gla_scan_kernel(dk, dv, qf_ref, rf_ref, qr_ref, rr_ref, w2_ref, gb_ref, of_ref, or_ref, s_ref):
    c = GLA_CHUNK
    nck = qf_ref.shape[0] // c
    kd = GLA_HEADS * dk

    @pl.when(pl.program_id(1) == 0)
    def _():
        s_ref[...] = jnp.zeros_like(s_ref)

    row = lax.broadcasted_iota(jnp.int32, (c, c), 0)
    col = lax.broadcasted_iota(jnp.int32, (c, c), 1)
    masks = (row >= col, row <= col)

    refs = ((qf_ref, rf_ref, of_ref), (qr_ref, rr_ref, or_ref))
    chains = [(dd, h) for dd in range(2) for h in range(GLA_HEADS)]
    for i in range(nck):
        rows = [slice(i * c, (i + 1) * c), slice((nck - 1 - i) * c, (nck - i) * c)]
        decay = []
        for dd in range(2):
            rb = refs[dd][1][rows[dd], :].astype(BF16)
            lg = jax.nn.log_sigmoid(_dot(rb, w2_ref[dd]) + gb_ref[dd]) * (1.0 / GLA_GATE_NORM)
            bcum = jnp.dot(masks[dd].astype(F32), lg, preferred_element_type=F32,
                           precision=lax.Precision.HIGHEST)
            blast = jnp.sum(lg, axis=0, keepdims=True)
            decay.append((jnp.exp(bcum), jnp.exp(-bcum), jnp.exp(blast - bcum), jnp.exp(blast)))
        qg, att, kdec, vv, st = {}, {}, {}, {}, {}
        for ch in chains:
            dd, h = ch
            x_ref = refs[dd][0]
            hs = slice(h * dk, (h + 1) * dk)
            q = x_ref[rows[dd], hs].astype(F32) * (float(dk) ** -0.5)
            k = x_ref[rows[dd], kd + h * dk:kd + (h + 1) * dk].astype(F32)
            vv[ch] = x_ref[rows[dd], 2 * kd + h * dv:2 * kd + (h + 1) * dv]
            qg[ch] = (q * decay[dd][0][:, hs]).astype(BF16)
            kg = (k * decay[dd][1][:, hs]).astype(BF16)
            kdec[ch] = k * decay[dd][2][:, hs]
            att[ch] = jnp.where(masks[dd], _dot_nt(qg[ch], kg), 0.0).astype(BF16)
        for ch in chains:
            dd, h = ch
            st[ch] = s_ref[dd, h]
            refs[dd][2][rows[dd], h * dv:(h + 1) * dv] = (
                _dot(att[ch], vv[ch]) + _dot(qg[ch], st[ch].astype(BF16)))
        for ch in chains:
            dd, h = ch
            eb = jnp.broadcast_to(decay[dd][3][:, h * dk:(h + 1) * dk], (dk - c, dk))
            mt = jnp.concatenate([kdec[ch], eb], axis=0).T
            s_ref[dd, h] = mt[:, c:c + 1] * st[ch] + _dot(mt[:, 0:c].astype(BF16), vv[ch])


def _gla_out_kernel(n_ctx, x_ref, of_ref, or_ref, gt_ref, on_ref, w_ref, m_ref, o_ref):
    is_ctx = _is_ctx(pl.program_id(1), x_ref.shape[0], n_ctx)
    o = of_ref[...] + or_ref[...]
    gt = gt_ref[...].astype(F32)
    dv = on_ref.shape[-1]
    parts = []
    for h in range(GLA_HEADS):
        sl = slice(h * dv, (h + 1) * dv)
        parts.append(_rms(o[:, sl], on_ref[...]) * (gt[:, sl] * jax.nn.sigmoid(gt[:, sl])))
    y = jnp.concatenate(parts, axis=-1).astype(BF16)
    o_ref[...] = x_ref[...] + _mod_row(m_ref, 2, is_ctx) * _dot(y, w_ref[...])


def mixer_gla(xs, mods, g, n_ctx, w_in, gk_w1, gk_w2, gk_b, o_norm, w_o):
    b, n, d = xs.shape
    kd = d // 2
    vd = d
    dk = kd // GLA_HEADS
    dv = vd // GLA_HEADS
    rk = GLA_GATE_RANK
    w1 = jnp.zeros((d, 128), F32).at[:, :rk].set(gk_w1[0]).at[:, rk:2 * rk].set(gk_w1[1]).astype(BF16)
    w2 = jnp.zeros((2, 128, kd), F32)
    for dd in range(2):
        w2 = w2.at[dd, dd * rk:(dd + 1) * rk, :].set(gk_w2[dd])
    w2 = w2.astype(BF16)
    gbias = gk_b.reshape(2, 1, kd)
    nw = w_in.shape[-1]

    qkvg, r = pl.pallas_call(
        functools.partial(_gla_in_kernel, n_ctx),
        grid=(b, n // TILE),
        in_specs=[
            pl.BlockSpec((None, TILE, d), lambda i, t: (i, t, 0)),
            pl.BlockSpec((1, d), lambda i, t: (0, 0)),
            pl.BlockSpec((2, 8, d), lambda i, t: (i, 0, 0)),
            pl.BlockSpec((d, nw), lambda i, t: (0, 0)),
            pl.BlockSpec((d, 128), lambda i, t: (0, 0)),
        ],
        out_specs=[
            pl.BlockSpec((None, TILE, nw), lambda i, t: (i, t, 0)),
            pl.BlockSpec((None, TILE, 128), lambda i, t: (i, t, 0)),
        ],
        out_shape=[
            jax.ShapeDtypeStruct((b, n, nw), BF16),
            jax.ShapeDtypeStruct((b, n, 128), F32),
        ],
        compiler_params=_params("arbitrary", "arbitrary"),
        name="gla_in",
    )(xs, g.reshape(1, d), mods, w_in.astype(BF16), w1)

    nt = n // TILE
    nct = n_ctx // TILE

    def rev_tile(s):
        return jnp.where(s < nct, nct - 1 - s, nt - 1 - (s - nct))

    qkv_w = 2 * kd + vd
    o_fwd, o_rev = pl.pallas_call(
        functools.partial(_gla_scan_kernel, dk, dv),
        grid=(b, nt),
        in_specs=[
            pl.BlockSpec((None, TILE, qkv_w), lambda i, s: (i, s, 0)),
            pl.BlockSpec((None, TILE, 128), lambda i, s: (i, s, 0)),
            pl.BlockSpec((None, TILE, qkv_w), lambda i, s: (i, rev_tile(s), 0)),
            pl.BlockSpec((None, TILE, 128), lambda i, s: (i, rev_tile(s), 0)),
            pl.BlockSpec(w2.shape, lambda i, s: (0, 0, 0)),
            pl.BlockSpec(gbias.shape, lambda i, s: (0, 0, 0)),
        ],
        out_specs=[
            pl.BlockSpec((None, TILE, vd), lambda i, s: (i, s, 0)),
            pl.BlockSpec((None, TILE, vd), lambda i, s: (i, rev_tile(s), 0)),
        ],
        out_shape=[jax.ShapeDtypeStruct((b, n, vd), F32), jax.ShapeDtypeStruct((b, n, vd), F32)],
        scratch_shapes=[pltpu.VMEM((2, GLA_HEADS, dk, dv), F32)],
        compiler_params=_params("arbitrary", "arbitrary"),
        name="gla_scan",
    )(qkvg, r, qkvg, r, w2, gbias)

    return pl.pallas_call(
        functools.partial(_gla_out_kernel, n_ctx),
        grid=(b, nt),
        in_specs=[
            pl.BlockSpec((None, TILE, d), lambda i, t: (i, t, 0)),
            pl.BlockSpec((None, TILE, vd), lambda i, t: (i, t, 0)),
            pl.BlockSpec((None, TILE, vd), lambda i, t: (i, t, 0)),
            pl.BlockSpec((None, TILE, vd), lambda i, t: (i, t, (2 * kd + vd) // vd)),
            pl.BlockSpec((1, dv), lambda i, t: (0, 0)),
            pl.BlockSpec((vd, d), lambda i, t: (0, 0)),
            pl.BlockSpec((2, 8, d), lambda i, t: (i, 0, 0)),
        ],
        out_specs=pl.BlockSpec((None, TILE, d), lambda i, t: (i, t, 0)),
        out_shape=jax.ShapeDtypeStruct(xs.shape, F32),
        input_output_aliases={0: 0},
        compiler_params=_params("arbitrary", "arbitrary"),
        name="gla_out",
    )(xs, o_fwd, o_rev, qkvg, o_norm.reshape(1, dv), w_o.astype(BF16), mods)


PEER_TT = 768
PEER_EB = 2048
PEER_SUB = 1024
PEER_RANKS = PEER_TOPK + 1


SUBLANES = 8


def _sorting_network(n):
    pairs = []
    p = 1
    while p < n:
        k = p
        while k >= 1:
            for j in range(k % p, n - k, 2 * k):
                for i in range(min(k, n - j - k)):
                    if (i + j) // (2 * p) == (i + j + k) // (2 * p):
                        pairs.append((i + j, i + j + k))
            k //= 2
        p *= 2
    return pairs


def _sublane_max_all(x):
    for shift in (4, 2, 1):
        x = jnp.maximum(x, pltpu.roll(x, shift, 0))
    return x


def _pop_top(lists, k):
    neg = jnp.full(lists[0].shape, -jnp.inf, F32)
    out = []
    for r in range(k):
        m = _sublane_max_all(lists[0])
        out.append(m)
        if r == k - 1:
            break
        hit = lists[0] == m
        keep = min(len(lists), k - 1 - r)
        lists = [jnp.where(hit, lists[p + 1] if p + 1 < len(lists) else neg, lists[p]) for p in range(keep)]
    return out


def _top_rows(s, k):
    tiles = [s[SUBLANES * v:SUBLANES * (v + 1), :] for v in range(s.shape[0] // SUBLANES)]
    for i, j in _sorting_network(len(tiles)):
        tiles[i], tiles[j] = jnp.maximum(tiles[i], tiles[j]), jnp.minimum(tiles[i], tiles[j])
    return _pop_top(tiles, k)


def _pair_threshold(a, b):
    k = PEER_RANKS
    lens = (k, k // 2, k // 3, k // 4, k - 4, k // 2 - 4, k // 3 - 4, 0)
    sub = lax.broadcasted_iota(jnp.int32, a[0].shape, 0)
    length = jnp.zeros(a[0].shape, jnp.int32)
    for c, ln in enumerate(lens):
        length = jnp.where(sub == c, ln, length)
    a_fix = jnp.where(sub == 0, a[0], jnp.where(sub == 1, a[1], jnp.where(sub == 2, a[2], a[3])))
    b_fix = jnp.where(sub == 4, b[0], jnp.where(sub == 5, b[1], b[2]))
    lists = []
    for p in range(k):
        a_p = jnp.where(sub < 4, a_fix, jnp.where(sub < 6, a[min(4 + p, k - 1)], a[4]))
        b_p = jnp.where(sub < 4, b[p], b_fix)
        lists.append(jnp.where(length > p, a_p + b_p, -jnp.inf))
    return _pop_top(lists, k)


def _peer_score_kernel(n_ctx, x_ref, g_ref, m_ref, wqt_ref, keys_ref, h_ref, e0_ref, e1_ref, th_ref, sc_ref):
    tt = x_ref.shape[0]
    is_ctx = _is_ctx(pl.program_id(1), tt, n_ctx)
    ht = _norm_mod(x_ref[...], g_ref[...], m_ref, 3, is_ctx).T.astype(BF16)
    h_ref[...] = ht
    qt = _dot(wqt_ref[...], ht).astype(BF16)
    dkey = keys_ref.shape[-1]
    for hp in range(2 * PEER_HEADS):
        sc_ref[hp] = _dot(keys_ref[hp], qt[hp * dkey:(hp + 1) * dkey, :])
    head_row = lax.broadcasted_iota(jnp.int32, th_ref.shape, 0)

    def head(hd, th_acc):
        th_parts = []
        for lt in range(tt // LANES):
            ls = slice(lt * LANES, (lt + 1) * LANES)
            s0 = sc_ref[2 * hd, :, ls]
            s1 = sc_ref[2 * hd + 1, :, ls]
            a = _top_rows(s0, PEER_RANKS)
            b = _top_rows(s1, PEER_RANKS)
            v = _pair_threshold(a, b)
            tau = 0.5 * (v[PEER_TOPK - 1] + v[PEER_TOPK])
            z = jnp.ones_like(v[0])
            for kk in range(1, PEER_TOPK):
                z = z + jnp.exp(v[kk] - v[0])
            rz = (1.0 / z)[0:1, :]
            e0_ref[hd, :, ls] = jnp.exp(s0 - a[0][0:1, :]) * rz
            e1_ref[hd, :, ls] = jnp.exp(s1 - b[0][0:1, :])
            th_parts.append(jnp.exp(tau - v[0])[0:1, :] * rz)
        return jnp.where(head_row == hd, jnp.concatenate(th_parts, axis=1), th_acc)

    th_ref[...] = lax.fori_loop(0, PEER_HEADS, head, jnp.zeros(th_ref.shape, F32))


def _gelu_tanh_sigmoid_form(x):
    c0 = -2.0 * 0.7978845608028654 * 1.4426950408889634
    c1 = c0 * 0.044715
    return x / (1.0 + jnp.exp2(x * (c0 + c1 * (x * x))))


def _peer_dense_kernel(n_ctx, sub, ht_ref, e0_ref, e1_ref, th_ref, u_ref, vt_ref, x_ref, m_ref, o_ref,
                       w_ref, acc_ref):
    e = pl.program_id(2)
    eb, tt = w_ref.shape
    nk = e1_ref.shape[1]

    @pl.when(e == 0)
    def _():
        acc_ref[...] = jnp.zeros_like(acc_ref)

    ht = ht_ref[...]
    for sb in range(eb // sub):
        a_sb = _dot(u_ref[sb * sub:(sb + 1) * sub, :], ht)
        for ii in range(sub // nk):
            i = sb * (sub // nk) + ii
            for lt in range(tt // LANES):
                ls = slice(lt * LANES, (lt + 1) * LANES)
                gsum = jnp.zeros((nk, LANES), F32)
                for hd in range(PEER_HEADS):
                    p = e0_ref[hd, i:i + 1, ls] * e1_ref[hd, :, ls]
                    gsum = gsum + jnp.where(p > th_ref[hd:hd + 1, ls], p, 0.0)
                act = _gelu_tanh_sigmoid_form(a_sb[ii * nk:(ii + 1) * nk, ls])
                w_ref[i * nk:(i + 1) * nk, ls] = (act * gsum).astype(BF16)
    acc_ref[...] += _dot(vt_ref[...], w_ref[...])

    @pl.when(e == pl.num_programs(2) - 1)
    def _():
        is_ctx = _is_ctx(pl.program_id(1), tt, n_ctx)
        o_ref[...] = x_ref[...] + _mod_row(m_ref, 5, is_ctx) * acc_ref[...].T


def peer_layer(xs, mods, g, n_ctx, w_q, keys, u_tab, v_tab, eb=PEER_EB, sub=PEER_SUB):
    b, n, d = xs.shape
    tt = PEER_TT
    ntt = n // tt
    ne = u_tab.shape[0]
    nk = PEER_NKEYS
    hq = w_q.shape[-1]
    wqt = jnp.transpose(w_q).astype(BF16)
    keys_b = keys.reshape(PEER_HEADS * 2, nk, keys.shape[-1]).astype(BF16)
    u_b = u_tab.astype(BF16)

    h2t, e0, e1, th = pl.pallas_call(
        functools.partial(_peer_score_kernel, n_ctx),
        grid=(b, ntt),
        in_specs=[
            pl.BlockSpec((None, tt, d), lambda i, t: (i, t, 0)),
            pl.BlockSpec((1, d), lambda i, t: (0, 0)),
            pl.BlockSpec((2, 8, d), lambda i, t: (i, 0, 0)),
            pl.BlockSpec((hq, d), lambda i, t: (0, 0)),
            pl.BlockSpec(keys_b.shape, lambda i, t: (0, 0, 0)),
        ],
        out_specs=[
            pl.BlockSpec((d, tt), lambda i, t: (0, i * ntt + t)),
            pl.BlockSpec((PEER_HEADS, nk, tt), lambda i, t: (0, 0, i * ntt + t)),
            pl.BlockSpec((PEER_HEADS, nk, tt), lambda i, t: (0, 0, i * ntt + t)),
            pl.BlockSpec((PEER_HEADS, tt), lambda i, t: (0, i * ntt + t)),
        ],
        out_shape=[
            jax.ShapeDtypeStruct((d, b * n), BF16),
            jax.ShapeDtypeStruct((PEER_HEADS, nk, b * n), F32),
            jax.ShapeDtypeStruct((PEER_HEADS, nk, b * n), F32),
            jax.ShapeDtypeStruct((PEER_HEADS, b * n), F32),
        ],
        scratch_shapes=[pltpu.VMEM((2 * PEER_HEADS, nk, tt), F32)],
        compiler_params=_params("arbitrary", "arbitrary"),
        name="peer_score",
    )(xs, g.reshape(1, d), mods, wqt, keys_b)

    nblk = ne // eb
    vt_b = jnp.transpose(v_tab.reshape(nblk, eb, d), (0, 2, 1)).astype(BF16)
    return pl.pallas_call(
        functools.partial(_peer_dense_kernel, n_ctx, sub),
        grid=(b, ntt, nblk),
        in_specs=[
            pl.BlockSpec((d, tt), lambda i, t, e: (0, i * ntt + t)),
            pl.BlockSpec((PEER_HEADS, eb // nk, tt), lambda i, t, e: (0, e, i * ntt + t)),
            pl.BlockSpec((PEER_HEADS, nk, tt), lambda i, t, e: (0, 0, i * ntt + t)),
            pl.BlockSpec((PEER_HEADS, tt), lambda i, t, e: (0, i * ntt + t)),
            pl.BlockSpec((eb, d), lambda i, t, e: (e, 0)),
            pl.BlockSpec((None, d, eb), lambda i, t, e: (e, 0, 0)),
            pl.BlockSpec((None, tt, d), lambda i, t, e: (i, t, 0)),
            pl.BlockSpec((2, 8, d), lambda i, t, e: (i, 0, 0)),
        ],
        out_specs=pl.BlockSpec((None, tt, d), lambda i, t, e: (i, t, 0)),
        out_shape=jax.ShapeDtypeStruct(xs.shape, F32),
        scratch_shapes=[
            pltpu.VMEM((eb, tt), BF16),
            pltpu.VMEM((d, tt), F32),
        ],
        input_output_aliases={6: 0},
        compiler_params=_params("arbitrary", "arbitrary", "arbitrary"),
        name="peer_dense",
    )(h2t, e0, e1, th, u_b, vt_b, xs, mods)


def _final_kernel(x_ref, g_ref, o_ref):
    o_ref[...] = _rms(x_ref[...], g_ref[...])


def final_norm(xs, g, n_ctx):
    b, n, d = xs.shape
    off = n_ctx // TILE
    return pl.pallas_call(
        _final_kernel,
        grid=(b, (n - n_ctx) // TILE),
        in_specs=[
            pl.BlockSpec((None, TILE, d), lambda i, t: (i, t + off, 0)),
            pl.BlockSpec((1, d), lambda i, t: (0, 0)),
        ],
        out_specs=pl.BlockSpec((None, TILE, d), lambda i, t: (i, t, 0)),
        out_shape=jax.ShapeDtypeStruct((b, n - n_ctx, d), F32),
        compiler_params=_params("arbitrary", "arbitrary"),
        name="final_norm",
    )(xs, g.reshape(1, d))


def kernel(x, c, ctx, c_ctx, norm_g, ada_w, ada_b, mla_w_in, mla_q_norm, mla_w_uq, mla_kv_norm, mla_w_ukv, mla_w_o, s5_lam_re, s5_lam_im, s5_b_re, s5_b_im, s5_c_re, s5_c_im, s5_log_step, s5_d, s5_w_glu, gla_w_in, gla_gk_w1, gla_gk_w2, gla_gk_b, gla_o_norm, gla_w_o, peer_w_q, peer_keys, peer_u, peer_v, final_g):
    b, n_lat, d = x.shape
    n_ctx = ctx.shape[1]
    depth = ada_w.shape[0]
    xs = jnp.concatenate([ctx, x], axis=1)

    r = 8 * ((b + 1 + 7) // 8)
    cond = jnp.zeros((r, d), F32).at[:b].set(c).at[b].set(c_ctx)
    ada = ada_all(cond, ada_w, ada_b).reshape(depth, r, 6, d)
    lat = ada[:, :b]
    ctxp = jnp.broadcast_to(ada[:, b:b + 1], lat.shape)
    mods_all = jnp.stack([ctxp, lat], axis=2)
    mods_all = jnp.pad(mods_all, ((0, 0), (0, 0), (0, 0), (0, 2), (0, 0))).reshape(depth, b * 2, 8, d)

    for i in range(depth):
        kind, j = i % N_MIXERS, i // N_MIXERS
        mods = mods_all[i]
        if kind == 0:
            xs = mixer_mla(xs, mods, norm_g[i, 0], n_ctx, i < depth - 1, mla_w_in[j], mla_q_norm[j], mla_w_uq[j],
                           mla_kv_norm[j], mla_w_ukv[j], mla_w_o[j])
        elif kind == 1:
            xs = mixer_s5(xs, mods, norm_g[i, 0], n_ctx, s5_lam_re[j], s5_lam_im[j], s5_b_re[j], s5_b_im[j],
                          s5_c_re[j], s5_c_im[j], s5_log_step[j], s5_d[j], s5_w_glu[j])
        else:
            xs = mixer_gla(xs, mods, norm_g[i, 0], n_ctx, gla_w_in[j], gla_gk_w1[j], gla_gk_w2[j],
                           gla_gk_b[j], gla_o_norm[j], gla_w_o[j])
        xs = peer_layer(xs, mods, norm_g[i, 1], n_ctx, peer_w_q[i], peer_keys[i], peer_u[i], peer_v[i])
    return final_norm(xs, final_g, n_ctx)
```
